```python
import math
import jax, jax.numpy as jnp
from jax import lax
import numpy as np

D_MODEL = 1024
BATCH = 8
SEQ = 2048
DEPTH = 4

D_RNN = D_MODEL
RNN_HEADS = 8
RNN_BLOCK = D_RNN // RNN_HEADS
CONV_WIDTH = 4
LRU_C = 8.0
SSM_D_INNER = D_MODEL
SSM_HEAD_DIM = 64
SSM_HEADS = SSM_D_INNER // SSM_HEAD_DIM
SSM_GROUPS = 4
SSM_HEADS_PER_GROUP = SSM_HEADS // SSM_GROUPS
SSM_STATE = 128
SSM_CHUNK = 128
SSM_CONV_CH = SSM_D_INNER + 2 * SSM_GROUPS * SSM_STATE
ATTN_HEADS = 16
ATTN_HEAD_DIM = 64
D_ATTN = ATTN_HEADS * ATTN_HEAD_DIM
Q_BLOCK = 128
N_BRANCHES = 3
N_EXPERTS = 16
N_EXPERT_GROUPS = 4
EXPERTS_PER_GROUP = N_EXPERTS // N_EXPERT_GROUPS
TOP_K = 2
D_EXPERT = 512
LN_EPS = 1e-5
RMS_EPS = 1e-6
DEEPNORM_ALPHA = (2 * DEPTH) ** 0.25
DEEPNORM_BETA = (8 * DEPTH) ** -0.25

SPLIT_SIZES = (D_RNN, D_RNN, SSM_D_INNER, SSM_CONV_CH, SSM_HEADS, 3 * D_ATTN, ATTN_HEADS, N_BRANCHES * D_MODEL)
SPLIT_IDX = tuple(int(v) for v in np.cumsum(SPLIT_SIZES)[:-1])
P_IN = int(sum(SPLIT_SIZES))

kernel_name = 'hybrid_rglru_ssd_fox_groupmoe_deepnorm'


def layer_norm(x, g, b):
    xf = x.astype(jnp.float32)
    mu = jnp.mean(xf, axis=-1, keepdims=True)
    var = jnp.mean(jnp.square(xf - mu), axis=-1, keepdims=True)
    return ((xf - mu) * lax.rsqrt(var + LN_EPS) * g + b).astype(x.dtype)


def causal_depthwise_conv(x, w, b):
    ch = x.shape[-1]
    y = lax.conv_general_dilated(x, w[:, None, :], window_strides=(1,), padding=[(CONV_WIDTH - 1, 0)],
                                 dimension_numbers=('NWC', 'WIO', 'NWC'), feature_group_count=ch)
    return y + b


def rg_lru(x, w_a, b_a, w_x, b_x, lam):
    bsz, s, _ = x.shape
    xh = x.reshape(bsz, s, RNN_HEADS, RNN_BLOCK)
    r_gate = jax.nn.sigmoid(jnp.einsum('bshi,hij->bshj', xh, w_a).reshape(bsz, s, D_RNN) + b_a)
    i_gate = jax.nn.sigmoid(jnp.einsum('bshi,hij->bshj', xh, w_x).reshape(bsz, s, D_RNN) + b_x)
    log_a = (-LRU_C * r_gate.astype(jnp.float32) * jax.nn.softplus(-lam.astype(jnp.float32)))
    a = jnp.exp(log_a)
    mult = jnp.sqrt(-jnp.expm1(2.0 * log_a))
    u = (x * i_gate).astype(jnp.float32) * mult

    def combine(lhs, rhs):
        a1, b1 = lhs
        a2, b2 = rhs
        return a1 * a2, a2 * b1 + b2

    _, h = lax.associative_scan(combine, (a, u), axis=1)
    return h.astype(x.dtype)


def recurrent_branch(x_raw, gate_raw, conv_w, conv_b, w_a, b_a, w_x, b_x, lam):
    xa = causal_depthwise_conv(x_raw, conv_w, conv_b)
    h = rg_lru(xa, w_a, b_a, w_x, b_x, lam)
    return h * jax.nn.gelu(gate_raw)


def segsum(a):
    t = a.shape[-1]
    rep = jnp.broadcast_to(a[..., :, None], a.shape + (t,))
    rep = jnp.where(jnp.tril(jnp.ones((t, t), dtype=bool), -1), rep, 0.0)
    ss = jnp.cumsum(rep, axis=-2)
    return jnp.where(jnp.tril(jnp.ones((t, t), dtype=bool)), ss, -jnp.inf)


def ssd_branch(z, xbc_raw, dt_raw, conv_w, conv_b, dt_bias, a_log, d_skip, norm_w):
    bsz, s, _ = z.shape
    nc = s // SSM_CHUNK
    g_, e_, p_, n_ = SSM_GROUPS, SSM_HEADS_PER_GROUP, SSM_HEAD_DIM, SSM_STATE
    xbc = jax.nn.silu(causal_depthwise_conv(xbc_raw, conv_w, conv_b))
    xs, bm, cm = jnp.split(xbc, [SSM_D_INNER, SSM_D_INNER + g_ * n_], axis=-1)
    dt = jax.nn.softplus((dt_raw + dt_bias).astype(jnp.float32))
    a_neg = -jnp.exp(a_log.astype(jnp.float32)).reshape(g_, e_)
    xh = xs.astype(jnp.float32).reshape(bsz, nc, SSM_CHUNK, g_, e_, p_)
    dth = dt.reshape(bsz, nc, SSM_CHUNK, g_, e_)
    xdt = xh * dth[..., None]
    bc = bm.astype(jnp.float32).reshape(bsz, nc, SSM_CHUNK, g_, n_)
    cc = cm.astype(jnp.float32).reshape(bsz, nc, SSM_CHUNK, g_, n_)
    a_dt = jnp.moveaxis(dth * a_neg, 2, -1)
    a_cs = jnp.cumsum(a_dt, axis=-1)
    decay_in = jnp.exp(segsum(a_dt))
    cb = jnp.einsum('bclgn,bcsgn->bcgls', cc, bc)
    y_diag = jnp.einsum('bcgls,bcgels,bcsgep->bclgep', cb, decay_in, xdt)
    decay_states = jnp.exp(a_cs[..., -1:] - a_cs)
    states = jnp.einsum('bclgn,bcgel,bclgep->bcgepn', bc, decay_states, xdt)
    chunk_tot = jnp.pad(jnp.moveaxis(a_cs[..., -1], 1, -1), ((0, 0), (0, 0), (0, 0), (1, 0)))
    decay_chunk = jnp.exp(segsum(chunk_tot))
    states = jnp.concatenate([jnp.zeros_like(states[:, :1]), states], axis=1)
    states = jnp.einsum('bgezc,bcgepn->bzgepn', decay_chunk, states)[:, :-1]
    y_off = jnp.einsum('bclgn,bcgepn,bcgel->bclgep', cc, states, jnp.exp(a_cs))
    y = y_diag + y_off + xh * d_skip.astype(jnp.float32).reshape(g_, e_)[:, :, None]
    y = y.reshape(bsz, s, SSM_D_INNER)
    gy = (y * jax.nn.silu(z.astype(jnp.float32))).reshape(bsz, s, g_, SSM_D_INNER // g_)
    gy = gy * lax.rsqrt(jnp.mean(jnp.square(gy), axis=-1, keepdims=True) + RMS_EPS)
    return (gy.reshape(bsz, s, SSM_D_INNER) * norm_w).astype(z.dtype)


def forgetting_attention(qkv, f_raw, forget_b):
    bsz, s, _ = qkv.shape
    q, k, v = jnp.split(qkv, 3, axis=-1)
    q, k, v = (t.reshape(bsz, s, ATTN_HEADS, ATTN_HEAD_DIM).transpose(0, 2, 1, 3) for t in (q, k, v))
    log_f = jax.nn.log_sigmoid((f_raw + forget_b).astype(jnp.float32))
    cum = jnp.cumsum(log_f, axis=1).transpose(0, 2, 1)
    scale = ATTN_HEAD_DIM ** -0.5
    outs = []
    for blk in range(s // Q_BLOCK):
        q0, q1 = blk * Q_BLOCK, (blk + 1) * Q_BLOCK
        sc = jnp.einsum('bhqd,bhkd->bhqk', q[:, :, q0:q1], k[:, :, :q1]).astype(jnp.float32) * scale
        sc = sc + cum[:, :, q0:q1, None] - cum[:, :, None, :q1]
        causal = jnp.arange(q0, q1)[:, None] >= jnp.arange(q1)[None, :]
        p = jax.nn.softmax(jnp.where(causal, sc, -jnp.inf), axis=-1)
        outs.append(jnp.einsum('bhqk,bhkd->bhqd', p.astype(v.dtype), v[:, :, :q1]))
    o = jnp.concatenate(outs, axis=2)
    return o.transpose(0, 2, 1, 3).reshape(bsz, s, D_ATTN)


def route(xf, router_w, router_b):
    t = xf.shape[0]
    probs = jax.nn.softmax((xf @ router_w + router_b).astype(jnp.float32), axis=-1)
    grouped = probs.reshape(t, N_EXPERT_GROUPS, EXPERTS_PER_GROUP)
    group_score = jnp.sum(lax.top_k(grouped, TOP_K)[0], axis=-1)
    in_group = jax.nn.one_hot(jnp.argmax(group_score, axis=-1), N_EXPERT_GROUPS, dtype=jnp.bool_)
    masked = jnp.where(jnp.repeat(in_group, EXPERTS_PER_GROUP, axis=-1), probs, -1.0)
    top_p, top_i = lax.top_k(masked, TOP_K)
    w = top_p / jnp.sum(top_p, axis=-1, keepdims=True)
    return jnp.sum(jax.nn.one_hot(top_i, N_EXPERTS, dtype=jnp.float32) * w[..., None], axis=1)


def moe_ffn(x, router_w, router_b, w1, w3, w2):
    bsz, s, d = x.shape
    xf = x.reshape(-1, d)
    combine = route(xf, router_w, router_b).astype(x.dtype)
    y = jnp.zeros_like(xf)
    for e in range(N_EXPERTS):
        h = jax.nn.silu(xf @ w1[e]) * (xf @ w3[e])
        y = y + combine[:, e:e + 1] * (h @ w2[e])
    return y.reshape(bsz, s, d)


def setup_inputs(seed: int = 0) -> dict:
    key = jax.random.key(seed)
    ks = jax.random.split(key, 32)
    f32 = jnp.float32

    def nrm(k, shape, scale):
        return jax.random.normal(k, shape, f32) * scale

    L = DEPTH
    u_a = jax.random.uniform(ks[7], (L, D_RNN), f32, minval=0.9, maxval=0.999)
    s_a = u_a ** (1.0 / LRU_C)
    dt0 = jnp.exp(jax.random.uniform(ks[10], (L, SSM_HEADS), f32, minval=math.log(1e-3), maxval=math.log(1e-1)))
    return {
        'x': nrm(ks[0], (BATCH, SEQ, D_MODEL), 1.0),
        'w_in': nrm(ks[1], (L, D_MODEL, P_IN), D_MODEL ** -0.5),
        'gate_b': nrm(ks[2], (L, N_BRANCHES * D_MODEL), 0.1),
        'conv_a_w': nrm(ks[3], (L, CONV_WIDTH, D_RNN), CONV_WIDTH ** -0.5),
        'conv_a_b': nrm(ks[4], (L, D_RNN), 0.02),
        'lru_wa': nrm(ks[5], (L, RNN_HEADS, RNN_BLOCK, RNN_BLOCK), RNN_BLOCK ** -0.5),
        'lru_ba': nrm(ks[6], (L, D_RNN), 0.1),
        'lru_wx': nrm(ks[8], (L, RNN_HEADS, RNN_BLOCK, RNN_BLOCK), RNN_BLOCK ** -0.5),
        'lru_bx': nrm(ks[9], (L, D_RNN), 0.1),
        'lru_lambda': jnp.log(s_a) - jnp.log1p(-s_a),
        'conv_b_w': nrm(ks[11], (L, CONV_WIDTH, SSM_CONV_CH), CONV_WIDTH ** -0.5),
        'conv_b_b': nrm(ks[12], (L, SSM_CONV_CH), 0.02),
        'dt_bias': dt0 + jnp.log(-jnp.expm1(-dt0)),
        'a_log': jnp.log(jax.random.uniform(ks[13], (L, SSM_HEADS), f32, minval=1.0, maxval=16.0)),
        'd_skip': 1.0 + nrm(ks[14], (L, SSM_HEADS), 0.1),
        'ssm_norm_w': 1.0 + nrm(ks[15], (L, SSM_D_INNER), 0.1),
        'forget_b': jax.random.uniform(ks[16], (L, ATTN_HEADS), f32, minval=1.0, maxval=5.0),
        'w_branch_a': nrm(ks[17], (L, D_RNN, D_MODEL), D_RNN ** -0.5 * DEEPNORM_BETA),
        'w_branch_b': nrm(ks[18], (L, SSM_D_INNER, D_MODEL), SSM_D_INNER ** -0.5 * DEEPNORM_BETA),
        'w_branch_c': nrm(ks[19], (L, D_ATTN, D_MODEL), D_ATTN ** -0.5 * DEEPNORM_BETA),
        'w_out': nrm(ks[20], (L, D_MODEL, D_MODEL), D_MODEL ** -0.5 * DEEPNORM_BETA),
        'ln1_g': 1.0 + nrm(ks[21], (L, D_MODEL), 0.1),
        'ln1_b': nrm(ks[22], (L, D_MODEL), 0.02),
        'router_w': nrm(ks[23], (D_MODEL, N_EXPERTS), D_MODEL ** -0.5),
        'router_b': nrm(ks[24], (N_EXPERTS,), 0.01),
        'w1': nrm(ks[25], (L, N_EXPERTS, D_MODEL, D_EXPERT), D_MODEL ** -0.5),
        'w3': nrm(ks[26], (L, N_EXPERTS, D_MODEL, D_EXPERT), D_MODEL ** -0.5),
        'w2': nrm(ks[27], (L, N_EXPERTS, D_EXPERT, D_MODEL), D_EXPERT ** -0.5 * DEEPNORM_BETA),
        'ln2_g': 1.0 + nrm(ks[28], (L, D_MODEL), 0.1),
        'ln2_b': nrm(ks[29], (L, D_MODEL), 0.02),
    }


def reference(x, w_in, gate_b, conv_a_w, conv_a_b, lru_wa, lru_ba, lru_wx, lru_bx, lru_lambda,
              conv_b_w, conv_b_b, dt_bias, a_log, d_skip, ssm_norm_w, forget_b,
              w_branch_a, w_branch_b, w_branch_c, w_out, ln1_g, ln1_b,
              router_w, router_b, w1, w3, w2, ln2_g, ln2_b):
    bsz, s, d = x.shape
    for l in range(DEPTH):
        proj = x @ w_in[l]
        a_x, a_gate, b_z, b_xbc, b_dt, c_qkv, c_f, gate_raw = jnp.split(proj, SPLIT_IDX, axis=-1)
        y_a = recurrent_branch(a_x, a_gate, conv_a_w[l], conv_a_b[l], lru_wa[l], lru_ba[l],
                               lru_wx[l], lru_bx[l], lru_lambda[l]) @ w_branch_a[l]
        y_b = ssd_branch(b_z, b_xbc, b_dt, conv_b_w[l], conv_b_b[l], dt_bias[l], a_log[l],
                         d_skip[l], ssm_norm_w[l]) @ w_branch_b[l]
        y_c = forgetting_attention(c_qkv, c_f, forget_b[l]) @ w_branch_c[l]
        g = jax.nn.sigmoid(gate_raw + gate_b[l]).reshape(bsz, s, N_BRANCHES, d)
        mixed = (g[:, :, 0] * y_a + g[:, :, 1] * y_b + g[:, :, 2] * y_c) @ w_out[l]
        x = layer_norm(DEEPNORM_ALPHA * x + mixed, ln1_g[l], ln1_b[l])
        x = layer_norm(DEEPNORM_ALPHA * x + moe_ffn(x, router_w, router_b, w1[l], w3[l], w2[l]),
                       ln2_g[l], ln2_b[l])
    return x
```

```python
import functools

import jax
import jax.numpy as jnp
from jax import lax
from jax.experimental import pallas as pl
from jax.experimental.pallas import tpu as pltpu

F32 = jnp.float32
BF16 = jnp.bfloat16
HIGHEST = lax.Precision.HIGHEST

D_MODEL = 1024
DEPTH = 4
RNN_HEADS = 8
RNN_BLOCK = 128
CONV_WIDTH = 4
LRU_C = 8.0
SSM_HEADS = 16
SSM_HEAD_DIM = 64
SSM_GROUPS = 4
SSM_STATE = 128
SSM_CHUNK = 128
SSM_CONV_CH = 2048
ATTN_HEADS = 16
ATTN_HEAD_DIM = 64
N_EXPERTS = 16
EXPERTS_PER_GROUP = 4
D_EXPERT = 512
LN_EPS = 1e-5
RMS_EPS = 1e-6
DEEPNORM_ALPHA = (2 * DEPTH) ** 0.25

LANES = 128
SUBLANES = 8
VMEM_LIMIT = 48 * 1024 * 1024

COL_QKV = 0
COL_GATE = 3072
COL_XBC = 6144
COL_AX = 8192
COL_AGATE = 9216
COL_BZ = 10240
N_MAIN = 11264
N_SMALL = 128


def _cparams(sem):
    return pltpu.CompilerParams(dimension_semantics=sem, vmem_limit_bytes=VMEM_LIMIT)


def _mm_kernel(x_ref, w_ref, o_ref):
    o_ref[...] = jnp.dot(x_ref[...], w_ref[...], preferred_element_type=F32).astype(o_ref.dtype)


def _matmul(x, w, out_dtype, tm, tn, name):
    m, k = x.shape
    n = w.shape[1]
    return pl.pallas_call(
        _mm_kernel,
        grid=(m // tm, n // tn),
        in_specs=[pl.BlockSpec((tm, k), lambda i, j: (i, 0)),
                  pl.BlockSpec((k, tn), lambda i, j: (0, j))],
        out_specs=pl.BlockSpec((tm, tn), lambda i, j: (i, j)),
        out_shape=jax.ShapeDtypeStruct((m, n), out_dtype),
        compiler_params=_cparams(("parallel", "parallel")),
        name=name,
    )(x, w)


def _causal_conv(x, xbuf, cw_ref, cb_ref, first):
    ts = x.shape[0]

    @pl.when(first)
    def _():
        xbuf[0:SUBLANES, :] = jnp.zeros((SUBLANES, x.shape[1]), F32)

    xbuf[SUBLANES:SUBLANES + ts, :] = x
    y = cb_ref[...]
    for k in range(CONV_WIDTH):
        off = SUBLANES - (CONV_WIDTH - 1) + k
        y = y + cw_ref[k:k + 1, :] * xbuf[off:off + ts, :]
    xbuf[0:SUBLANES, :] = xbuf[ts:ts + SUBLANES, :]
    return y


def _rglru_kernel(x_ref, g_ref, cw_ref, cb_ref, wg_ref, ba_ref, bx_ref, lam_ref, o_ref, xbuf, hcar):
    s = pl.program_id(1)
    ts = x_ref.shape[0]
    first = s == 0

    @pl.when(first)
    def _():
        hcar[...] = jnp.zeros(hcar.shape, F32)

    xa = _causal_conv(x_ref[...].astype(F32), xbuf, cw_ref, cb_ref, first)
    xab = xa.astype(BF16)
    r_parts, i_parts = [], []
    for h in range(RNN_HEADS):
        pre = jnp.dot(xab[:, h * RNN_BLOCK:(h + 1) * RNN_BLOCK], wg_ref[h], preferred_element_type=F32)
        r_parts.append(pre[:, :RNN_BLOCK])
        i_parts.append(pre[:, RNN_BLOCK:])
    r_gate = jax.nn.sigmoid(jnp.concatenate(r_parts, axis=1) + ba_ref[...])
    i_gate = jax.nn.sigmoid(jnp.concatenate(i_parts, axis=1) + bx_ref[...])
    log_a = (-LRU_C) * r_gate * jax.nn.softplus(-lam_ref[...])
    a = jnp.exp(log_a)
    mult = jnp.sqrt(1.0 - jnp.exp(2.0 * log_a))
    u = (xa * i_gate) * mult

    ng = ts // SUBLANES
    a3 = a.reshape(ng, SUBLANES, D_MODEL)
    b3 = u.reshape(ng, SUBLANES, D_MODEL)
    row = lax.broadcasted_iota(jnp.int32, a3.shape, 1)
    d = 1
    while d < SUBLANES:
        valid = row >= d
        a_s = jnp.where(valid, pltpu.roll(a3, d, axis=1), 1.0)
        b_s = jnp.where(valid, pltpu.roll(b3, d, axis=1), 0.0)
        b3 = a3 * b_s + b3
        a3 = a3 * a_s
        d *= 2
    h_in = hcar[SUBLANES - 1:SUBLANES, :]
    groups = []
    for gi in range(ng):
        hg = b3[gi] + a3[gi] * h_in
        groups.append(hg)
        h_in = hg[SUBLANES - 1:SUBLANES, :]
    h = jnp.concatenate(groups, axis=0)
    hcar[...] = groups[-1]
    o_ref[...] = (h * jax.nn.gelu(g_ref[...].astype(F32))).astype(o_ref.dtype)


def _branch_a(proj, cw, cb, wg, ba, bx, lam, bsz, seq, ts=256):
    nst = seq // ts
    full = lambda shape: pl.BlockSpec(shape, lambda b, s: (0,) * len(shape))
    return pl.pallas_call(
        _rglru_kernel,
        grid=(bsz, nst),
        in_specs=[pl.BlockSpec((ts, D_MODEL), lambda b, s: (b * nst + s, COL_AX // D_MODEL)),
                  pl.BlockSpec((ts, D_MODEL), lambda b, s: (b * nst + s, COL_AGATE // D_MODEL)),
                  full((CONV_WIDTH, D_MODEL)), full((1, D_MODEL)),
                  full((RNN_HEADS, RNN_BLOCK, 2 * RNN_BLOCK)),
                  full((1, D_MODEL)), full((1, D_MODEL)), full((1, D_MODEL))],
        out_specs=pl.BlockSpec((ts, D_MODEL), lambda b, s: (b * nst + s, 0)),
        out_shape=jax.ShapeDtypeStruct((bsz * seq, D_MODEL), BF16),
        scratch_shapes=[pltpu.VMEM((ts + 2 * SUBLANES, D_MODEL), F32), pltpu.VMEM((SUBLANES, D_MODEL), F32)],
        compiler_params=_cparams(("parallel", "arbitrary")),
        name="rglru",
    )(proj, proj, cw, cb, wg, ba, bx, lam)


def _ssd_kernel(z_ref, xbc_ref, dtf_ref, cw_ref, cb_ref, dtb_ref, alog_ref, dskip_ref, nw_ref,
                o_ref, xbuf, state):
    c = pl.program_id(1)
    L = SSM_CHUNK
    first = c == 0

    @pl.when(first)
    def _():
        state[...] = jnp.zeros(state.shape, F32)

    conv = _causal_conv(xbc_ref[...].astype(F32), xbuf, cw_ref, cb_ref, first)
    act = conv * jax.nn.sigmoid(conv)
    xs = act[:, :D_MODEL]
    bm = act[:, D_MODEL:D_MODEL + SSM_GROUPS * SSM_STATE]
    cm = act[:, D_MODEL + SSM_GROUPS * SSM_STATE:]

    lane = lax.broadcasted_iota(jnp.int32, (L, LANES), 1)
    head_lane = lane < SSM_HEADS
    dt = jnp.where(head_lane, jax.nn.softplus(dtf_ref[...] + dtb_ref[...]), 0.0)
    a_dt = dt * (-jnp.exp(alog_ref[...]))
    ri = lax.broadcasted_iota(jnp.int32, (L, L), 0)
    ci = lax.broadcasted_iota(jnp.int32, (L, L), 1)
    causal = ri >= ci
    tril = jnp.where(causal, 1.0, 0.0).astype(F32)
    cs = jnp.dot(tril, a_dt, precision=HIGHEST, preferred_element_type=F32)
    cs_t = cs.T
    tot = cs[L - 1:L, :]
    dstate = jnp.exp(tot - cs)
    exp_cs = jnp.exp(cs)

    er = lax.broadcasted_iota(jnp.int32, (LANES, D_MODEL), 0)
    ec = lax.broadcasted_iota(jnp.int32, (LANES, D_MODEL), 1)
    expand = jnp.where(ec // SSM_HEAD_DIM == er, 1.0, 0.0).astype(F32)
    dt_e = jnp.dot(dt, expand, precision=HIGHEST, preferred_element_type=F32)
    dtds_e = jnp.dot(dt * dstate, expand, precision=HIGHEST, preferred_element_type=F32)
    tot_e = jnp.dot(jnp.broadcast_to(jnp.exp(tot), (SUBLANES, LANES)), expand,
                    precision=HIGHEST, preferred_element_type=F32)[0:1, :]
    xdt = xs * dt_e
    xdt_end = (xs * dtds_e).astype(BF16)

    lo_half = lax.broadcasted_iota(jnp.int32, (2 * L, LANES), 1) < SSM_HEAD_DIM
    heads_per_group = SSM_HEADS // SSM_GROUPS
    y_parts = []
    new_states = []
    for g in range(SSM_GROUPS):
        bg = bm[:, g * SSM_STATE:(g + 1) * SSM_STATE]
        cg = cm[:, g * SSM_STATE:(g + 1) * SSM_STATE]
        cb = lax.dot_general(cg.astype(BF16), bg.astype(BF16), (((1,), (1,)), ((), ())),
                             preferred_element_type=F32)
        lhs = []
        for e in range(heads_per_group):
            hd = g * heads_per_group + e
            colb = jnp.broadcast_to(cs[:, hd:hd + 1], (L, L))
            rowb = jnp.broadcast_to(cs_t[hd:hd + 1, :], (L, L))
            decay = jnp.exp(jnp.where(causal, colb - rowb, -jnp.inf))
            m = (cb * decay).astype(BF16)
            c_off = (cg * jnp.broadcast_to(exp_cs[:, hd:hd + 1], (L, L))).astype(BF16)
            lhs.append(jnp.concatenate([m, c_off], axis=1))
        for j in range(heads_per_group // 2):
            col = (g * heads_per_group + 2 * j) * SSM_HEAD_DIM
            rhs = jnp.concatenate([xdt[:, col:col + LANES], state[:, col:col + LANES]], axis=0).astype(BF16)
            zero = jnp.zeros_like(rhs)
            y_parts.append(jnp.dot(lhs[2 * j], jnp.where(lo_half, rhs, zero), preferred_element_type=F32)
                           + jnp.dot(lhs[2 * j + 1], jnp.where(lo_half, zero, rhs), preferred_element_type=F32))
        gw = heads_per_group * SSM_HEAD_DIM
        new_states.append(jnp.dot(bg.T.astype(BF16), xdt_end[:, g * gw:(g + 1) * gw],
                                  preferred_element_type=F32))
    y = jnp.concatenate(y_parts, axis=1)
    state[...] = state[...] * tot_e + jnp.concatenate(new_states, axis=1)

    y = y + xs * dskip_ref[...]
    z = z_ref[...].astype(F32)
    gy = y * (z * jax.nn.sigmoid(z))
    gw = D_MODEL // SSM_GROUPS
    outs = []
    for g in range(SSM_GROUPS):
        gg = gy[:, g * gw:(g + 1) * gw]
        ms = jnp.mean(gg * gg, axis=-1, keepdims=True)
        outs.append(gg * lax.rsqrt(ms + RMS_EPS))
    o_ref[...] = (jnp.concatenate(outs, axis=1) * nw_ref[...]).astype(o_ref.dtype)


def _branch_b(proj, small, cw, cb, dtb, alog, dskip_e, nw, bsz, seq):
    L = SSM_CHUNK
    nc = seq // L
    full = lambda shape: pl.BlockSpec(shape, lambda b, c: (0,) * len(shape))
    return pl.pallas_call(
        _ssd_kernel,
        grid=(bsz, nc),
        in_specs=[pl.BlockSpec((L, D_MODEL), lambda b, c: (b * nc + c, COL_BZ // D_MODEL)),
                  pl.BlockSpec((L, SSM_CONV_CH), lambda b, c: (b * nc + c, COL_XBC // SSM_CONV_CH)),
                  pl.BlockSpec((L, N_SMALL), lambda b, c: (b * nc + c, 0)),
                  full((CONV_WIDTH, SSM_CONV_CH)), full((1, SSM_CONV_CH)),
                  full((1, N_SMALL)), full((1, N_SMALL)), full((1, D_MODEL)), full((1, D_MODEL))],
        out_specs=pl.BlockSpec((L, D_MODEL), lambda b, c: (b * nc + c, 0)),
        out_shape=jax.ShapeDtypeStruct((bsz * seq, D_MODEL), BF16),
        scratch_shapes=[pltpu.VMEM((L + 2 * SUBLANES, SSM_CONV_CH), F32), pltpu.VMEM((SSM_STATE, D_MODEL), F32)],
        compiler_params=_cparams(("parallel", "arbitrary")),
        name="ssd",
    )(proj, proj, small, cw, cb, dtb, alog, dskip_e, nw)


CUM_BLOCK = 256


def _fox_cum_kernel(dtf_ref, fb_ref, o_ref):
    seq = dtf_ref.shape[0]
    ri = lax.broadcasted_iota(jnp.int32, (CUM_BLOCK, CUM_BLOCK), 0)
    ci = lax.broadcasted_iota(jnp.int32, (CUM_BLOCK, CUM_BLOCK), 1)
    tril = jnp.where(ri >= ci, 1.0, 0.0).astype(F32)
    carry = jnp.zeros((1, LANES), F32)
    blocks = []
    for i in range(seq // CUM_BLOCK):
        logf = jax.nn.log_sigmoid(dtf_ref[i * CUM_BLOCK:(i + 1) * CUM_BLOCK, :] + fb_ref[...])
        cb = jnp.dot(tril, logf, precision=HIGHEST, preferred_element_type=F32) + carry
        carry = cb[CUM_BLOCK - 1:CUM_BLOCK, :]
        blocks.append(cb.T)
    cum_t = jnp.concatenate(blocks, axis=1)
    o_ref[0] = cum_t[ATTN_HEADS:2 * ATTN_HEADS, :]


def _fox_cum(small, fb, bsz, seq):
    return pl.pallas_call(
        _fox_cum_kernel,
        grid=(bsz,),
        in_specs=[pl.BlockSpec((seq, N_SMALL), lambda b: (b, 0)),
                  pl.BlockSpec((1, N_SMALL), lambda b: (0, 0))],
        out_specs=pl.BlockSpec((1, ATTN_HEADS, seq), lambda b: (b, 0, 0)),
        out_shape=jax.ShapeDtypeStruct((bsz, ATTN_HEADS, seq), F32),
        compiler_params=_cparams(("parallel",)),
        name="fox_cum",
    )(small, fb)


def _fox_attn_kernel(q_ref, k_ref, v_ref, cum_ref, o_ref, *, tq, tk):
    seq = q_ref.shape[0]
    hd = ATTN_HEAD_DIM
    lane_q = lax.broadcasted_iota(jnp.int32, (tq, LANES), 1) < hd
    lane_k = lax.broadcasted_iota(jnp.int32, (tk, LANES), 1) < hd
    rel_r = lax.broadcasted_iota(jnp.int32, (tq, tk), 0)
    rel_c = lax.broadcasted_iota(jnp.int32, (tq, tk), 1)
    nkq = tq // tk

    def q_body(qi, _):
        q0 = pl.multiple_of(qi * tq, tq)
        q = q_ref[pl.ds(q0, tq), :] * jnp.asarray(hd ** -0.5, BF16)
        zq = jnp.zeros_like(q)
        q_heads = (jnp.where(lane_q, q, zq), jnp.where(lane_q, zq, q))
        c_ref0 = cum_ref[0, 0, :, pl.ds(q0, LANES)][:, 0:1]

        def step(kj, carry, masked):
            m0, l0, m1, l1, acc = carry
            k0 = pl.multiple_of(kj * tk, tk)
            k = k_ref[pl.ds(k0, tk), :]
            v = v_ref[pl.ds(k0, tk), :]
            zv = jnp.zeros_like(v)
            v_heads = (jnp.where(lane_k, v, zv), jnp.where(lane_k, zv, v))
            bias = c_ref0 - cum_ref[0, 0, :, pl.ds(k0, tk)]
            ms, ls, alphas, pvs = [], [], [], []
            for hh, (m_prev, l_prev) in enumerate(((m0, l0), (m1, l1))):
                s = lax.dot_general(q_heads[hh], k, (((1,), (1,)), ((), ())), preferred_element_type=F32)
                s = s + bias[hh:hh + 1, :]
                if masked:
                    s = jnp.where(rel_c + (k0 - q0) <= rel_r, s, -jnp.inf)
                m_new = jnp.maximum(m_prev, jnp.max(s, axis=-1, keepdims=True))
                alpha = jnp.exp(m_prev - m_new)
                p = jnp.exp(s - m_new)
                ls.append(alpha * l_prev + jnp.sum(p, axis=-1, keepdims=True))
                ms.append(m_new)
                alphas.append(alpha)
                pvs.append(jnp.dot(p.astype(BF16), v_heads[hh], preferred_element_type=F32))
            alpha_pair = jnp.where(lane_q, alphas[0], alphas[1])
            acc = acc * alpha_pair + pvs[0] + pvs[1]
            return ms[0], ls[0], ms[1], ls[1], acc

        neg = jnp.full((tq, 1), -1e30, F32)
        zero = jnp.zeros((tq, 1), F32)
        carry = (neg, zero, neg, zero, jnp.zeros((tq, LANES), F32))
        carry = lax.fori_loop(0, qi * nkq, lambda kj, c: step(kj, c, False), carry)
        for d in range(nkq):
            carry = step(qi * nkq + d, carry, True)
        m0, l0, m1, l1, acc = carry
        inv = jnp.where(lane_q, 1.0 / l0, 1.0 / l1)
        o_ref[pl.ds(q0, tq), :] = (acc * inv).astype(o_ref.dtype)
        return 0

    lax.fori_loop(0, seq // tq, q_body, 0)


def _branch_c(proj, cum, bsz, seq, tq=256, tk=256):
    npair = ATTN_HEADS // 2
    cum4 = cum.reshape(bsz, npair, 2, seq)
    return pl.pallas_call(
        functools.partial(_fox_attn_kernel, tq=tq, tk=tk),
        grid=(bsz, npair),
        in_specs=[pl.BlockSpec((seq, LANES), lambda b, p: (b, COL_QKV // LANES + p)),
                  pl.BlockSpec((seq, LANES), lambda b, p: (b, COL_QKV // LANES + npair + p)),
                  pl.BlockSpec((seq, LANES), lambda b, p: (b, COL_QKV // LANES + 2 * npair + p)),
                  pl.BlockSpec((1, 1, 2, seq), lambda b, p: (b, p, 0, 0))],
        out_specs=pl.BlockSpec((seq, LANES), lambda b, p: (b, p)),
        out_shape=jax.ShapeDtypeStruct((bsz * seq, D_MODEL), BF16),
        compiler_params=_cparams(("parallel", "parallel")),
        name="fox_attn",
    )(proj, proj, proj, cum4)


def _layer_norm(x, g, b):
    mu = jnp.mean(x, axis=-1, keepdims=True)
    xc = x - mu
    var = jnp.mean(xc * xc, axis=-1, keepdims=True)
    return xc * lax.rsqrt(var + LN_EPS) * g + b


def _top2_sum(a, b, c, d):
    hi1, lo1 = jnp.maximum(a, b), jnp.minimum(a, b)
    hi2, lo2 = jnp.maximum(c, d), jnp.minimum(c, d)
    return jnp.maximum(hi1, hi2) + jnp.maximum(jnp.minimum(hi1, hi2), jnp.maximum(lo1, lo2))


def _route_rows(logits_t):
    rows = [logits_t[e:e + 1, :] for e in range(N_EXPERTS)]
    mx = functools.reduce(jnp.maximum, rows)
    ex = [jnp.exp(r - mx) for r in rows]
    den = functools.reduce(jnp.add, ex)
    probs = [e / den for e in ex]
    ngroups = N_EXPERTS // EXPERTS_PER_GROUP
    scores = [_top2_sum(*probs[EXPERTS_PER_GROUP * g:EXPERTS_PER_GROUP * (g + 1)]) for g in range(ngroups)]
    best_g = jnp.zeros_like(mx, dtype=jnp.int32)
    best_s = scores[0]
    for g in range(1, ngroups):
        better = scores[g] > best_s
        best_g = jnp.where(better, g, best_g)
        best_s = jnp.where(better, scores[g], best_s)
    masked = [jnp.where(best_g == e // EXPERTS_PER_GROUP, probs[e], -1.0) for e in range(N_EXPERTS)]
    v1, i1 = masked[0], jnp.zeros_like(best_g)
    for e in range(1, N_EXPERTS):
        better = masked[e] > v1
        i1 = jnp.where(better, e, i1)
        v1 = jnp.where(better, masked[e], v1)
    v2, i2 = jnp.full_like(v1, -2.0), jnp.zeros_like(best_g)
    for e in range(N_EXPERTS):
        better = (masked[e] > v2) & (i1 != e)
        i2 = jnp.where(better, e, i2)
        v2 = jnp.where(better, masked[e], v2)
    tot = v1 + v2
    return i1, i2, v1 / tot, v2 / tot


def _merge_kernel(ha_ref, hb_ref, hc_ref, gate_ref, x_ref, wa_ref, wb_ref, wc_ref, wo_ref, gb_ref,
                  lg_ref, lb_ref, rwh_ref, rwl_ref, rb_ref,
                  x1_ref, x1b_ref, rcols_ref):
    tm = x_ref.shape[0]
    ya = jnp.dot(ha_ref[...], wa_ref[...], preferred_element_type=F32)
    yb = jnp.dot(hb_ref[...], wb_ref[...], preferred_element_type=F32)
    yc = jnp.dot(hc_ref[...], wc_ref[...], preferred_element_type=F32)
    g = jax.nn.sigmoid(gate_ref[...].astype(F32) + gb_ref[...])
    mixed_in = (g[:, :D_MODEL] * ya + g[:, D_MODEL:2 * D_MODEL] * yb + g[:, 2 * D_MODEL:] * yc).astype(BF16)
    mixed = jnp.dot(mixed_in, wo_ref[...], preferred_element_type=F32)
    x1 = _layer_norm(DEEPNORM_ALPHA * x_ref[...] + mixed, lg_ref[...], lb_ref[...])
    x1_ref[...] = x1
    x1h = x1.astype(BF16)
    x1b_ref[...] = x1h
    x1l = (x1 - x1h.astype(F32)).astype(BF16)
    nt = (((1,), (1,)), ((), ()))
    logits_t = (lax.dot_general(rwh_ref[...], x1h, nt, preferred_element_type=F32)
                + lax.dot_general(rwl_ref[...], x1h, nt, preferred_element_type=F32)
                + lax.dot_general(rwh_ref[...], x1l, nt, preferred_element_type=F32)
                + rb_ref[...])
    i1, i2, w1, w2 = _route_rows(logits_t)
    rows = [jnp.where(i1 == e, w1, 0.0) + jnp.where(i2 == e, w2, 0.0) for e in range(N_EXPERTS)]
    rows += [w1, w2, i1.astype(F32), i2.astype(F32)]
    nrow = 4 * SUBLANES
    sub = lax.broadcasted_iota(jnp.int32, (nrow, tm), 0)
    packed = jnp.zeros((nrow, tm), F32)
    for r, val in enumerate(rows):
        packed = jnp.where(sub == r, val, packed)
    packed = jnp.concatenate([packed, jnp.zeros((LANES - nrow, tm), F32)], axis=0)
    rcols_ref[...] = packed.T


def _merge(ha, hb, hc, proj, x, wa, wb, wc, wo, gb, lg, lb, rwh, rwl, rb, tm=512):
    t = x.shape[0]
    full = lambda shape: pl.BlockSpec(shape, lambda i: (0,) * len(shape))
    row = lambda w: pl.BlockSpec((tm, w), lambda i: (i, 0))
    return pl.pallas_call(
        _merge_kernel,
        grid=(t // tm,),
        in_specs=[row(D_MODEL), row(D_MODEL), row(D_MODEL),
                  pl.BlockSpec((tm, 3 * D_MODEL), lambda i: (i, COL_GATE // (3 * D_MODEL))),
                  row(D_MODEL),
                  full((D_MODEL, D_MODEL)), full((D_MODEL, D_MODEL)), full((D_MODEL, D_MODEL)),
                  full((D_MODEL, D_MODEL)), full((1, 3 * D_MODEL)),
                  full((1, D_MODEL)), full((1, D_MODEL)),
                  full((N_EXPERTS, D_MODEL)), full((N_EXPERTS, D_MODEL)), full((N_EXPERTS, 1))],
        out_specs=[row(D_MODEL), row(D_MODEL), row(LANES)],
        out_shape=[jax.ShapeDtypeStruct((t, D_MODEL), F32), jax.ShapeDtypeStruct((t, D_MODEL), BF16),
                   jax.ShapeDtypeStruct((t, LANES), F32)],
        compiler_params=_cparams(("parallel",)),
        name="merge",
    )(ha, hb, hc, proj, x, wa, wb, wc, wo, gb, lg, lb, rwh, rwl, rb)


def _moe_dense_kernel(xb_ref, x_ref, rc_ref, w1_ref, w3_ref, w2_ref, lg_ref, lb_ref, o_ref, ob_ref, acc):
    e = pl.program_id(1)

    @pl.when(e == 0)
    def _():
        acc[...] = jnp.zeros(acc.shape, F32)

    xb = xb_ref[...]
    h1 = jnp.dot(xb, w1_ref[0], preferred_element_type=F32)
    h3 = jnp.dot(xb, w3_ref[0], preferred_element_type=F32)
    h = (h1 * jax.nn.sigmoid(h1) * h3).astype(BF16)
    y = jnp.dot(h, w2_ref[0], preferred_element_type=F32)
    rc = rc_ref[...]
    lane = lax.broadcasted_iota(jnp.int32, rc.shape, 1)
    wcol = jnp.sum(jnp.where(lane == e, rc, 0.0), axis=-1, keepdims=True)
    acc[...] += wcol * y

    @pl.when(e == N_EXPERTS - 1)
    def _():
        x2 = _layer_norm(DEEPNORM_ALPHA * x_ref[...] + acc[...], lg_ref[...], lb_ref[...])
        o_ref[...] = x2
        ob_ref[...] = x2.astype(BF16)


def _moe_dense(x1b, x1, rcols, w1, w3, w2, lg, lb, tm=1024):
    t = x1.shape[0]
    row = lambda w: pl.BlockSpec((tm, w), lambda i, e: (i, 0))
    full = lambda shape: pl.BlockSpec(shape, lambda i, e: (0,) * len(shape))
    return pl.pallas_call(
        _moe_dense_kernel,
        grid=(t // tm, N_EXPERTS),
        in_specs=[row(D_MODEL), row(D_MODEL), row(LANES),
                  pl.BlockSpec((1, D_MODEL, D_EXPERT), lambda i, e: (e, 0, 0)),
                  pl.BlockSpec((1, D_MODEL, D_EXPERT), lambda i, e: (e, 0, 0)),
                  pl.BlockSpec((1, D_EXPERT, D_MODEL), lambda i, e: (e, 0, 0)),
                  full((1, D_MODEL)), full((1, D_MODEL))],
        out_specs=[row(D_MODEL), row(D_MODEL)],
        out_shape=[jax.ShapeDtypeStruct((t, D_MODEL), F32), jax.ShapeDtypeStruct((t, D_MODEL), BF16)],
        scratch_shapes=[pltpu.VMEM((tm, D_MODEL), F32)],
        compiler_params=_cparams(("parallel", "arbitrary")),
        name="moe_dense",
    )(x1b, x1, rcols, w1, w3, w2, lg, lb)


def _split_hi_lo(w):
    hi = w.astype(BF16)
    return hi, (w - hi.astype(F32)).astype(BF16)


def _prepare(w_in, gate_b, conv_a_w, conv_a_b, lru_wa, lru_ba, lru_wx, lru_bx, lru_lambda,
             conv_b_w, conv_b_b, dt_bias, a_log, d_skip, ssm_norm_w, forget_b,
             w_branch_a, w_branch_b, w_branch_c, w_out, ln1_g, ln1_b,
             router_w, router_b, w1, w3, w2, ln2_g, ln2_b):
    depth = w_in.shape[0]
    s0 = D_MODEL
    o_ax, o_ag, o_bz, o_xbc = 0, s0, 2 * s0, 3 * s0
    o_dt = o_xbc + SSM_CONV_CH
    o_qkv = o_dt + SSM_HEADS
    o_f = o_qkv + 3 * D_MODEL
    o_gate = o_f + ATTN_HEADS
    w_main = jnp.concatenate([w_in[:, :, o_qkv:o_f], w_in[:, :, o_gate:], w_in[:, :, o_xbc:o_dt],
                              w_in[:, :, o_ax:o_ag], w_in[:, :, o_ag:o_bz], w_in[:, :, o_bz:o_xbc]],
                             axis=-1).astype(BF16)
    w_small = jnp.concatenate([w_in[:, :, o_dt:o_qkv], w_in[:, :, o_f:o_gate],
                               jnp.zeros((depth, D_MODEL, N_SMALL - SSM_HEADS - ATTN_HEADS), F32)],
                              axis=-1).astype(BF16)
    w_gates = jnp.concatenate([lru_wa, lru_wx], axis=-1).astype(BF16)
    pad_heads = lambda v, off: jnp.pad(v, ((0, 0), (off, N_SMALL - off - v.shape[1])))[:, None, :]
    dtb_p = pad_heads(dt_bias, 0)
    alog_p = pad_heads(a_log, 0)
    fb_p = pad_heads(forget_b, SSM_HEADS)
    dskip_e = jnp.repeat(d_skip, SSM_HEAD_DIM, axis=-1)[:, None, :]
    row = lambda v: v[:, None, :]
    wa_b, wb_b, wc_b, wo_b = (w.astype(BF16) for w in (w_branch_a, w_branch_b, w_branch_c, w_out))
    w1_b, w3_b, w2_b = w1.astype(BF16), w3.astype(BF16), w2.astype(BF16)
    rwh, rwl = _split_hi_lo(router_w.T)
    return dict(
        w_main=w_main, w_small=w_small, conv_a_w=conv_a_w, conv_a_b=row(conv_a_b), w_gates=w_gates,
        lru_ba=row(lru_ba), lru_bx=row(lru_bx), lru_lambda=row(lru_lambda),
        conv_b_w=conv_b_w, conv_b_b=row(conv_b_b), dtb=dtb_p, alog=alog_p, dskip=dskip_e,
        ssm_norm_w=row(ssm_norm_w), fb=fb_p, wa=wa_b, wb=wb_b, wc=wc_b, wo=wo_b, gate_b=row(gate_b),
        ln1_g=row(ln1_g), ln1_b=row(ln1_b), rwh=rwh, rwl=rwl, rb=router_b[:, None],
        w1=w1_b, w3=w3_b, w2=w2_b, ln2_g=row(ln2_g), ln2_b=row(ln2_b))


def _layer(l, xf, xb, p, bsz, seq):
    proj = _matmul(xb, p['w_main'][l], BF16, 1024, 1024, "in_proj")
    small = _matmul(xb, p['w_small'][l], F32, 1024, N_SMALL, "in_proj_small")
    ha = _branch_a(proj, p['conv_a_w'][l], p['conv_a_b'][l], p['w_gates'][l], p['lru_ba'][l], p['lru_bx'][l],
                   p['lru_lambda'][l], bsz, seq)
    hb = _branch_b(proj, small, p['conv_b_w'][l], p['conv_b_b'][l], p['dtb'][l], p['alog'][l], p['dskip'][l],
                   p['ssm_norm_w'][l], bsz, seq)
    cum = _fox_cum(small, p['fb'][l], bsz, seq)
    hc = _branch_c(proj, cum, bsz, seq)
    x1, x1b, rcols = _merge(ha, hb, hc, proj, xf, p['wa'][l], p['wb'][l], p['wc'][l], p['wo'][l],
                            p['gate_b'][l], p['ln1_g'][l], p['ln1_b'][l], p['rwh'], p['rwl'], p['rb'])
    x2, x2b = _moe_dense(x1b, x1, rcols, p['w1'][l], p['w3'][l], p['w2'][l], p['ln2_g'][l], p['ln2_b'][l])
    return dict(proj=proj, small=small, ha=ha, hb=hb, cum=cum, hc=hc, x1=x1, rcols=rcols, x2=x2, x2b=x2b)


def kernel(x, w_in, gate_b, conv_a_w, conv_a_b, lru_wa, lru_ba, lru_wx, lru_bx, lru_lambda,
           conv_b_w, conv_b_b, dt_bias, a_log, d_skip, ssm_norm_w, forget_b,
           w_branch_a, w_branch_b, w_branch_c, w_out, ln1_g, ln1_b,
           router_w, router_b, w1, w3, w2, ln2_g, ln2_b):
    bsz, seq, d = x.shape
    p = _prepare(w_in, gate_b, conv_a_w, conv_a_b, lru_wa, lru_ba, lru_wx, lru_bx, lru_lambda,
                 conv_b_w, conv_b_b, dt_bias, a_log, d_skip, ssm_norm_w, forget_b,
                 w_branch_a, w_branch_b, w_branch_c, w_out, ln1_g, ln1_b,
                 router_w, router_b, w1, w3, w2, ln2_g, ln2_b)
    xf = x.reshape(bsz * seq, d)
    xb = xf.astype(BF16)
    for l in range(w_in.shape[0]):
        stages = _layer(l, xf, xb, p, bsz, seq)
        xf, xb = stages['x2'], stages['x2b']
    return xf.reshape(bsz, seq, d)
```

```python
import functools

import jax
import jax.numpy as jnp
from jax import lax
from jax.experimental import pallas as pl
from jax.experimental.pallas import tpu as pltpu

F32 = jnp.float32
BF16 = jnp.bfloat16
HIGHEST = lax.Precision.HIGHEST

D_MODEL = 1024
DEPTH = 4
RNN_HEADS = 8
RNN_BLOCK = 128
CONV_WIDTH = 4
LRU_C = 8.0
SSM_HEADS = 16
SSM_HEAD_DIM = 64
SSM_GROUPS = 4
SSM_STATE = 128
SSM_CHUNK = 128
SSM_CONV_CH = 2048
ATTN_HEADS = 16
ATTN_HEAD_DIM = 64
N_EXPERTS = 16
EXPERTS_PER_GROUP = 4
D_EXPERT = 512
LN_EPS = 1e-5
RMS_EPS = 1e-6
DEEPNORM_ALPHA = (2 * DEPTH) ** 0.25

LANES = 128
SUBLANES = 8
VMEM_LIMIT = 48 * 1024 * 1024

COL_QKV = 0
COL_GATE = 3072
COL_XBC = 6144
COL_AX = 8192
COL_AGATE = 9216
COL_BZ = 10240
N_MAIN = 11264
N_SMALL = 128


def _cparams(sem):
    return pltpu.CompilerParams(dimension_semantics=sem, vmem_limit_bytes=VMEM_LIMIT)


def _mm_kernel(x_ref, w_ref, o_ref):
    o_ref[...] = jnp.dot(x_ref[...], w_ref[...], preferred_element_type=F32).astype(o_ref.dtype)


def _matmul(x, w, out_dtype, tm, tn, name):
    m, k = x.shape
    n = w.shape[1]
    return pl.pallas_call(
        _mm_kernel,
        grid=(m // tm, n // tn),
        in_specs=[pl.BlockSpec((tm, k), lambda i, j: (i, 0)),
                  pl.BlockSpec((k, tn), lambda i, j: (0, j))],
        out_specs=pl.BlockSpec((tm, tn), lambda i, j: (i, j)),
        out_shape=jax.ShapeDtypeStruct((m, n), out_dtype),
        compiler_params=_cparams(("parallel", "parallel")),
        name=name,
    )(x, w)


def _causal_conv(x, xbuf, cw_ref, cb_ref, first):
    ts = x.shape[0]

    @pl.when(first)
    def _():
        xbuf[0:SUBLANES, :] = jnp.zeros((SUBLANES, x.shape[1]), F32)

    xbuf[SUBLANES:SUBLANES + ts, :] = x
    y = cb_ref[...]
    for k in range(CONV_WIDTH):
        off = SUBLANES - (CONV_WIDTH - 1) + k
        y = y + cw_ref[k:k + 1, :] * xbuf[off:off + ts, :]
    xbuf[0:SUBLANES, :] = xbuf[ts:ts + SUBLANES, :]
    return y


def _rglru_kernel(x_ref, g_ref, cw_ref, cb_ref, wg_ref, ba_ref, bx_ref, lam_ref, o_ref, xbuf, hcar):
    s = pl.program_id(1)
    ts = x_ref.shape[0]
    first = s == 0

    @pl.when(first)
    def _():
        hcar[...] = jnp.zeros(hcar.shape, F32)

    xa = _causal_conv(x_ref[...].astype(F32), xbuf, cw_ref, cb_ref, first)
    xab = xa.astype(BF16)
    r_parts, i_parts = [], []
    for h in range(RNN_HEADS):
        pre = jnp.dot(xab[:, h * RNN_BLOCK:(h + 1) * RNN_BLOCK], wg_ref[h], preferred_element_type=F32)
        r_parts.append(pre[:, :RNN_BLOCK])
        i_parts.append(pre[:, RNN_BLOCK:])
    r_gate = jax.nn.sigmoid(jnp.concatenate(r_parts, axis=1) + ba_ref[...])
    i_gate = jax.nn.sigmoid(jnp.concatenate(i_parts, axis=1) + bx_ref[...])
    log_a = (-LRU_C) * r_gate * jax.nn.softplus(-lam_ref[...])
    a = jnp.exp(log_a)
    mult = jnp.sqrt(1.0 - jnp.exp(2.0 * log_a))
    u = (xa * i_gate) * mult

    ng = ts // SUBLANES
    a3 = a.reshape(ng, SUBLANES, D_MODEL)
    b3 = u.reshape(ng, SUBLANES, D_MODEL)
    row = lax.broadcasted_iota(jnp.int32, a3.shape, 1)
    d = 1
    while d < SUBLANES:
        valid = row >= d
        a_s = jnp.where(valid, pltpu.roll(a3, d, axis=1), 1.0)
        b_s = jnp.where(valid, pltpu.roll(b3, d, axis=1), 0.0)
        b3 = a3 * b_s + b3
        a3 = a3 * a_s
        d *= 2
    h_in = hcar[SUBLANES - 1:SUBLANES, :]
    groups = []
    for gi in range(ng):
        hg = b3[gi] + a3[gi] * h_in
        groups.append(hg)
        h_in = hg[SUBLANES - 1:SUBLANES, :]
    h = jnp.concatenate(groups, axis=0)
    hcar[...] = groups[-1]
    o_ref[...] = (h * jax.nn.gelu(g_ref[...].astype(F32))).astype(o_ref.dtype)


def _branch_a(proj, cw, cb, wg, ba, bx, lam, bsz, seq, ts=256):
    nst = seq // ts
    full = lambda shape: pl.BlockSpec(shape, lambda b, s: (0,) * len(shape))
    return pl.pallas_call(
        _rglru_kernel,
        grid=(bsz, nst),
        in_specs=[pl.BlockSpec((ts, D_MODEL), lambda b, s: (b * nst + s, COL_AX // D_MODEL)),
                  pl.BlockSpec((ts, D_MODEL), lambda b, s: (b * nst + s, COL_AGATE // D_MODEL)),
                  full((CONV_WIDTH, D_MODEL)), full((1, D_MODEL)),
                  full((RNN_HEADS, RNN_BLOCK, 2 * RNN_BLOCK)),
                  full((1, D_MODEL)), full((1, D_MODEL)), full((1, D_MODEL))],
        out_specs=pl.BlockSpec((ts, D_MODEL), lambda b, s: (b * nst + s, 0)),
        out_shape=jax.ShapeDtypeStruct((bsz * seq, D_MODEL), BF16),
        scratch_shapes=[pltpu.VMEM((ts + 2 * SUBLANES, D_MODEL), F32), pltpu.VMEM((SUBLANES, D_MODEL), F32)],
        compiler_params=_cparams(("parallel", "arbitrary")),
        name="rglru",
    )(proj, proj, cw, cb, wg, ba, bx, lam)


def _ssd_kernel(z_ref, xbc_ref, dtf_ref, cw_ref, cb_ref, dtb_ref, alog_ref, dskip_ref, nw_ref,
                o_ref, xbuf, state):
    c = pl.program_id(1)
    L = SSM_CHUNK
    first = c == 0

    @pl.when(first)
    def _():
        state[...] = jnp.zeros(state.shape, F32)

    conv = _causal_conv(xbc_ref[...].astype(F32), xbuf, cw_ref, cb_ref, first)
    act = conv * jax.nn.sigmoid(conv)
    xs = act[:, :D_MODEL]
    bm = act[:, D_MODEL:D_MODEL + SSM_GROUPS * SSM_STATE]
    cm = act[:, D_MODEL + SSM_GROUPS * SSM_STATE:]

    lane = lax.broadcasted_iota(jnp.int32, (L, LANES), 1)
    head_lane = lane < SSM_HEADS
    dt = jnp.where(head_lane, jax.nn.softplus(dtf_ref[...] + dtb_ref[...]), 0.0)
    a_dt = dt * (-jnp.exp(alog_ref[...]))
    ri = lax.broadcasted_iota(jnp.int32, (L, L), 0)
    ci = lax.broadcasted_iota(jnp.int32, (L, L), 1)
    causal = ri >= ci
    tril = jnp.where(causal, 1.0, 0.0).astype(F32)
    cs = jnp.dot(tril, a_dt, precision=HIGHEST, preferred_element_type=F32)
    cs_t = cs.T
    tot = cs[L - 1:L, :]
    dstate = jnp.exp(tot - cs)
    exp_cs = jnp.exp(cs)

    er = lax.broadcasted_iota(jnp.int32, (LANES, D_MODEL), 0)
    ec = lax.broadcasted_iota(jnp.int32, (LANES, D_MODEL), 1)
    expand = jnp.where(ec // SSM_HEAD_DIM == er, 1.0, 0.0).astype(F32)
    dt_e = jnp.dot(dt, expand, precision=HIGHEST, preferred_element_type=F32)
    dtds_e = jnp.dot(dt * dstate, expand, precision=HIGHEST, preferred_element_type=F32)
    tot_e = jnp.dot(jnp.broadcast_to(jnp.exp(tot), (SUBLANES, LANES)), expand,
                    precision=HIGHEST, preferred_element_type=F32)[0:1, :]
    xdt = xs * dt_e
    xdt_end = (xs * dtds_e).astype(BF16)

    lo_half = lax.broadcasted_iota(jnp.int32, (2 * L, LANES), 1) < SSM_HEAD_DIM
    heads_per_group = SSM_HEADS // SSM_GROUPS
    y_parts = []
    new_states = []
    for g in range(SSM_GROUPS):
        bg = bm[:, g * SSM_STATE:(g + 1) * SSM_STATE]
        cg = cm[:, g * SSM_STATE:(g + 1) * SSM_STATE]
        cb = lax.dot_general(cg.astype(BF16), bg.astype(BF16), (((1,), (1,)), ((), ())),
                             preferred_element_type=F32)
        lhs = []
        for e in range(heads_per_group):
            hd = g * heads_per_group + e
            colb = jnp.broadcast_to(cs[:, hd:hd + 1], (L, L))
            rowb = jnp.broadcast_to(cs_t[hd:hd + 1, :], (L, L))
            decay = jnp.exp(jnp.where(causal, colb - rowb, -jnp.inf))
            m = (cb * decay).astype(BF16)
            c_off = (cg * jnp.broadcast_to(exp_cs[:, hd:hd + 1], (L, L))).astype(BF16)
            lhs.append(jnp.concatenate([m, c_off], axis=1))
        for j in range(heads_per_group // 2):
            col = (g * heads_per_group + 2 * j) * SSM_HEAD_DIM
            rhs = jnp.concatenate([xdt[:, col:col + LANES], state[:, col:col + LANES]], axis=0).astype(BF16)
            zero = jnp.zeros_like(rhs)
            y_parts.append(jnp.dot(lhs[2 * j], jnp.where(lo_half, rhs, zero), preferred_element_type=F32)
                           + jnp.dot(lhs[2 * j + 1], jnp.where(lo_half, zero, rhs), preferred_element_type=F32))
        gw = heads_per_group * SSM_HEAD_DIM
        new_states.append(jnp.dot(bg.T.astype(BF16), xdt_end[:, g * gw:(g + 1) * gw],
                                  preferred_element_type=F32))
    y = jnp.concatenate(y_parts, axis=1)
    state[...] = state[...] * tot_e + jnp.concatenate(new_states, axis=1)

    y = y + xs * dskip_ref[...]
    z = z_ref[...].astype(F32)
    gy = y * (z * jax.nn.sigmoid(z))
    gw = D_MODEL // SSM_GROUPS
    outs = []
    for g in range(SSM_GROUPS):
        gg = gy[:, g * gw:(g + 1) * gw]
        ms = jnp.mean(gg * gg, axis=-1, keepdims=True)
        outs.append(gg * lax.rsqrt(ms + RMS_EPS))
    o_ref[...] = (jnp.concatenate(outs, axis=1) * nw_ref[...]).astype(o_ref.dtype)


def _branch_b(proj, small, cw, cb, dtb, alog, dskip_e, nw, bsz, seq):
    L = SSM_CHUNK
    nc = seq // L
    full = lambda shape: pl.BlockSpec(shape, lambda b, c: (0,) * len(shape))
    return pl.pallas_call(
        _ssd_kernel,
        grid=(bsz, nc),
        in_specs=[pl.BlockSpec((L, D_MODEL), lambda b, c: (b * nc + c, COL_BZ // D_MODEL)),
                  pl.BlockSpec((L, SSM_CONV_CH), lambda b, c: (b * nc + c, COL_XBC // SSM_CONV_CH)),
                  pl.BlockSpec((L, N_SMALL), lambda b, c: (b * nc + c, 0)),
                  full((CONV_WIDTH, SSM_CONV_CH)), full((1, SSM_CONV_CH)),
                  full((1, N_SMALL)), full((1, N_SMALL)), full((1, D_MODEL)), full((1, D_MODEL))],
        out_specs=pl.BlockSpec((L, D_MODEL), lambda b, c: (b * nc + c, 0)),
        out_shape=jax.ShapeDtypeStruct((bsz * seq, D_MODEL), BF16),
        scratch_shapes=[pltpu.VMEM((L + 2 * SUBLANES, SSM_CONV_CH), F32), pltpu.VMEM((SSM_STATE, D_MODEL), F32)],
        compiler_params=_cparams(("parallel", "arbitrary")),
        name="ssd",
    )(proj, proj, small, cw, cb, dtb, alog, dskip_e, nw)


CUM_BLOCK = 256
LOG2E = 1.4426950408889634
BIAS_PARTS = 3


def _fox_cum_kernel(dtf_ref, fb_ref, o_ref):
    seq = dtf_ref.shape[0]
    ri = lax.broadcasted_iota(jnp.int32, (CUM_BLOCK, CUM_BLOCK), 0)
    ci = lax.broadcasted_iota(jnp.int32, (CUM_BLOCK, CUM_BLOCK), 1)
    tril = jnp.where(ri >= ci, 1.0, 0.0).astype(F32)
    lane = lax.broadcasted_iota(jnp.int32, (CUM_BLOCK, LANES), 1)
    live = (lane >= SSM_HEADS) & (lane < SSM_HEADS + ATTN_HEADS)
    carry = jnp.zeros((1, LANES), F32)
    for i in range(seq // CUM_BLOCK):
        rows = slice(i * CUM_BLOCK, (i + 1) * CUM_BLOCK)
        logf = jnp.where(live, jax.nn.log_sigmoid(dtf_ref[rows, :] + fb_ref[...]), 0.0)
        cb = jnp.dot(tril, logf, precision=HIGHEST, preferred_element_type=F32) + carry
        carry = cb[CUM_BLOCK - 1:CUM_BLOCK, :]
        o_ref[rows, :] = cb


def _fox_cum(small, fb, bsz, seq):
    return pl.pallas_call(
        _fox_cum_kernel,
        grid=(bsz,),
        in_specs=[pl.BlockSpec((seq, N_SMALL), lambda b: (b, 0)),
                  pl.BlockSpec((1, N_SMALL), lambda b: (0, 0))],
        out_specs=pl.BlockSpec((seq, N_SMALL), lambda b: (b, 0)),
        out_shape=jax.ShapeDtypeStruct((bsz * seq, N_SMALL), F32),
        compiler_params=_cparams(("parallel",)),
        name="fox_cum",
    )(small, fb)


def _split3(x):
    hi = x.astype(BF16)
    r1 = x - hi.astype(F32)
    mid = r1.astype(BF16)
    lo = (r1 - mid.astype(F32)).astype(BF16)
    return hi, mid, lo


def _fox_attn_kernel(q_ref, k_ref, v_ref, cum_ref, o_ref, k0_s, k1_s, v0_s, v1_s, aug_s, *, tq):
    seq = q_ref.shape[0]
    hd = ATTN_HEAD_DIM
    pair = pl.program_id(1)
    lane = lax.broadcasted_iota(jnp.int32, (seq, LANES), 1)
    lo_half = lane < hd

    sr = lax.broadcasted_iota(jnp.int32, (LANES, LANES), 0)
    sc = lax.broadcasted_iota(jnp.int32, (LANES, LANES), 1)
    src0 = SSM_HEADS + 2 * pair
    parts = _split3(cum_ref[...] * (-LOG2E))
    aug = jnp.zeros((seq, LANES), F32)
    for j, part in enumerate(parts):
        sel = jnp.where(((sr == src0) & (sc == hd + j)) | ((sr == src0 + 1) & (sc == j)), 1.0, 0.0).astype(BF16)
        aug = aug + jnp.dot(part, sel, preferred_element_type=F32)
    aug_s[...] = aug
    half_lane = lane % hd
    k_ones = (half_lane >= BIAS_PARTS) & (half_lane < 2 * BIAS_PARTS)
    aug_k = jnp.where(k_ones, 1.0, aug).astype(BF16)
    k = k_ref[...]
    v = v_ref[...]
    zero = jnp.zeros_like(v)
    k0_s[...] = jnp.where(lo_half, k, aug_k)
    k1_s[...] = jnp.where(lo_half, aug_k, k)
    v0_s[...] = jnp.where(lo_half, v, zero)
    v1_s[...] = jnp.where(lo_half, zero, v)

    lane_q = lax.broadcasted_iota(jnp.int32, (tq, LANES), 1)
    lo_half_q = lane_q < hd
    half_q = lane_q % hd
    q_ones = half_q < BIAS_PARTS
    q_const = (half_q >= BIAS_PARTS) & (half_q < 2 * BIAS_PARTS)
    tri = lax.broadcasted_iota(jnp.int32, (tq, tq), 1) <= lax.broadcasted_iota(jnp.int32, (tq, tq), 0)
    nt = (((1,), (1,)), ((), ()))

    for qi in reversed(range(seq // tq)):
        q0 = qi * tq
        q = q_ref[q0:q0 + tq, :]
        c_row = jnp.broadcast_to(pltpu.roll(-aug_s[q0:q0 + 1, :], BIAS_PARTS, axis=1), (tq, LANES))
        q_aug = jnp.where(q_ones, 1.0, jnp.where(q_const, c_row, 0.0)).astype(BF16)
        q_heads = (jnp.where(lo_half_q, q, q_aug), jnp.where(lo_half_q, q_aug, q))
        out = None
        for hh, (k_s, v_s) in enumerate(((k0_s, v0_s), (k1_s, v1_s))):
            s_diag = lax.dot_general(q_heads[hh], k_s[q0:q0 + tq, :], nt, preferred_element_type=F32)
            s_diag = jnp.where(tri, s_diag, -jnp.inf)
            m = jnp.max(s_diag, axis=-1, keepdims=True)
            if qi > 0:
                s_off = lax.dot_general(q_heads[hh], k_s[0:q0, :], nt, preferred_element_type=F32)
                m = jnp.maximum(m, jnp.max(s_off, axis=-1, keepdims=True))
            p_diag = jnp.exp2(s_diag - m)
            l = jnp.sum(p_diag, axis=-1, keepdims=True)
            acc = jnp.dot(p_diag.astype(BF16), v_s[q0:q0 + tq, :], preferred_element_type=F32)
            if qi > 0:
                p_off = jnp.exp2(s_off - m)
                l = l + jnp.sum(p_off, axis=-1, keepdims=True)
                acc = acc + jnp.dot(p_off.astype(BF16), v_s[0:q0, :], preferred_element_type=F32)
            acc = acc * (1.0 / l)
            out = acc if out is None else out + acc
        o_ref[q0:q0 + tq, :] = out.astype(o_ref.dtype)


def _branch_c(proj, cum, bsz, seq, tq=256):
    npair = ATTN_HEADS // 2
    kv_scratch = pltpu.VMEM((seq, LANES), BF16)
    return pl.pallas_call(
        functools.partial(_fox_attn_kernel, tq=tq),
        grid=(bsz, npair),
        in_specs=[pl.BlockSpec((seq, LANES), lambda b, p: (b, COL_QKV // LANES + p)),
                  pl.BlockSpec((seq, LANES), lambda b, p: (b, COL_QKV // LANES + npair + p)),
                  pl.BlockSpec((seq, LANES), lambda b, p: (b, COL_QKV // LANES + 2 * npair + p)),
                  pl.BlockSpec((seq, N_SMALL), lambda b, p: (b, 0))],
        out_specs=pl.BlockSpec((seq, LANES), lambda b, p: (b, p)),
        out_shape=jax.ShapeDtypeStruct((bsz * seq, D_MODEL), BF16),
        scratch_shapes=[kv_scratch, kv_scratch, kv_scratch, kv_scratch, pltpu.VMEM((seq, LANES), F32)],
        compiler_params=_cparams(("parallel", "parallel")),
        name="fox_attn",
    )(proj, proj, proj, cum)


def _layer_norm(x, g, b):
    mu = jnp.mean(x, axis=-1, keepdims=True)
    xc = x - mu
    var = jnp.mean(xc * xc, axis=-1, keepdims=True)
    return xc * lax.rsqrt(var + LN_EPS) * g + b


def _top2_sum(a, b, c, d):
    hi1, lo1 = jnp.maximum(a, b), jnp.minimum(a, b)
    hi2, lo2 = jnp.maximum(c, d), jnp.minimum(c, d)
    return jnp.maximum(hi1, hi2) + jnp.maximum(jnp.minimum(hi1, hi2), jnp.maximum(lo1, lo2))


def _route_rows(logits_t):
    rows = [logits_t[e:e + 1, :] for e in range(N_EXPERTS)]
    mx = functools.reduce(jnp.maximum, rows)
    ex = [jnp.exp(r - mx) for r in rows]
    den = functools.reduce(jnp.add, ex)
    probs = [e / den for e in ex]
    ngroups = N_EXPERTS // EXPERTS_PER_GROUP
    scores = [_top2_sum(*probs[EXPERTS_PER_GROUP * g:EXPERTS_PER_GROUP * (g + 1)]) for g in range(ngroups)]
    best_g = jnp.zeros_like(mx, dtype=jnp.int32)
    best_s = scores[0]
    for g in range(1, ngroups):
        better = scores[g] > best_s
        best_g = jnp.where(better, g, best_g)
        best_s = jnp.where(better, scores[g], best_s)
    masked = [jnp.where(best_g == e // EXPERTS_PER_GROUP, probs[e], -1.0) for e in range(N_EXPERTS)]
    v1, i1 = masked[0], jnp.zeros_like(best_g)
    for e in range(1, N_EXPERTS):
        better = masked[e] > v1
        i1 = jnp.where(better, e, i1)
        v1 = jnp.where(better, masked[e], v1)
    v2, i2 = jnp.full_like(v1, -2.0), jnp.zeros_like(best_g)
    for e in range(N_EXPERTS):
        better = (masked[e] > v2) & (i1 != e)
        i2 = jnp.where(better, e, i2)
        v2 = jnp.where(better, masked[e], v2)
    tot = v1 + v2
    return i1, i2, v1 / tot, v2 / tot


def _merge_kernel(ha_ref, hb_ref, hc_ref, gate_ref, x_ref, wa_ref, wb_ref, wc_ref, wo_ref, gb_ref,
                  lg_ref, lb_ref, rwh_ref, rwl_ref, rb_ref,
                  x1_ref, x1b_ref, rcols_ref):
    tm = x_ref.shape[0]
    ya = jnp.dot(ha_ref[...], wa_ref[...], preferred_element_type=F32)
    yb = jnp.dot(hb_ref[...], wb_ref[...], preferred_element_type=F32)
    yc = jnp.dot(hc_ref[...], wc_ref[...], preferred_element_type=F32)
    g = jax.nn.sigmoid(gate_ref[...].astype(F32) + gb_ref[...])
    mixed_in = (g[:, :D_MODEL] * ya + g[:, D_MODEL:2 * D_MODEL] * yb + g[:, 2 * D_MODEL:] * yc).astype(BF16)
    mixed = jnp.dot(mixed_in, wo_ref[...], preferred_element_type=F32)
    x1 = _layer_norm(DEEPNORM_ALPHA * x_ref[...] + mixed, lg_ref[...], lb_ref[...])
    x1_ref[...] = x1
    x1h = x1.astype(BF16)
    x1b_ref[...] = x1h
    x1l = (x1 - x1h.astype(F32)).astype(BF16)
    nt = (((1,), (1,)), ((), ()))
    logits_t = (lax.dot_general(rwh_ref[...], x1h, nt, preferred_element_type=F32)
                + lax.dot_general(rwl_ref[...], x1h, nt, preferred_element_type=F32)
                + lax.dot_general(rwh_ref[...], x1l, nt, preferred_element_type=F32)
                + rb_ref[...])
    i1, i2, w1, w2 = _route_rows(logits_t)
    rows = [jnp.where(i1 == e, w1, 0.0) + jnp.where(i2 == e, w2, 0.0) for e in range(N_EXPERTS)]
    rows += [w1, w2, i1.astype(F32), i2.astype(F32)]
    nrow = 4 * SUBLANES
    sub = lax.broadcasted_iota(jnp.int32, (nrow, tm), 0)
    packed = jnp.zeros((nrow, tm), F32)
    for r, val in enumerate(rows):
        packed = jnp.where(sub == r, val, packed)
    packed = jnp.concatenate([packed, jnp.zeros((LANES - nrow, tm), F32)], axis=0)
    rcols_ref[...] = packed.T


def _merge(ha, hb, hc, proj, x, wa, wb, wc, wo, gb, lg, lb, rwh, rwl, rb, tm=512):
    t = x.shape[0]
    full = lambda shape: pl.BlockSpec(shape, lambda i: (0,) * len(shape))
    row = lambda w: pl.BlockSpec((tm, w), lambda i: (i, 0))
    return pl.pallas_call(
        _merge_kernel,
        grid=(t // tm,),
        in_specs=[row(D_MODEL), row(D_MODEL), row(D_MODEL),
                  pl.BlockSpec((tm, 3 * D_MODEL), lambda i: (i, COL_GATE // (3 * D_MODEL))),
                  row(D_MODEL),
                  full((D_MODEL, D_MODEL)), full((D_MODEL, D_MODEL)), full((D_MODEL, D_MODEL)),
                  full((D_MODEL, D_MODEL)), full((1, 3 * D_MODEL)),
                  full((1, D_MODEL)), full((1, D_MODEL)),
                  full((N_EXPERTS, D_MODEL)), full((N_EXPERTS, D_MODEL)), full((N_EXPERTS, 1))],
        out_specs=[row(D_MODEL), row(D_MODEL), row(LANES)],
        out_shape=[jax.ShapeDtypeStruct((t, D_MODEL), F32), jax.ShapeDtypeStruct((t, D_MODEL), BF16),
                   jax.ShapeDtypeStruct((t, LANES), F32)],
        compiler_params=_cparams(("parallel",)),
        name="merge",
    )(ha, hb, hc, proj, x, wa, wb, wc, wo, gb, lg, lb, rwh, rwl, rb)


def _moe_dense_kernel(xb_ref, x_ref, rc_ref, w1_ref, w3_ref, w2_ref, lg_ref, lb_ref, o_ref, ob_ref, acc):
    e = pl.program_id(1)

    @pl.when(e == 0)
    def _():
        acc[...] = jnp.zeros(acc.shape, F32)

    xb = xb_ref[...]
    h1 = jnp.dot(xb, w1_ref[0], preferred_element_type=F32)
    h3 = jnp.dot(xb, w3_ref[0], preferred_element_type=F32)
    h = (h1 * jax.nn.sigmoid(h1) * h3).astype(BF16)
    y = jnp.dot(h, w2_ref[0], preferred_element_type=F32)
    rc = rc_ref[...]
    lane = lax.broadcasted_iota(jnp.int32, rc.shape, 1)
    wcol = jnp.sum(jnp.where(lane == e, rc, 0.0), axis=-1, keepdims=True)
    acc[...] += wcol * y

    @pl.when(e == N_EXPERTS - 1)
    def _():
        x2 = _layer_norm(DEEPNORM_ALPHA * x_ref[...] + acc[...], lg_ref[...], lb_ref[...])
        o_ref[...] = x2
        ob_ref[...] = x2.astype(BF16)


def _moe_dense(x1b, x1, rcols, w1, w3, w2, lg, lb, tm=1024):
    t = x1.shape[0]
    row = lambda w: pl.BlockSpec((tm, w), lambda i, e: (i, 0))
    full = lambda shape: pl.BlockSpec(shape, lambda i, e: (0,) * len(shape))
    return pl.pallas_call(
        _moe_dense_kernel,
        grid=(t // tm, N_EXPERTS),
        in_specs=[row(D_MODEL), row(D_MODEL), row(LANES),
                  pl.BlockSpec((1, D_MODEL, D_EXPERT), lambda i, e: (e, 0, 0)),
                  pl.BlockSpec((1, D_MODEL, D_EXPERT), lambda i, e: (e, 0, 0)),
                  pl.BlockSpec((1, D_EXPERT, D_MODEL), lambda i, e: (e, 0, 0)),
                  full((1, D_MODEL)), full((1, D_MODEL))],
        out_specs=[row(D_MODEL), row(D_MODEL)],
        out_shape=[jax.ShapeDtypeStruct((t, D_MODEL), F32), jax.ShapeDtypeStruct((t, D_MODEL), BF16)],
        scratch_shapes=[pltpu.VMEM((tm, D_MODEL), F32)],
        compiler_params=_cparams(("parallel", "arbitrary")),
        name="moe_dense",
    )(x1b, x1, rcols, w1, w3, w2, lg, lb)


def _split_hi_lo(w):
    hi = w.astype(BF16)
    return hi, (w - hi.astype(F32)).astype(BF16)


def _prepare(w_in, gate_b, conv_a_w, conv_a_b, lru_wa, lru_ba, lru_wx, lru_bx, lru_lambda,
             conv_b_w, conv_b_b, dt_bias, a_log, d_skip, ssm_norm_w, forget_b,
             w_branch_a, w_branch_b, w_branch_c, w_out, ln1_g, ln1_b,
             router_w, router_b, w1, w3, w2, ln2_g, ln2_b):
    depth = w_in.shape[0]
    s0 = D_MODEL
    o_ax, o_ag, o_bz, o_xbc = 0, s0, 2 * s0, 3 * s0
    o_dt = o_xbc + SSM_CONV_CH
    o_qkv = o_dt + SSM_HEADS
    o_f = o_qkv + 3 * D_MODEL
    o_gate = o_f + ATTN_HEADS
    w_q = w_in[:, :, o_qkv:o_qkv + D_MODEL] * (ATTN_HEAD_DIM ** -0.5 * LOG2E)
    w_main = jnp.concatenate([w_q, w_in[:, :, o_qkv + D_MODEL:o_f], w_in[:, :, o_gate:], w_in[:, :, o_xbc:o_dt],
                              w_in[:, :, o_ax:o_ag], w_in[:, :, o_ag:o_bz], w_in[:, :, o_bz:o_xbc]],
                             axis=-1).astype(BF16)
    w_small = jnp.concatenate([w_in[:, :, o_dt:o_qkv], w_in[:, :, o_f:o_gate],
                               jnp.zeros((depth, D_MODEL, N_SMALL - SSM_HEADS - ATTN_HEADS), F32)],
                              axis=-1).astype(BF16)
    w_gates = jnp.concatenate([lru_wa, lru_wx], axis=-1).astype(BF16)
    pad_heads = lambda v, off: jnp.pad(v, ((0, 0), (off, N_SMALL - off - v.shape[1])))[:, None, :]
    dtb_p = pad_heads(dt_bias, 0)
    alog_p = pad_heads(a_log, 0)
    fb_p = pad_heads(forget_b, SSM_HEADS)
    dskip_e = jnp.repeat(d_skip, SSM_HEAD_DIM, axis=-1)[:, None, :]
    row = lambda v: v[:, None, :]
    wa_b, wb_b, wc_b, wo_b = (w.astype(BF16) for w in (w_branch_a, w_branch_b, w_branch_c, w_out))
    w1_b, w3_b, w2_b = w1.astype(BF16), w3.astype(BF16), w2.astype(BF16)
    rwh, rwl = _split_hi_lo(router_w.T)
    return dict(
        w_main=w_main, w_small=w_small, conv_a_w=conv_a_w, conv_a_b=row(conv_a_b), w_gates=w_gates,
        lru_ba=row(lru_ba), lru_bx=row(lru_bx), lru_lambda=row(lru_lambda),
        conv_b_w=conv_b_w, conv_b_b=row(conv_b_b), dtb=dtb_p, alog=alog_p, dskip=dskip_e,
        ssm_norm_w=row(ssm_norm_w), fb=fb_p, wa=wa_b, wb=wb_b, wc=wc_b, wo=wo_b, gate_b=row(gate_b),
        ln1_g=row(ln1_g), ln1_b=row(ln1_b), rwh=rwh, rwl=rwl, rb=router_b[:, None],
        w1=w1_b, w3=w3_b, w2=w2_b, ln2_g=row(ln2_g), ln2_b=row(ln2_b))


def _layer(l, xf, xb, p, bsz, seq):
    proj = _matmul(xb, p['w_main'][l], BF16, 1024, 1024, "in_proj")
    small = _matmul(xb, p['w_small'][l], F32, 1024, N_SMALL, "in_proj_small")
    ha = _branch_a(proj, p['conv_a_w'][l], p['conv_a_b'][l], p['w_gates'][l], p['lru_ba'][l], p['lru_bx'][l],
                   p['lru_lambda'][l], bsz, seq)
    hb = _branch_b(proj, small, p['conv_b_w'][l], p['conv_b_b'][l], p['dtb'][l], p['alog'][l], p['dskip'][l],
                   p['ssm_norm_w'][l], bsz, seq)
    cum = _fox_cum(small, p['fb'][l], bsz, seq)
    hc = _branch_c(proj, cum, bsz, seq)
    x1, x1b, rcols = _merge(ha, hb, hc, proj, xf, p['wa'][l], p['wb'][l], p['wc'][l], p['wo'][l],
                            p['gate_b'][l], p['ln1_g'][l], p['ln1_b'][l], p['rwh'], p['rwl'], p['rb'])
    x2, x2b = _moe_dense(x1b, x1, rcols, p['w1'][l], p['w3'][l], p['w2'][l], p['ln2_g'][l], p['ln2_b'][l])
    return dict(proj=proj, small=small, ha=ha, hb=hb, cum=cum, hc=hc, x1=x1, rcols=rcols, x2=x2, x2b=x2b)


def kernel(x, w_in, gate_b, conv_a_w, conv_a_b, lru_wa, lru_ba, lru_wx, lru_bx, lru_lambda,
           conv_b_w, conv_b_b, dt_bias, a_log, d_skip, ssm_norm_w, forget_b,
           w_branch_a, w_branch_b, w_branch_c, w_out, ln1_g, ln1_b,
           router_w, router_b, w1, w3, w2, ln2_g, ln2_b):
    bsz, seq, d = x.shape
    p = _prepare(w_in, gate_b, conv_a_w, conv_a_b, lru_wa, lru_ba, lru_wx, lru_bx, lru_lambda,
                 conv_b_w, conv_b_b, dt_bias, a_log, d_skip, ssm_norm_w, forget_b,
                 w_branch_a, w_branch_b, w_branch_c, w_out, ln1_g, ln1_b,
                 router_w, router_b, w1, w3, w2, ln2_g, ln2_b)
    xf = x.reshape(bsz * seq, d)
    xb = xf.astype(BF16)
    for l in range(w_in.shape[0]):
        stages = _layer(l, xf, xb, p, bsz, seq)
        xf, xb = stages['x2'], stages['x2b']
    return xf.reshape(bsz, seq, d)
```

```python
import functools

import jax
import jax.numpy as jnp
from jax import lax
from jax.experimental import pallas as pl
from jax.experimental.pallas import tpu as pltpu

F32 = jnp.float32
BF16 = jnp.bfloat16
HIGHEST = lax.Precision.HIGHEST

D_MODEL = 1024
DEPTH = 4
RNN_HEADS = 8
RNN_BLOCK = 128
CONV_WIDTH = 4
LRU_C = 8.0
SSM_HEADS = 16
SSM_HEAD_DIM = 64
SSM_GROUPS = 4
SSM_STATE = 128
SSM_CHUNK = 128
SSM_CONV_CH = 2048
ATTN_HEADS = 16
ATTN_HEAD_DIM = 64
N_EXPERTS = 16
EXPERTS_PER_GROUP = 4
D_EXPERT = 512
LN_EPS = 1e-5
RMS_EPS = 1e-6
DEEPNORM_ALPHA = (2 * DEPTH) ** 0.25

LANES = 128
SUBLANES = 8
VMEM_LIMIT = 48 * 1024 * 1024

COL_QKV = 0
COL_GATE = 3072
COL_XBC = 6144
COL_AX = 8192
COL_AGATE = 9216
COL_BZ = 10240
N_MAIN = 11264
N_SMALL = 128


def _cparams(sem):
    return pltpu.CompilerParams(dimension_semantics=sem, vmem_limit_bytes=VMEM_LIMIT)


def _mm_kernel(x_ref, w_ref, o_ref):
    o_ref[...] = jnp.dot(x_ref[...], w_ref[...], preferred_element_type=F32).astype(o_ref.dtype)


def _matmul(x, w, out_dtype, tm, tn, name):
    m, k = x.shape
    n = w.shape[1]
    return pl.pallas_call(
        _mm_kernel,
        grid=(m // tm, n // tn),
        in_specs=[pl.BlockSpec((tm, k), lambda i, j: (i, 0)),
                  pl.BlockSpec((k, tn), lambda i, j: (0, j))],
        out_specs=pl.BlockSpec((tm, tn), lambda i, j: (i, j)),
        out_shape=jax.ShapeDtypeStruct((m, n), out_dtype),
        compiler_params=_cparams(("parallel", "parallel")),
        name=name,
    )(x, w)


def _causal_conv(x, xbuf, cw_ref, cb_ref, first):
    ts = x.shape[0]

    @pl.when(first)
    def _():
        xbuf[0:SUBLANES, :] = jnp.zeros((SUBLANES, x.shape[1]), F32)

    xbuf[SUBLANES:SUBLANES + ts, :] = x
    y = cb_ref[...]
    for k in range(CONV_WIDTH):
        off = SUBLANES - (CONV_WIDTH - 1) + k
        y = y + cw_ref[k:k + 1, :] * xbuf[off:off + ts, :]
    xbuf[0:SUBLANES, :] = xbuf[ts:ts + SUBLANES, :]
    return y


def _rglru_kernel(x_ref, g_ref, cw_ref, cb_ref, wg_ref, ba_ref, bx_ref, lam_ref, o_ref, xbuf, hcar):
    s = pl.program_id(1)
    ts = x_ref.shape[0]
    first = s == 0

    @pl.when(first)
    def _():
        hcar[...] = jnp.zeros(hcar.shape, F32)

    xa = _causal_conv(x_ref[...].astype(F32), xbuf, cw_ref, cb_ref, first)
    xab = xa.astype(BF16)
    r_parts, i_parts = [], []
    for h in range(RNN_HEADS):
        pre = jnp.dot(xab[:, h * RNN_BLOCK:(h + 1) * RNN_BLOCK], wg_ref[h], preferred_element_type=F32)
        r_parts.append(pre[:, :RNN_BLOCK])
        i_parts.append(pre[:, RNN_BLOCK:])
    r_gate = jax.nn.sigmoid(jnp.concatenate(r_parts, axis=1) + ba_ref[...])
    i_gate = jax.nn.sigmoid(jnp.concatenate(i_parts, axis=1) + bx_ref[...])
    log_a = (-LRU_C) * r_gate * jax.nn.softplus(-lam_ref[...])
    a = jnp.exp(log_a)
    mult = jnp.sqrt(1.0 - jnp.exp(2.0 * log_a))
    u = (xa * i_gate) * mult

    ng = ts // SUBLANES
    a3 = a.reshape(ng, SUBLANES, D_MODEL)
    b3 = u.reshape(ng, SUBLANES, D_MODEL)
    row = lax.broadcasted_iota(jnp.int32, a3.shape, 1)
    d = 1
    while d < SUBLANES:
        valid = row >= d
        a_s = jnp.where(valid, pltpu.roll(a3, d, axis=1), 1.0)
        b_s = jnp.where(valid, pltpu.roll(b3, d, axis=1), 0.0)
        b3 = a3 * b_s + b3
        a3 = a3 * a_s
        d *= 2
    h_in = hcar[SUBLANES - 1:SUBLANES, :]
    groups = []
    for gi in range(ng):
        hg = b3[gi] + a3[gi] * h_in
        groups.append(hg)
        h_in = hg[SUBLANES - 1:SUBLANES, :]
    h = jnp.concatenate(groups, axis=0)
    hcar[...] = groups[-1]
    o_ref[...] = (h * jax.nn.gelu(g_ref[...].astype(F32))).astype(o_ref.dtype)


def _branch_a(proj, cw, cb, wg, ba, bx, lam, bsz, seq, ts=256):
    nst = seq // ts
    full = lambda shape: pl.BlockSpec(shape, lambda b, s: (0,) * len(shape))
    return pl.pallas_call(
        _rglru_kernel,
        grid=(bsz, nst),
        in_specs=[pl.BlockSpec((ts, D_MODEL), lambda b, s: (b * nst + s, COL_AX // D_MODEL)),
                  pl.BlockSpec((ts, D_MODEL), lambda b, s: (b * nst + s, COL_AGATE // D_MODEL)),
                  full((CONV_WIDTH, D_MODEL)), full((1, D_MODEL)),
                  full((RNN_HEADS, RNN_BLOCK, 2 * RNN_BLOCK)),
                  full((1, D_MODEL)), full((1, D_MODEL)), full((1, D_MODEL))],
        out_specs=pl.BlockSpec((ts, D_MODEL), lambda b, s: (b * nst + s, 0)),
        out_shape=jax.ShapeDtypeStruct((bsz * seq, D_MODEL), BF16),
        scratch_shapes=[pltpu.VMEM((ts + 2 * SUBLANES, D_MODEL), F32), pltpu.VMEM((SUBLANES, D_MODEL), F32)],
        compiler_params=_cparams(("parallel", "arbitrary")),
        name="rglru",
    )(proj, proj, cw, cb, wg, ba, bx, lam)


def _ssd_kernel(z_ref, xbc_ref, dtf_ref, cw_ref, cb_ref, dtb_ref, alog_ref, dskip_ref, nw_ref,
                o_ref, xbuf, state):
    c = pl.program_id(1)
    L = SSM_CHUNK
    first = c == 0

    @pl.when(first)
    def _():
        state[...] = jnp.zeros(state.shape, F32)

    conv = _causal_conv(xbc_ref[...].astype(F32), xbuf, cw_ref, cb_ref, first)
    act = conv * jax.nn.sigmoid(conv)
    xs = act[:, :D_MODEL]
    bm = act[:, D_MODEL:D_MODEL + SSM_GROUPS * SSM_STATE]
    cm = act[:, D_MODEL + SSM_GROUPS * SSM_STATE:]

    lane = lax.broadcasted_iota(jnp.int32, (L, LANES), 1)
    head_lane = lane < SSM_HEADS
    dt = jnp.where(head_lane, jax.nn.softplus(dtf_ref[...] + dtb_ref[...]), 0.0)
    a_dt = dt * (-jnp.exp(alog_ref[...]))
    ri = lax.broadcasted_iota(jnp.int32, (L, L), 0)
    ci = lax.broadcasted_iota(jnp.int32, (L, L), 1)
    causal = ri >= ci
    tril = jnp.where(causal, 1.0, 0.0).astype(F32)
    cs = jnp.dot(tril, a_dt, precision=HIGHEST, preferred_element_type=F32)
    cs_t = cs.T
    tot = cs[L - 1:L, :]
    dstate = jnp.exp(tot - cs)
    exp_cs = jnp.exp(cs)

    er = lax.broadcasted_iota(jnp.int32, (LANES, D_MODEL), 0)
    ec = lax.broadcasted_iota(jnp.int32, (LANES, D_MODEL), 1)
    expand = jnp.where(ec // SSM_HEAD_DIM == er, 1.0, 0.0).astype(F32)
    dt_e = jnp.dot(dt, expand, precision=HIGHEST, preferred_element_type=F32)
    dtds_e = jnp.dot(dt * dstate, expand, precision=HIGHEST, preferred_element_type=F32)
    tot_e = jnp.dot(jnp.broadcast_to(jnp.exp(tot), (SUBLANES, LANES)), expand,
                    precision=HIGHEST, preferred_element_type=F32)[0:1, :]
    xdt = xs * dt_e
    xdt_end = (xs * dtds_e).astype(BF16)

    lo_half = lax.broadcasted_iota(jnp.int32, (2 * L, LANES), 1) < SSM_HEAD_DIM
    heads_per_group = SSM_HEADS // SSM_GROUPS
    y_parts = []
    new_states = []
    for g in range(SSM_GROUPS):
        bg = bm[:, g * SSM_STATE:(g + 1) * SSM_STATE]
        cg = cm[:, g * SSM_STATE:(g + 1) * SSM_STATE]
        cb = lax.dot_general(cg.astype(BF16), bg.astype(BF16), (((1,), (1,)), ((), ())),
                             preferred_element_type=F32)
        lhs = []
        for e in range(heads_per_group):
            hd = g * heads_per_group + e
            colb = jnp.broadcast_to(cs[:, hd:hd + 1], (L, L))
            rowb = jnp.broadcast_to(cs_t[hd:hd + 1, :], (L, L))
            decay = jnp.exp(jnp.where(causal, colb - rowb, -jnp.inf))
            m = (cb * decay).astype(BF16)
            c_off = (cg * jnp.broadcast_to(exp_cs[:, hd:hd + 1], (L, L))).astype(BF16)
            lhs.append(jnp.concatenate([m, c_off], axis=1))
        for j in range(heads_per_group // 2):
            col = (g * heads_per_group + 2 * j) * SSM_HEAD_DIM
            rhs = jnp.concatenate([xdt[:, col:col + LANES], state[:, col:col + LANES]], axis=0).astype(BF16)
            zero = jnp.zeros_like(rhs)
            y_parts.append(jnp.dot(lhs[2 * j], jnp.where(lo_half, rhs, zero), preferred_element_type=F32)
                           + jnp.dot(lhs[2 * j + 1], jnp.where(lo_half, zero, rhs), preferred_element_type=F32))
        gw = heads_per_group * SSM_HEAD_DIM
        new_states.append(jnp.dot(bg.T.astype(BF16), xdt_end[:, g * gw:(g + 1) * gw],
                                  preferred_element_type=F32))
    y = jnp.concatenate(y_parts, axis=1)
    state[...] = state[...] * tot_e + jnp.concatenate(new_states, axis=1)

    y = y + xs * dskip_ref[...]
    z = z_ref[...].astype(F32)
    gy = y * (z * jax.nn.sigmoid(z))
    gw = D_MODEL // SSM_GROUPS
    outs = []
    for g in range(SSM_GROUPS):
        gg = gy[:, g * gw:(g + 1) * gw]
        ms = jnp.mean(gg * gg, axis=-1, keepdims=True)
        outs.append(gg * lax.rsqrt(ms + RMS_EPS))
    o_ref[...] = (jnp.concatenate(outs, axis=1) * nw_ref[...]).astype(o_ref.dtype)


def _branch_b(proj, small, cw, cb, dtb, alog, dskip_e, nw, bsz, seq):
    L = SSM_CHUNK
    nc = seq // L
    full = lambda shape: pl.BlockSpec(shape, lambda b, c: (0,) * len(shape))
    return pl.pallas_call(
        _ssd_kernel,
        grid=(bsz, nc),
        in_specs=[pl.BlockSpec((L, D_MODEL), lambda b, c: (b * nc + c, COL_BZ // D_MODEL)),
                  pl.BlockSpec((L, SSM_CONV_CH), lambda b, c: (b * nc + c, COL_XBC // SSM_CONV_CH)),
                  pl.BlockSpec((L, N_SMALL), lambda b, c: (b * nc + c, 0)),
                  full((CONV_WIDTH, SSM_CONV_CH)), full((1, SSM_CONV_CH)),
                  full((1, N_SMALL)), full((1, N_SMALL)), full((1, D_MODEL)), full((1, D_MODEL))],
        out_specs=pl.BlockSpec((L, D_MODEL), lambda b, c: (b * nc + c, 0)),
        out_shape=jax.ShapeDtypeStruct((bsz * seq, D_MODEL), BF16),
        scratch_shapes=[pltpu.VMEM((L + 2 * SUBLANES, SSM_CONV_CH), F32), pltpu.VMEM((SSM_STATE, D_MODEL), F32)],
        compiler_params=_cparams(("parallel", "arbitrary")),
        name="ssd",
    )(proj, proj, small, cw, cb, dtb, alog, dskip_e, nw)


CUM_BLOCK = 256
LOG2E = 1.4426950408889634
BIAS_PARTS = 3


def _fox_cum_kernel(dtf_ref, fb_ref, o_ref):
    seq = dtf_ref.shape[0]
    ri = lax.broadcasted_iota(jnp.int32, (CUM_BLOCK, CUM_BLOCK), 0)
    ci = lax.broadcasted_iota(jnp.int32, (CUM_BLOCK, CUM_BLOCK), 1)
    tril = jnp.where(ri >= ci, 1.0, 0.0).astype(F32)
    lane = lax.broadcasted_iota(jnp.int32, (CUM_BLOCK, LANES), 1)
    live = (lane >= SSM_HEADS) & (lane < SSM_HEADS + ATTN_HEADS)
    carry = jnp.zeros((1, LANES), F32)
    for i in range(seq // CUM_BLOCK):
        rows = slice(i * CUM_BLOCK, (i + 1) * CUM_BLOCK)
        logf = jnp.where(live, jax.nn.log_sigmoid(dtf_ref[rows, :] + fb_ref[...]), 0.0)
        cb = jnp.dot(tril, logf, precision=HIGHEST, preferred_element_type=F32) + carry
        carry = cb[CUM_BLOCK - 1:CUM_BLOCK, :]
        o_ref[rows, :] = cb


def _fox_cum(small, fb, bsz, seq):
    return pl.pallas_call(
        _fox_cum_kernel,
        grid=(bsz,),
        in_specs=[pl.BlockSpec((seq, N_SMALL), lambda b: (b, 0)),
                  pl.BlockSpec((1, N_SMALL), lambda b: (0, 0))],
        out_specs=pl.BlockSpec((seq, N_SMALL), lambda b: (b, 0)),
        out_shape=jax.ShapeDtypeStruct((bsz * seq, N_SMALL), F32),
        compiler_params=_cparams(("parallel",)),
        name="fox_cum",
    )(small, fb)


def _split3(x):
    hi = x.astype(BF16)
    r1 = x - hi.astype(F32)
    mid = r1.astype(BF16)
    lo = (r1 - mid.astype(F32)).astype(BF16)
    return hi, mid, lo


def _fox_attn_kernel(q_ref, k_ref, v_ref, cum_ref, o_ref, k0_s, k1_s, v0_s, v1_s, aug_s, *, tq):
    seq = q_ref.shape[0]
    hd = ATTN_HEAD_DIM
    pair = pl.program_id(1)
    lane = lax.broadcasted_iota(jnp.int32, (seq, LANES), 1)
    lo_half = lane < hd

    sr = lax.broadcasted_iota(jnp.int32, (LANES, LANES), 0)
    sc = lax.broadcasted_iota(jnp.int32, (LANES, LANES), 1)
    src0 = SSM_HEADS + 2 * pair
    parts = _split3(cum_ref[...] * (-LOG2E))
    aug = jnp.zeros((seq, LANES), F32)
    for j, part in enumerate(parts):
        sel = jnp.where(((sr == src0) & (sc == hd + j)) | ((sr == src0 + 1) & (sc == j)), 1.0, 0.0).astype(BF16)
        aug = aug + jnp.dot(part, sel, preferred_element_type=F32)
    aug_s[...] = aug
    half_lane = lane % hd
    k_ones = (half_lane >= BIAS_PARTS) & (half_lane < 2 * BIAS_PARTS)
    aug_k = jnp.where(k_ones, 1.0, aug).astype(BF16)
    k = k_ref[...]
    v = v_ref[...]
    zero = jnp.zeros_like(v)
    k0_s[...] = jnp.where(lo_half, k, aug_k)
    k1_s[...] = jnp.where(lo_half, aug_k, k)
    v0_s[...] = jnp.where(lo_half, v, zero)
    v1_s[...] = jnp.where(lo_half, zero, v)

    lane_q = lax.broadcasted_iota(jnp.int32, (tq, LANES), 1)
    lo_half_q = lane_q < hd
    half_q = lane_q % hd
    q_ones = half_q < BIAS_PARTS
    q_const = (half_q >= BIAS_PARTS) & (half_q < 2 * BIAS_PARTS)
    tri = lax.broadcasted_iota(jnp.int32, (tq, tq), 1) <= lax.broadcasted_iota(jnp.int32, (tq, tq), 0)
    nt = (((1,), (1,)), ((), ()))

    for qi in reversed(range(seq // tq)):
        q0 = qi * tq
        q = q_ref[q0:q0 + tq, :]
        c_row = jnp.broadcast_to(pltpu.roll(-aug_s[q0:q0 + 1, :], BIAS_PARTS, axis=1), (tq, LANES))
        q_aug = jnp.where(q_ones, 1.0, jnp.where(q_const, c_row, 0.0)).astype(BF16)
        q_heads = (jnp.where(lo_half_q, q, q_aug), jnp.where(lo_half_q, q_aug, q))
        out = None
        for hh, (k_s, v_s) in enumerate(((k0_s, v0_s), (k1_s, v1_s))):
            s_diag = lax.dot_general(q_heads[hh], k_s[q0:q0 + tq, :], nt, preferred_element_type=F32)
            s_diag = jnp.where(tri, s_diag, -jnp.inf)
            m = jnp.max(s_diag, axis=-1, keepdims=True)
            if qi > 0:
                s_off = lax.dot_general(q_heads[hh], k_s[0:q0, :], nt, preferred_element_type=F32)
                m = jnp.maximum(m, jnp.max(s_off, axis=-1, keepdims=True))
            p_diag = jnp.exp2(s_diag - m)
            l = jnp.sum(p_diag, axis=-1, keepdims=True)
            acc = jnp.dot(p_diag.astype(BF16), v_s[q0:q0 + tq, :], preferred_element_type=F32)
            if qi > 0:
                p_off = jnp.exp2(s_off - m)
                l = l + jnp.sum(p_off, axis=-1, keepdims=True)
                acc = acc + jnp.dot(p_off.astype(BF16), v_s[0:q0, :], preferred_element_type=F32)
            acc = acc * (1.0 / l)
            out = acc if out is None else out + acc
        o_ref[q0:q0 + tq, :] = out.astype(o_ref.dtype)


def _branch_c(proj, cum, bsz, seq, tq=256):
    npair = ATTN_HEADS // 2
    kv_scratch = pltpu.VMEM((seq, LANES), BF16)
    return pl.pallas_call(
        functools.partial(_fox_attn_kernel, tq=tq),
        grid=(bsz, npair),
        in_specs=[pl.BlockSpec((seq, LANES), lambda b, p: (b, COL_QKV // LANES + p)),
                  pl.BlockSpec((seq, LANES), lambda b, p: (b, COL_QKV // LANES + npair + p)),
                  pl.BlockSpec((seq, LANES), lambda b, p: (b, COL_QKV // LANES + 2 * npair + p)),
                  pl.BlockSpec((seq, N_SMALL), lambda b, p: (b, 0))],
        out_specs=pl.BlockSpec((seq, LANES), lambda b, p: (b, p)),
        out_shape=jax.ShapeDtypeStruct((bsz * seq, D_MODEL), BF16),
        scratch_shapes=[kv_scratch, kv_scratch, kv_scratch, kv_scratch, pltpu.VMEM((seq, LANES), F32)],
        compiler_params=_cparams(("parallel", "parallel")),
        name="fox_attn",
    )(proj, proj, proj, cum)


def _layer_norm(x, g, b):
    mu = jnp.mean(x, axis=-1, keepdims=True)
    xc = x - mu
    var = jnp.mean(xc * xc, axis=-1, keepdims=True)
    return xc * lax.rsqrt(var + LN_EPS) * g + b


def _top2_sum(a, b, c, d):
    hi1, lo1 = jnp.maximum(a, b), jnp.minimum(a, b)
    hi2, lo2 = jnp.maximum(c, d), jnp.minimum(c, d)
    return jnp.maximum(hi1, hi2) + jnp.maximum(jnp.minimum(hi1, hi2), jnp.maximum(lo1, lo2))


def _route_rows(logits_t):
    rows = [logits_t[e:e + 1, :] for e in range(N_EXPERTS)]
    mx = functools.reduce(jnp.maximum, rows)
    ex = [jnp.exp(r - mx) for r in rows]
    den = functools.reduce(jnp.add, ex)
    probs = [e / den for e in ex]
    ngroups = N_EXPERTS // EXPERTS_PER_GROUP
    scores = [_top2_sum(*probs[EXPERTS_PER_GROUP * g:EXPERTS_PER_GROUP * (g + 1)]) for g in range(ngroups)]
    best_g = jnp.zeros_like(mx, dtype=jnp.int32)
    best_s = scores[0]
    for g in range(1, ngroups):
        better = scores[g] > best_s
        best_g = jnp.where(better, g, best_g)
        best_s = jnp.where(better, scores[g], best_s)
    masked = [jnp.where(best_g == e // EXPERTS_PER_GROUP, probs[e], -1.0) for e in range(N_EXPERTS)]
    v1, i1 = masked[0], jnp.zeros_like(best_g)
    for e in range(1, N_EXPERTS):
        better = masked[e] > v1
        i1 = jnp.where(better, e, i1)
        v1 = jnp.where(better, masked[e], v1)
    v2, i2 = jnp.full_like(v1, -2.0), jnp.zeros_like(best_g)
    for e in range(N_EXPERTS):
        better = (masked[e] > v2) & (i1 != e)
        i2 = jnp.where(better, e, i2)
        v2 = jnp.where(better, masked[e], v2)
    tot = v1 + v2
    return i1, i2, v1 / tot, v2 / tot


def _merge_kernel(ha_ref, hb_ref, hc_ref, gate_ref, x_ref, wa_ref, wb_ref, wc_ref, wo_ref, gb_ref,
                  lg_ref, lb_ref, rwh_ref, rwl_ref, rb_ref,
                  x1_ref, rcols_ref, ids_ref):
    tm = x_ref.shape[0]
    ya = jnp.dot(ha_ref[...], wa_ref[...], preferred_element_type=F32)
    yb = jnp.dot(hb_ref[...], wb_ref[...], preferred_element_type=F32)
    yc = jnp.dot(hc_ref[...], wc_ref[...], preferred_element_type=F32)
    g = jax.nn.sigmoid(gate_ref[...].astype(F32) + gb_ref[...])
    mixed_in = (g[:, :D_MODEL] * ya + g[:, D_MODEL:2 * D_MODEL] * yb + g[:, 2 * D_MODEL:] * yc).astype(BF16)
    mixed = jnp.dot(mixed_in, wo_ref[...], preferred_element_type=F32)
    x1 = _layer_norm(DEEPNORM_ALPHA * x_ref[...] + mixed, lg_ref[...], lb_ref[...])
    x1_ref[...] = x1
    x1h = x1.astype(BF16)
    x1l = (x1 - x1h.astype(F32)).astype(BF16)
    nt = (((1,), (1,)), ((), ()))
    logits_t = (lax.dot_general(rwh_ref[...], x1h, nt, preferred_element_type=F32)
                + lax.dot_general(rwl_ref[...], x1h, nt, preferred_element_type=F32)
                + lax.dot_general(rwh_ref[...], x1l, nt, preferred_element_type=F32)
                + rb_ref[...])
    i1, i2, w1, w2 = _route_rows(logits_t)
    sub = lax.broadcasted_iota(jnp.int32, (SUBLANES, tm), 0)
    ids_ref[...] = jnp.where(sub == 0, i1, jnp.where(sub == 1, i2, 0))
    wrows = jnp.where(sub == 0, w1, jnp.where(sub == 1, w2, 0.0))
    wrows = jnp.concatenate([wrows, jnp.zeros((LANES - SUBLANES, tm), F32)], axis=0)
    rcols_ref[...] = wrows.T


def _merge(ha, hb, hc, proj, x, wa, wb, wc, wo, gb, lg, lb, rwh, rwl, rb, tm=512):
    t = x.shape[0]
    full = lambda shape: pl.BlockSpec(shape, lambda i: (0,) * len(shape))
    row = lambda w: pl.BlockSpec((tm, w), lambda i: (i, 0))
    return pl.pallas_call(
        _merge_kernel,
        grid=(t // tm,),
        in_specs=[row(D_MODEL), row(D_MODEL), row(D_MODEL),
                  pl.BlockSpec((tm, 3 * D_MODEL), lambda i: (i, COL_GATE // (3 * D_MODEL))),
                  row(D_MODEL),
                  full((D_MODEL, D_MODEL)), full((D_MODEL, D_MODEL)), full((D_MODEL, D_MODEL)),
                  full((D_MODEL, D_MODEL)), full((1, 3 * D_MODEL)),
                  full((1, D_MODEL)), full((1, D_MODEL)),
                  full((N_EXPERTS, D_MODEL)), full((N_EXPERTS, D_MODEL)), full((N_EXPERTS, 1))],
        out_specs=[row(D_MODEL), row(LANES), pl.BlockSpec((SUBLANES, tm), lambda i: (0, i))],
        out_shape=[jax.ShapeDtypeStruct((t, D_MODEL), F32), jax.ShapeDtypeStruct((t, LANES), F32),
                   jax.ShapeDtypeStruct((SUBLANES, t), jnp.int32)],
        compiler_params=_cparams(("parallel",)),
        name="merge",
    )(ha, hb, hc, proj, x, wa, wb, wc, wo, gb, lg, lb, rwh, rwl, rb)


TOP_K = 2
MOE_TILE = 512
MOE_TILE_SHIFT = 9
PLAN_BLOCK = 256


def _moe_rows(t):
    return TOP_K * t + N_EXPERTS * MOE_TILE


def _plan_kernel(ids_ref, pos_ref, meta_ref, cnt_s):
    t = ids_ref.shape[1]
    nblk = t // PLAN_BLOCK
    sub_e = lax.broadcasted_iota(jnp.int32, (N_EXPERTS, PLAN_BLOCK), 0)
    ur = lax.broadcasted_iota(jnp.int32, (PLAN_BLOCK, PLAN_BLOCK), 0)
    uc = lax.broadcasted_iota(jnp.int32, (PLAN_BLOCK, PLAN_BLOCK), 1)
    before = jnp.where(ur < uc, 1.0, 0.0).astype(BF16)

    def one_hots(c):
        cols = pl.ds(pl.multiple_of(c * PLAN_BLOCK, PLAN_BLOCK), PLAN_BLOCK)
        ids = ids_ref[:, cols]
        oh0 = jnp.where(ids[0:1, :] == sub_e, 1.0, 0.0)
        oh1 = jnp.where(ids[1:2, :] == sub_e, 1.0, 0.0)
        return cols, oh0, oh1

    def count_block(c, carry):
        cols, oh0, oh1 = one_hots(c)
        oh = oh0 + oh1
        cnt_s[:, cols] = jnp.dot(oh.astype(BF16), before, preferred_element_type=F32) + carry
        return carry + jnp.sum(oh, axis=1, keepdims=True)

    counts = lax.fori_loop(0, nblk, count_block, jnp.zeros((N_EXPERTS, 1), F32))
    padded = ((counts.astype(jnp.int32) + (MOE_TILE - 1)) >> MOE_TILE_SHIFT) << MOE_TILE_SHIFT
    padded = jnp.broadcast_to(padded, (N_EXPERTS, LANES))
    sub = lax.broadcasted_iota(jnp.int32, (N_EXPERTS, LANES), 0)
    lane = lax.broadcasted_iota(jnp.int32, (N_EXPERTS, LANES), 1)
    start = jnp.zeros((N_EXPERTS, LANES), jnp.int32)
    run = jnp.zeros((1, LANES), jnp.int32)
    for e in range(N_EXPERTS):
        start = jnp.where(sub == e, run, start)
        run = run + padded[e:e + 1, :]
    ended = jnp.where(start + padded <= lane * MOE_TILE, 1, 0)
    tile_expert = jnp.minimum(jnp.sum(ended, axis=0, keepdims=True), N_EXPERTS - 1)
    sub8 = lax.broadcasted_iota(jnp.int32, (SUBLANES, LANES), 0)
    meta_ref[...] = jnp.where(sub8 == 0, tile_expert, jnp.where(sub8 == 1, run >> MOE_TILE_SHIFT, 0))

    start_f = start[:, 0:1].astype(F32)
    sub8b = lax.broadcasted_iota(jnp.int32, (SUBLANES, PLAN_BLOCK), 0)

    def place_block(c, _):
        cols, oh0, oh1 = one_hots(c)
        base = cnt_s[:, cols] + start_f
        p0 = jnp.sum(oh0 * base, axis=0, keepdims=True).astype(jnp.int32)
        p1 = jnp.sum(oh1 * base, axis=0, keepdims=True).astype(jnp.int32)
        pos_ref[:, cols] = jnp.where(sub8b == 0, p0, jnp.where(sub8b == 1, p1, 0))
        return 0

    lax.fori_loop(0, nblk, place_block, 0)


def _plan(ids):
    t = ids.shape[1]
    return pl.pallas_call(
        _plan_kernel,
        out_shape=[jax.ShapeDtypeStruct((SUBLANES, t), jnp.int32),
                   jax.ShapeDtypeStruct((SUBLANES, LANES), jnp.int32)],
        scratch_shapes=[pltpu.VMEM((N_EXPERTS, t), F32)],
        compiler_params=pltpu.CompilerParams(vmem_limit_bytes=VMEM_LIMIT),
        name="moe_plan",
    )(ids)


DISPATCH_CHUNK = 256


def _row_copy(src, src_row, dst, dst_row, sem):
    return pltpu.make_async_copy(src.at[pl.ds(src_row, 1)], dst.at[pl.ds(dst_row, 1)], sem)


def _dispatch_kernel(pos_ref, x_hbm, xs_init_hbm, xs_hbm, sems):
    del xs_init_hbm
    t = x_hbm.shape[0]
    nchunk = t // DISPATCH_CHUNK

    def issue(c, slot):
        def body(i, _):
            tok = c * DISPATCH_CHUNK + i
            for k in range(TOP_K):
                _row_copy(x_hbm, tok, xs_hbm, pos_ref[k * t + tok], sems.at[slot]).start()
            return 0
        lax.fori_loop(0, DISPATCH_CHUNK, body, 0, unroll=8)

    def drain(slot):
        n = TOP_K * DISPATCH_CHUNK
        pltpu.make_async_copy(x_hbm.at[pl.ds(0, n)], xs_hbm.at[pl.ds(0, n)], sems.at[slot]).wait()

    issue(0, 0)

    def chunk(c, _):
        issue(c, c % 2)
        drain(1 - c % 2)
        return 0

    lax.fori_loop(1, nchunk, chunk, 0)
    drain((nchunk - 1) % 2)


def _dispatch(pos, x1):
    t = x1.shape[0]
    any_spec = pl.BlockSpec(memory_space=pl.ANY)
    xs0 = jnp.zeros((_moe_rows(t), D_MODEL), F32)
    return pl.pallas_call(
        _dispatch_kernel,
        grid_spec=pltpu.PrefetchScalarGridSpec(
            num_scalar_prefetch=1, grid=(1,),
            in_specs=[any_spec, any_spec], out_specs=any_spec,
            scratch_shapes=[pltpu.SemaphoreType.DMA((2,))]),
        out_shape=jax.ShapeDtypeStruct(xs0.shape, F32),
        input_output_aliases={2: 0},
        compiler_params=pltpu.CompilerParams(dimension_semantics=("arbitrary",)),
        name="moe_dispatch",
    )(pos, x1, xs0)


def _ffn_kernel(te_ref, nt_ref, xs_ref, w1_ref, w3_ref, w2_ref, ys_ref):
    del te_ref
    in_use = pl.program_id(0) < nt_ref[0]

    @pl.when(in_use)
    def _():
        xb = xs_ref[...].astype(BF16)
        h1 = jnp.dot(xb, w1_ref[0], preferred_element_type=F32)
        h3 = jnp.dot(xb, w3_ref[0], preferred_element_type=F32)
        h = (h1 * jax.nn.sigmoid(h1) * h3).astype(BF16)
        ys_ref[...] = jnp.dot(h, w2_ref[0], preferred_element_type=F32)

    @pl.when(jnp.logical_not(in_use))
    def _():
        ys_ref[...] = jnp.zeros(ys_ref.shape, F32)


def _ffn(tile_expert, ntiles, xs, w1, w3, w2):
    nrows = xs.shape[0]
    tile = lambda j, te, nt: (jnp.minimum(j, nt[0] - 1), 0)
    expert = lambda j, te, nt: (te[jnp.minimum(j, nt[0] - 1)], 0, 0)
    return pl.pallas_call(
        _ffn_kernel,
        grid_spec=pltpu.PrefetchScalarGridSpec(
            num_scalar_prefetch=2, grid=(nrows // MOE_TILE,),
            in_specs=[pl.BlockSpec((MOE_TILE, D_MODEL), tile),
                      pl.BlockSpec((1, D_MODEL, D_EXPERT), expert),
                      pl.BlockSpec((1, D_MODEL, D_EXPERT), expert),
                      pl.BlockSpec((1, D_EXPERT, D_MODEL), expert)],
            out_specs=pl.BlockSpec((MOE_TILE, D_MODEL), lambda j, te, nt: (j, 0))),
        out_shape=jax.ShapeDtypeStruct((nrows, D_MODEL), F32),
        compiler_params=_cparams(("arbitrary",)),
        name="moe_ffn",
    )(tile_expert, ntiles, xs, w1, w3, w2)


def _combine_kernel(pos_ref, ys_hbm, x1_ref, rc_ref, lg_ref, lb_ref, o_ref, ob_ref, g0, g1, sem):
    tm = x1_ref.shape[0]
    t = pos_ref.shape[0] // TOP_K
    base = pl.program_id(0) * tm

    def body(j, _):
        tok = base + j
        _row_copy(ys_hbm, pos_ref[tok], g0, j, sem.at[0]).start()
        _row_copy(ys_hbm, pos_ref[t + tok], g1, j, sem.at[0]).start()
        return 0

    lax.fori_loop(0, tm, body, 0, unroll=8)
    for g in (g0, g1):
        pltpu.make_async_copy(ys_hbm.at[pl.ds(0, tm)], g, sem.at[0]).wait()
    rc = rc_ref[...]
    y = rc[:, 0:1] * g0[...] + rc[:, 1:2] * g1[...]
    x2 = _layer_norm(DEEPNORM_ALPHA * x1_ref[...] + y, lg_ref[...], lb_ref[...])
    o_ref[...] = x2
    ob_ref[...] = x2.astype(BF16)


def _combine(pos, ys, x1, rcols, lg, lb, tm=256):
    t = x1.shape[0]
    row = lambda w: pl.BlockSpec((tm, w), lambda i, pos: (i, 0))
    full = lambda shape: pl.BlockSpec(shape, lambda i, pos: (0,) * len(shape))
    return pl.pallas_call(
        _combine_kernel,
        grid_spec=pltpu.PrefetchScalarGridSpec(
            num_scalar_prefetch=1, grid=(t // tm,),
            in_specs=[pl.BlockSpec(memory_space=pl.ANY), row(D_MODEL), row(LANES),
                      full((1, D_MODEL)), full((1, D_MODEL))],
            out_specs=[row(D_MODEL), row(D_MODEL)],
            scratch_shapes=[pltpu.VMEM((tm, D_MODEL), F32), pltpu.VMEM((tm, D_MODEL), F32),
                            pltpu.SemaphoreType.DMA((1,))]),
        out_shape=[jax.ShapeDtypeStruct((t, D_MODEL), F32), jax.ShapeDtypeStruct((t, D_MODEL), BF16)],
        compiler_params=_cparams(("arbitrary",)),
        name="moe_combine",
    )(pos, ys, x1, rcols, lg, lb)


def _moe(x1, rcols, ids, w1, w3, w2, lg, lb):
    pos8, meta = _plan(ids)
    pos = pos8[:TOP_K].reshape(-1)
    nt_max = _moe_rows(x1.shape[0]) // MOE_TILE
    xs = _dispatch(pos, x1)
    ys = _ffn(meta[0, :nt_max], meta[1, :1], xs, w1, w3, w2)
    return _combine(pos, ys, x1, rcols, lg, lb)


def _split_hi_lo(w):
    hi = w.astype(BF16)
    return hi, (w - hi.astype(F32)).astype(BF16)


def _prepare(w_in, gate_b, conv_a_w, conv_a_b, lru_wa, lru_ba, lru_wx, lru_bx, lru_lambda,
             conv_b_w, conv_b_b, dt_bias, a_log, d_skip, ssm_norm_w, forget_b,
             w_branch_a, w_branch_b, w_branch_c, w_out, ln1_g, ln1_b,
             router_w, router_b, w1, w3, w2, ln2_g, ln2_b):
    depth = w_in.shape[0]
    s0 = D_MODEL
    o_ax, o_ag, o_bz, o_xbc = 0, s0, 2 * s0, 3 * s0
    o_dt = o_xbc + SSM_CONV_CH
    o_qkv = o_dt + SSM_HEADS
    o_f = o_qkv + 3 * D_MODEL
    o_gate = o_f + ATTN_HEADS
    w_q = w_in[:, :, o_qkv:o_qkv + D_MODEL] * (ATTN_HEAD_DIM ** -0.5 * LOG2E)
    w_main = jnp.concatenate([w_q, w_in[:, :, o_qkv + D_MODEL:o_f], w_in[:, :, o_gate:], w_in[:, :, o_xbc:o_dt],
                              w_in[:, :, o_ax:o_ag], w_in[:, :, o_ag:o_bz], w_in[:, :, o_bz:o_xbc]],
                             axis=-1).astype(BF16)
    w_small = jnp.concatenate([w_in[:, :, o_dt:o_qkv], w_in[:, :, o_f:o_gate],
                               jnp.zeros((depth, D_MODEL, N_SMALL - SSM_HEADS - ATTN_HEADS), F32)],
                              axis=-1).astype(BF16)
    w_gates = jnp.concatenate([lru_wa, lru_wx], axis=-1).astype(BF16)
    pad_heads = lambda v, off: jnp.pad(v, ((0, 0), (off, N_SMALL - off - v.shape[1])))[:, None, :]
    dtb_p = pad_heads(dt_bias, 0)
    alog_p = pad_heads(a_log, 0)
    fb_p = pad_heads(forget_b, SSM_HEADS)
    dskip_e = jnp.repeat(d_skip, SSM_HEAD_DIM, axis=-1)[:, None, :]
    row = lambda v: v[:, None, :]
    wa_b, wb_b, wc_b, wo_b = (w.astype(BF16) for w in (w_branch_a, w_branch_b, w_branch_c, w_out))
    w1_b, w3_b, w2_b = w1.astype(BF16), w3.astype(BF16), w2.astype(BF16)
    rwh, rwl = _split_hi_lo(router_w.T)
    return dict(
        w_main=w_main, w_small=w_small, conv_a_w=conv_a_w, conv_a_b=row(conv_a_b), w_gates=w_gates,
        lru_ba=row(lru_ba), lru_bx=row(lru_bx), lru_lambda=row(lru_lambda),
        conv_b_w=conv_b_w, conv_b_b=row(conv_b_b), dtb=dtb_p, alog=alog_p, dskip=dskip_e,
        ssm_norm_w=row(ssm_norm_w), fb=fb_p, wa=wa_b, wb=wb_b, wc=wc_b, wo=wo_b, gate_b=row(gate_b),
        ln1_g=row(ln1_g), ln1_b=row(ln1_b), rwh=rwh, rwl=rwl, rb=router_b[:, None],
        w1=w1_b, w3=w3_b, w2=w2_b, ln2_g=row(ln2_g), ln2_b=row(ln2_b))


def _layer(l, xf, xb, p, bsz, seq):
    proj = _matmul(xb, p['w_main'][l], BF16, 1024, 1024, "in_proj")
    small = _matmul(xb, p['w_small'][l], F32, 1024, N_SMALL, "in_proj_small")
    ha = _branch_a(proj, p['conv_a_w'][l], p['conv_a_b'][l], p['w_gates'][l], p['lru_ba'][l], p['lru_bx'][l],
                   p['lru_lambda'][l], bsz, seq)
    hb = _branch_b(proj, small, p['conv_b_w'][l], p['conv_b_b'][l], p['dtb'][l], p['alog'][l], p['dskip'][l],
                   p['ssm_norm_w'][l], bsz, seq)
    cum = _fox_cum(small, p['fb'][l], bsz, seq)
    hc = _branch_c(proj, cum, bsz, seq)
    x1, rcols, ids = _merge(ha, hb, hc, proj, xf, p['wa'][l], p['wb'][l], p['wc'][l], p['wo'][l],
                            p['gate_b'][l], p['ln1_g'][l], p['ln1_b'][l], p['rwh'], p['rwl'], p['rb'])
    x2, x2b = _moe(x1, rcols, ids, p['w1'][l], p['w3'][l], p['w2'][l], p['ln2_g'][l], p['ln2_b'][l])
    return dict(proj=proj, small=small, ha=ha, hb=hb, cum=cum, hc=hc, x1=x1, rcols=rcols, ids=ids,
                x2=x2, x2b=x2b)


def kernel(x, w_in, gate_b, conv_a_w, conv_a_b, lru_wa, lru_ba, lru_wx, lru_bx, lru_lambda,
           conv_b_w, conv_b_b, dt_bias, a_log, d_skip, ssm_norm_w, forget_b,
           w_branch_a, w_branch_b, w_branch_c, w_out, ln1_g, ln1_b,
           router_w, router_b, w1, w3, w2, ln2_g, ln2_b):
    bsz, seq, d = x.shape
    p = _prepare(w_in, gate_b, conv_a_w, conv_a_b, lru_wa, lru_ba, lru_wx, lru_bx, lru_lambda,
                 conv_b_w, conv_b_b, dt_bias, a_log, d_skip, ssm_norm_w, forget_b,
                 w_branch_a, w_branch_b, w_branch_c, w_out, ln1_g, ln1_b,
                 router_w, router_b, w1, w3, w2, ln2_g, ln2_b)
    xf = x.reshape(bsz * seq, d)
    xb = xf.astype(BF16)
    for l in range(w_in.shape[0]):
        stages = _layer(l, xf, xb, p, bsz, seq)
        xf, xb = stages['x2'], stages['x2b']
    return xf.reshape(bsz, seq, d)
```

```python
import functools

import jax
import jax.numpy as jnp
from jax import lax
from jax.experimental import pallas as pl
from jax.experimental.pallas import tpu as pltpu

F32 = jnp.float32
BF16 = jnp.bfloat16
HIGHEST = lax.Precision.HIGHEST

D_MODEL = 1024
DEPTH = 4
RNN_HEADS = 8
RNN_BLOCK = 128
CONV_WIDTH = 4
LRU_C = 8.0
SSM_HEADS = 16
SSM_HEAD_DIM = 64
SSM_GROUPS = 4
SSM_STATE = 128
SSM_CHUNK = 128
SSM_CONV_CH = 2048
ATTN_HEADS = 16
ATTN_HEAD_DIM = 64
N_EXPERTS = 16
EXPERTS_PER_GROUP = 4
D_EXPERT = 512
LN_EPS = 1e-5
RMS_EPS = 1e-6
DEEPNORM_ALPHA = (2 * DEPTH) ** 0.25

LANES = 128
SUBLANES = 8
VMEM_LIMIT = 48 * 1024 * 1024

COL_QKV = 0
COL_GATE = 3072
COL_XBC = 6144
COL_AX = 8192
COL_AGATE = 9216
COL_BZ = 10240
N_MAIN = 11264
N_SMALL = 128


def _cparams(sem):
    return pltpu.CompilerParams(dimension_semantics=sem, vmem_limit_bytes=VMEM_LIMIT)


def _mm_kernel(x_ref, w_ref, o_ref):
    o_ref[...] = jnp.dot(x_ref[...], w_ref[...], preferred_element_type=F32).astype(o_ref.dtype)


def _matmul(x, w, out_dtype, tm, tn, name):
    m, k = x.shape
    n = w.shape[1]
    return pl.pallas_call(
        _mm_kernel,
        grid=(m // tm, n // tn),
        in_specs=[pl.BlockSpec((tm, k), lambda i, j: (i, 0)),
                  pl.BlockSpec((k, tn), lambda i, j: (0, j))],
        out_specs=pl.BlockSpec((tm, tn), lambda i, j: (i, j)),
        out_shape=jax.ShapeDtypeStruct((m, n), out_dtype),
        compiler_params=_cparams(("parallel", "parallel")),
        name=name,
    )(x, w)


def _causal_conv(x, xbuf, cw_ref, cb_ref, first):
    ts = x.shape[0]

    @pl.when(first)
    def _():
        xbuf[0:SUBLANES, :] = jnp.zeros((SUBLANES, x.shape[1]), F32)

    xbuf[SUBLANES:SUBLANES + ts, :] = x
    y = cb_ref[...]
    for k in range(CONV_WIDTH):
        off = SUBLANES - (CONV_WIDTH - 1) + k
        y = y + cw_ref[k:k + 1, :] * xbuf[off:off + ts, :]
    xbuf[0:SUBLANES, :] = xbuf[ts:ts + SUBLANES, :]
    return y


ROW_TILES = D_MODEL // LANES


def _store_token_tiles(ref, x):
    rows = x.shape[0]
    for s in range(ROW_TILES):
        ref[pl.ds(s, rows, stride=ROW_TILES), :] = x[:, s * LANES:(s + 1) * LANES]


def _load_token_tiles(ref, rows):
    return jnp.concatenate([ref[pl.ds(s, rows, stride=ROW_TILES), :] for s in range(ROW_TILES)], axis=1)


def _token_tile(ref, row):
    return ref.at[pl.ds(pl.multiple_of(row * ROW_TILES, ROW_TILES), ROW_TILES), :]


def _rglru_kernel(x_ref, g_ref, cw_ref, cb_ref, wg_ref, ba_ref, bx_ref, lam_ref, o_ref, xbuf, hcar):
    s = pl.program_id(1)
    ts = x_ref.shape[0]
    first = s == 0

    @pl.when(first)
    def _():
        hcar[...] = jnp.zeros(hcar.shape, F32)

    xa = _causal_conv(x_ref[...].astype(F32), xbuf, cw_ref, cb_ref, first)
    xab = xa.astype(BF16)
    r_parts, i_parts = [], []
    for h in range(RNN_HEADS):
        pre = jnp.dot(xab[:, h * RNN_BLOCK:(h + 1) * RNN_BLOCK], wg_ref[h], preferred_element_type=F32)
        r_parts.append(pre[:, :RNN_BLOCK])
        i_parts.append(pre[:, RNN_BLOCK:])
    r_gate = jax.nn.sigmoid(jnp.concatenate(r_parts, axis=1) + ba_ref[...])
    i_gate = jax.nn.sigmoid(jnp.concatenate(i_parts, axis=1) + bx_ref[...])
    log_a = (-LRU_C) * r_gate * jax.nn.softplus(-lam_ref[...])
    a = jnp.exp(log_a)
    mult = jnp.sqrt(1.0 - jnp.exp(2.0 * log_a))
    u = (xa * i_gate) * mult

    ng = ts // SUBLANES
    a3 = a.reshape(ng, SUBLANES, D_MODEL)
    b3 = u.reshape(ng, SUBLANES, D_MODEL)
    row = lax.broadcasted_iota(jnp.int32, a3.shape, 1)
    d = 1
    while d < SUBLANES:
        valid = row >= d
        a_s = jnp.where(valid, pltpu.roll(a3, d, axis=1), 1.0)
        b_s = jnp.where(valid, pltpu.roll(b3, d, axis=1), 0.0)
        b3 = a3 * b_s + b3
        a3 = a3 * a_s
        d *= 2
    h_in = hcar[SUBLANES - 1:SUBLANES, :]
    groups = []
    for gi in range(ng):
        hg = b3[gi] + a3[gi] * h_in
        groups.append(hg)
        h_in = hg[SUBLANES - 1:SUBLANES, :]
    h = jnp.concatenate(groups, axis=0)
    hcar[...] = groups[-1]
    o_ref[...] = (h * jax.nn.gelu(g_ref[...].astype(F32))).astype(o_ref.dtype)


def _branch_a(proj, cw, cb, wg, ba, bx, lam, bsz, seq, ts=256):
    nst = seq // ts
    full = lambda shape: pl.BlockSpec(shape, lambda b, s: (0,) * len(shape))
    return pl.pallas_call(
        _rglru_kernel,
        grid=(bsz, nst),
        in_specs=[pl.BlockSpec((ts, D_MODEL), lambda b, s: (b * nst + s, COL_AX // D_MODEL)),
                  pl.BlockSpec((ts, D_MODEL), lambda b, s: (b * nst + s, COL_AGATE // D_MODEL)),
                  full((CONV_WIDTH, D_MODEL)), full((1, D_MODEL)),
                  full((RNN_HEADS, RNN_BLOCK, 2 * RNN_BLOCK)),
                  full((1, D_MODEL)), full((1, D_MODEL)), full((1, D_MODEL))],
        out_specs=pl.BlockSpec((ts, D_MODEL), lambda b, s: (b * nst + s, 0)),
        out_shape=jax.ShapeDtypeStruct((bsz * seq, D_MODEL), BF16),
        scratch_shapes=[pltpu.VMEM((ts + 2 * SUBLANES, D_MODEL), F32), pltpu.VMEM((SUBLANES, D_MODEL), F32)],
        compiler_params=_cparams(("parallel", "arbitrary")),
        name="rglru",
    )(proj, proj, cw, cb, wg, ba, bx, lam)


def _ssd_kernel(z_ref, xbc_ref, dtf_ref, cw_ref, cb_ref, dtb_ref, alog_ref, dskip_ref, nw_ref,
                o_ref, xbuf, state):
    c = pl.program_id(1)
    L = SSM_CHUNK
    first = c == 0

    @pl.when(first)
    def _():
        state[...] = jnp.zeros(state.shape, F32)

    conv = _causal_conv(xbc_ref[...].astype(F32), xbuf, cw_ref, cb_ref, first)
    act = conv * jax.nn.sigmoid(conv)
    xs = act[:, :D_MODEL]
    bm = act[:, D_MODEL:D_MODEL + SSM_GROUPS * SSM_STATE]
    cm = act[:, D_MODEL + SSM_GROUPS * SSM_STATE:]

    lane = lax.broadcasted_iota(jnp.int32, (L, LANES), 1)
    head_lane = lane < SSM_HEADS
    dt = jnp.where(head_lane, jax.nn.softplus(dtf_ref[...] + dtb_ref[...]), 0.0)
    a_dt = dt * (-jnp.exp(alog_ref[...]))
    ri = lax.broadcasted_iota(jnp.int32, (L, L), 0)
    ci = lax.broadcasted_iota(jnp.int32, (L, L), 1)
    causal = ri >= ci
    tril = jnp.where(causal, 1.0, 0.0).astype(F32)
    cs = jnp.dot(tril, a_dt, precision=HIGHEST, preferred_element_type=F32)
    cs_t = cs.T
    tot = cs[L - 1:L, :]
    dstate = jnp.exp(tot - cs)
    exp_cs = jnp.exp(cs)

    er = lax.broadcasted_iota(jnp.int32, (LANES, D_MODEL), 0)
    ec = lax.broadcasted_iota(jnp.int32, (LANES, D_MODEL), 1)
    expand = jnp.where(ec // SSM_HEAD_DIM == er, 1.0, 0.0).astype(F32)
    dt_e = jnp.dot(dt, expand, precision=HIGHEST, preferred_element_type=F32)
    dtds_e = jnp.dot(dt * dstate, expand, precision=HIGHEST, preferred_element_type=F32)
    tot_e = jnp.dot(jnp.broadcast_to(jnp.exp(tot), (SUBLANES, LANES)), expand,
                    precision=HIGHEST, preferred_element_type=F32)[0:1, :]
    xdt = xs * dt_e
    xdt_end = (xs * dtds_e).astype(BF16)

    lo_half = lax.broadcasted_iota(jnp.int32, (2 * L, LANES), 1) < SSM_HEAD_DIM
    heads_per_group = SSM_HEADS // SSM_GROUPS
    y_parts = []
    new_states = []
    for g in range(SSM_GROUPS):
        bg = bm[:, g * SSM_STATE:(g + 1) * SSM_STATE]
        cg = cm[:, g * SSM_STATE:(g + 1) * SSM_STATE]
        cb = lax.dot_general(cg.astype(BF16), bg.astype(BF16), (((1,), (1,)), ((), ())),
                             preferred_element_type=F32)
        lhs = []
        for e in range(heads_per_group):
            hd = g * heads_per_group + e
            colb = jnp.broadcast_to(cs[:, hd:hd + 1], (L, L))
            rowb = jnp.broadcast_to(cs_t[hd:hd + 1, :], (L, L))
            decay = jnp.exp(jnp.where(causal, colb - rowb, -jnp.inf))
            m = (cb * decay).astype(BF16)
            c_off = (cg * jnp.broadcast_to(exp_cs[:, hd:hd + 1], (L, L))).astype(BF16)
            lhs.append(jnp.concatenate([m, c_off], axis=1))
        for j in range(heads_per_group // 2):
            col = (g * heads_per_group + 2 * j) * SSM_HEAD_DIM
            rhs = jnp.concatenate([xdt[:, col:col + LANES], state[:, col:col + LANES]], axis=0).astype(BF16)
            zero = jnp.zeros_like(rhs)
            y_parts.append(jnp.dot(lhs[2 * j], jnp.where(lo_half, rhs, zero), preferred_element_type=F32)
                           + jnp.dot(lhs[2 * j + 1], jnp.where(lo_half, zero, rhs), preferred_element_type=F32))
        gw = heads_per_group * SSM_HEAD_DIM
        new_states.append(jnp.dot(bg.T.astype(BF16), xdt_end[:, g * gw:(g + 1) * gw],
                                  preferred_element_type=F32))
    y = jnp.concatenate(y_parts, axis=1)
    state[...] = state[...] * tot_e + jnp.concatenate(new_states, axis=1)

    y = y + xs * dskip_ref[...]
    z = z_ref[...].astype(F32)
    gy = y * (z * jax.nn.sigmoid(z))
    gw = D_MODEL // SSM_GROUPS
    outs = []
    for g in range(SSM_GROUPS):
        gg = gy[:, g * gw:(g + 1) * gw]
        ms = jnp.mean(gg * gg, axis=-1, keepdims=True)
        outs.append(gg * lax.rsqrt(ms + RMS_EPS))
    o_ref[...] = (jnp.concatenate(outs, axis=1) * nw_ref[...]).astype(o_ref.dtype)


def _branch_b(proj, small, cw, cb, dtb, alog, dskip_e, nw, bsz, seq):
    L = SSM_CHUNK
    nc = seq // L
    full = lambda shape: pl.BlockSpec(shape, lambda b, c: (0,) * len(shape))
    return pl.pallas_call(
        _ssd_kernel,
        grid=(bsz, nc),
        in_specs=[pl.BlockSpec((L, D_MODEL), lambda b, c: (b * nc + c, COL_BZ // D_MODEL)),
                  pl.BlockSpec((L, SSM_CONV_CH), lambda b, c: (b * nc + c, COL_XBC // SSM_CONV_CH)),
                  pl.BlockSpec((L, N_SMALL), lambda b, c: (b * nc + c, 0)),
                  full((CONV_WIDTH, SSM_CONV_CH)), full((1, SSM_CONV_CH)),
                  full((1, N_SMALL)), full((1, N_SMALL)), full((1, D_MODEL)), full((1, D_MODEL))],
        out_specs=pl.BlockSpec((L, D_MODEL), lambda b, c: (b * nc + c, 0)),
        out_shape=jax.ShapeDtypeStruct((bsz * seq, D_MODEL), BF16),
        scratch_shapes=[pltpu.VMEM((L + 2 * SUBLANES, SSM_CONV_CH), F32), pltpu.VMEM((SSM_STATE, D_MODEL), F32)],
        compiler_params=_cparams(("parallel", "arbitrary")),
        name="ssd",
    )(proj, proj, small, cw, cb, dtb, alog, dskip_e, nw)


CUM_BLOCK = 256
LOG2E = 1.4426950408889634
BIAS_PARTS = 3


def _fox_cum_kernel(dtf_ref, fb_ref, o_ref):
    seq = dtf_ref.shape[0]
    ri = lax.broadcasted_iota(jnp.int32, (CUM_BLOCK, CUM_BLOCK), 0)
    ci = lax.broadcasted_iota(jnp.int32, (CUM_BLOCK, CUM_BLOCK), 1)
    tril = jnp.where(ri >= ci, 1.0, 0.0).astype(F32)
    lane = lax.broadcasted_iota(jnp.int32, (CUM_BLOCK, LANES), 1)
    live = (lane >= SSM_HEADS) & (lane < SSM_HEADS + ATTN_HEADS)
    carry = jnp.zeros((1, LANES), F32)
    for i in range(seq // CUM_BLOCK):
        rows = slice(i * CUM_BLOCK, (i + 1) * CUM_BLOCK)
        logf = jnp.where(live, jax.nn.log_sigmoid(dtf_ref[rows, :] + fb_ref[...]), 0.0)
        cb = jnp.dot(tril, logf, precision=HIGHEST, preferred_element_type=F32) + carry
        carry = cb[CUM_BLOCK - 1:CUM_BLOCK, :]
        o_ref[rows, :] = cb


def _fox_cum(small, fb, bsz, seq):
    return pl.pallas_call(
        _fox_cum_kernel,
        grid=(bsz,),
        in_specs=[pl.BlockSpec((seq, N_SMALL), lambda b: (b, 0)),
                  pl.BlockSpec((1, N_SMALL), lambda b: (0, 0))],
        out_specs=pl.BlockSpec((seq, N_SMALL), lambda b: (b, 0)),
        out_shape=jax.ShapeDtypeStruct((bsz * seq, N_SMALL), F32),
        compiler_params=_cparams(("parallel",)),
        name="fox_cum",
    )(small, fb)


def _split3(x):
    hi = x.astype(BF16)
    r1 = x - hi.astype(F32)
    mid = r1.astype(BF16)
    lo = (r1 - mid.astype(F32)).astype(BF16)
    return hi, mid, lo


def _fox_attn_kernel(q_ref, k_ref, v_ref, cum_ref, o_ref, k0_s, k1_s, v0_s, v1_s, aug_s, *, tq):
    seq = q_ref.shape[0]
    hd = ATTN_HEAD_DIM
    pair = pl.program_id(1)
    lane = lax.broadcasted_iota(jnp.int32, (seq, LANES), 1)
    lo_half = lane < hd

    sr = lax.broadcasted_iota(jnp.int32, (LANES, LANES), 0)
    sc = lax.broadcasted_iota(jnp.int32, (LANES, LANES), 1)
    src0 = SSM_HEADS + 2 * pair
    parts = _split3(cum_ref[...] * (-LOG2E))
    aug = jnp.zeros((seq, LANES), F32)
    for j, part in enumerate(parts):
        sel = jnp.where(((sr == src0) & (sc == hd + j)) | ((sr == src0 + 1) & (sc == j)), 1.0, 0.0).astype(BF16)
        aug = aug + jnp.dot(part, sel, preferred_element_type=F32)
    aug_s[...] = aug
    half_lane = lane % hd
    k_ones = (half_lane >= BIAS_PARTS) & (half_lane < 2 * BIAS_PARTS)
    aug_k = jnp.where(k_ones, 1.0, aug).astype(BF16)
    k = k_ref[...]
    v = v_ref[...]
    zero = jnp.zeros_like(v)
    k0_s[...] = jnp.where(lo_half, k, aug_k)
    k1_s[...] = jnp.where(lo_half, aug_k, k)
    v0_s[...] = jnp.where(lo_half, v, zero)
    v1_s[...] = jnp.where(lo_half, zero, v)

    lane_q = lax.broadcasted_iota(jnp.int32, (tq, LANES), 1)
    lo_half_q = lane_q < hd
    half_q = lane_q % hd
    q_ones = half_q < BIAS_PARTS
    q_const = (half_q >= BIAS_PARTS) & (half_q < 2 * BIAS_PARTS)
    tri = lax.broadcasted_iota(jnp.int32, (tq, tq), 1) <= lax.broadcasted_iota(jnp.int32, (tq, tq), 0)
    nt = (((1,), (1,)), ((), ()))

    for qi in reversed(range(seq // tq)):
        q0 = qi * tq
        q = q_ref[q0:q0 + tq, :]
        c_row = jnp.broadcast_to(pltpu.roll(-aug_s[q0:q0 + 1, :], BIAS_PARTS, axis=1), (tq, LANES))
        q_aug = jnp.where(q_ones, 1.0, jnp.where(q_const, c_row, 0.0)).astype(BF16)
        q_heads = (jnp.where(lo_half_q, q, q_aug), jnp.where(lo_half_q, q_aug, q))
        out = None
        for hh, (k_s, v_s) in enumerate(((k0_s, v0_s), (k1_s, v1_s))):
            s_diag = lax.dot_general(q_heads[hh], k_s[q0:q0 + tq, :], nt, preferred_element_type=F32)
            s_diag = jnp.where(tri, s_diag, -jnp.inf)
            m = jnp.max(s_diag, axis=-1, keepdims=True)
            if qi > 0:
                s_off = lax.dot_general(q_heads[hh], k_s[0:q0, :], nt, preferred_element_type=F32)
                m = jnp.maximum(m, jnp.max(s_off, axis=-1, keepdims=True))
            p_diag = jnp.exp2(s_diag - m)
            l = jnp.sum(p_diag, axis=-1, keepdims=True)
            acc = jnp.dot(p_diag.astype(BF16), v_s[q0:q0 + tq, :], preferred_element_type=F32)
            if qi > 0:
                p_off = jnp.exp2(s_off - m)
                l = l + jnp.sum(p_off, axis=-1, keepdims=True)
                acc = acc + jnp.dot(p_off.astype(BF16), v_s[0:q0, :], preferred_element_type=F32)
            acc = acc * (1.0 / l)
            out = acc if out is None else out + acc
        o_ref[q0:q0 + tq, :] = out.astype(o_ref.dtype)


def _branch_c(proj, cum, bsz, seq, tq=256):
    npair = ATTN_HEADS // 2
    kv_scratch = pltpu.VMEM((seq, LANES), BF16)
    return pl.pallas_call(
        functools.partial(_fox_attn_kernel, tq=tq),
        grid=(bsz, npair),
        in_specs=[pl.BlockSpec((seq, LANES), lambda b, p: (b, COL_QKV // LANES + p)),
                  pl.BlockSpec((seq, LANES), lambda b, p: (b, COL_QKV // LANES + npair + p)),
                  pl.BlockSpec((seq, LANES), lambda b, p: (b, COL_QKV // LANES + 2 * npair + p)),
                  pl.BlockSpec((seq, N_SMALL), lambda b, p: (b, 0))],
        out_specs=pl.BlockSpec((seq, LANES), lambda b, p: (b, p)),
        out_shape=jax.ShapeDtypeStruct((bsz * seq, D_MODEL), BF16),
        scratch_shapes=[kv_scratch, kv_scratch, kv_scratch, kv_scratch, pltpu.VMEM((seq, LANES), F32)],
        compiler_params=_cparams(("parallel", "parallel")),
        name="fox_attn",
    )(proj, proj, proj, cum)


def _layer_norm(x, g, b):
    mu = jnp.mean(x, axis=-1, keepdims=True)
    xc = x - mu
    var = jnp.mean(xc * xc, axis=-1, keepdims=True)
    return xc * lax.rsqrt(var + LN_EPS) * g + b


def _top2_sum(a, b, c, d):
    hi1, lo1 = jnp.maximum(a, b), jnp.minimum(a, b)
    hi2, lo2 = jnp.maximum(c, d), jnp.minimum(c, d)
    return jnp.maximum(hi1, hi2) + jnp.maximum(jnp.minimum(hi1, hi2), jnp.maximum(lo1, lo2))


def _route_rows(logits_t):
    rows = [logits_t[e:e + 1, :] for e in range(N_EXPERTS)]
    mx = functools.reduce(jnp.maximum, rows)
    ex = [jnp.exp(r - mx) for r in rows]
    den = functools.reduce(jnp.add, ex)
    probs = [e / den for e in ex]
    ngroups = N_EXPERTS // EXPERTS_PER_GROUP
    scores = [_top2_sum(*probs[EXPERTS_PER_GROUP * g:EXPERTS_PER_GROUP * (g + 1)]) for g in range(ngroups)]
    best_g = jnp.zeros_like(mx, dtype=jnp.int32)
    best_s = scores[0]
    for g in range(1, ngroups):
        better = scores[g] > best_s
        best_g = jnp.where(better, g, best_g)
        best_s = jnp.where(better, scores[g], best_s)
    masked = [jnp.where(best_g == e // EXPERTS_PER_GROUP, probs[e], -1.0) for e in range(N_EXPERTS)]
    v1, i1 = masked[0], jnp.zeros_like(best_g)
    for e in range(1, N_EXPERTS):
        better = masked[e] > v1
        i1 = jnp.where(better, e, i1)
        v1 = jnp.where(better, masked[e], v1)
    v2, i2 = jnp.full_like(v1, -2.0), jnp.zeros_like(best_g)
    for e in range(N_EXPERTS):
        better = (masked[e] > v2) & (i1 != e)
        i2 = jnp.where(better, e, i2)
        v2 = jnp.where(better, masked[e], v2)
    tot = v1 + v2
    return i1, i2, v1 / tot, v2 / tot


def _merge_kernel(ha_ref, hb_ref, hc_ref, gate_ref, x_ref, wa_ref, wb_ref, wc_ref, wo_ref, gb_ref,
                  lg_ref, lb_ref, rwh_ref, rwl_ref, rb_ref,
                  x1t_ref, rcols_ref, ids_ref):
    tm = x_ref.shape[0]
    ya = jnp.dot(ha_ref[...], wa_ref[...], preferred_element_type=F32)
    yb = jnp.dot(hb_ref[...], wb_ref[...], preferred_element_type=F32)
    yc = jnp.dot(hc_ref[...], wc_ref[...], preferred_element_type=F32)
    g = jax.nn.sigmoid(gate_ref[...].astype(F32) + gb_ref[...])
    mixed_in = (g[:, :D_MODEL] * ya + g[:, D_MODEL:2 * D_MODEL] * yb + g[:, 2 * D_MODEL:] * yc).astype(BF16)
    mixed = jnp.dot(mixed_in, wo_ref[...], preferred_element_type=F32)
    x1 = _layer_norm(DEEPNORM_ALPHA * x_ref[...] + mixed, lg_ref[...], lb_ref[...])
    _store_token_tiles(x1t_ref, x1)
    x1h = x1.astype(BF16)
    x1l = (x1 - x1h.astype(F32)).astype(BF16)
    nt = (((1,), (1,)), ((), ()))
    logits_t = (lax.dot_general(rwh_ref[...], x1h, nt, preferred_element_type=F32)
                + lax.dot_general(rwl_ref[...], x1h, nt, preferred_element_type=F32)
                + lax.dot_general(rwh_ref[...], x1l, nt, preferred_element_type=F32)
                + rb_ref[...])
    i1, i2, w1, w2 = _route_rows(logits_t)
    sub = lax.broadcasted_iota(jnp.int32, (SUBLANES, tm), 0)
    ids_ref[...] = jnp.where(sub == 0, i1, jnp.where(sub == 1, i2, 0))
    wrows = jnp.where(sub == 0, w1, jnp.where(sub == 1, w2, 0.0))
    wrows = jnp.concatenate([wrows, jnp.zeros((LANES - SUBLANES, tm), F32)], axis=0)
    rcols_ref[...] = wrows.T


def _merge(ha, hb, hc, proj, x, wa, wb, wc, wo, gb, lg, lb, rwh, rwl, rb, tm=512):
    t = x.shape[0]
    full = lambda shape: pl.BlockSpec(shape, lambda i: (0,) * len(shape))
    row = lambda w: pl.BlockSpec((tm, w), lambda i: (i, 0))
    return pl.pallas_call(
        _merge_kernel,
        grid=(t // tm,),
        in_specs=[row(D_MODEL), row(D_MODEL), row(D_MODEL),
                  pl.BlockSpec((tm, 3 * D_MODEL), lambda i: (i, COL_GATE // (3 * D_MODEL))),
                  row(D_MODEL),
                  full((D_MODEL, D_MODEL)), full((D_MODEL, D_MODEL)), full((D_MODEL, D_MODEL)),
                  full((D_MODEL, D_MODEL)), full((1, 3 * D_MODEL)),
                  full((1, D_MODEL)), full((1, D_MODEL)),
                  full((N_EXPERTS, D_MODEL)), full((N_EXPERTS, D_MODEL)), full((N_EXPERTS, 1))],
        out_specs=[pl.BlockSpec((tm * ROW_TILES, LANES), lambda i: (i, 0)), row(LANES),
                   pl.BlockSpec((SUBLANES, tm), lambda i: (0, i))],
        out_shape=[jax.ShapeDtypeStruct((t * ROW_TILES, LANES), F32), jax.ShapeDtypeStruct((t, LANES), F32),
                   jax.ShapeDtypeStruct((SUBLANES, t), jnp.int32)],
        compiler_params=_cparams(("parallel",)),
        name="merge",
    )(ha, hb, hc, proj, x, wa, wb, wc, wo, gb, lg, lb, rwh, rwl, rb)


TOP_K = 2
MOE_TILE = 512
MOE_TILE_SHIFT = 9
PLAN_BLOCK = 256


def _moe_rows(t):
    return TOP_K * t + (N_EXPERTS + 1) * MOE_TILE


def _plan_kernel(ids_ref, pos_ref, meta_ref, cnt_s):
    t = ids_ref.shape[1]
    nblk = t // PLAN_BLOCK
    sub_e = lax.broadcasted_iota(jnp.int32, (N_EXPERTS, PLAN_BLOCK), 0)
    ur = lax.broadcasted_iota(jnp.int32, (PLAN_BLOCK, PLAN_BLOCK), 0)
    uc = lax.broadcasted_iota(jnp.int32, (PLAN_BLOCK, PLAN_BLOCK), 1)
    before = jnp.where(ur < uc, 1.0, 0.0).astype(BF16)

    def one_hots(c):
        cols = pl.ds(pl.multiple_of(c * PLAN_BLOCK, PLAN_BLOCK), PLAN_BLOCK)
        ids = ids_ref[:, cols]
        oh0 = jnp.where(ids[0:1, :] == sub_e, 1.0, 0.0)
        oh1 = jnp.where(ids[1:2, :] == sub_e, 1.0, 0.0)
        return cols, oh0, oh1

    def count_block(c, carry):
        cols, oh0, oh1 = one_hots(c)
        oh = oh0 + oh1
        cnt_s[:, cols] = jnp.dot(oh.astype(BF16), before, preferred_element_type=F32) + carry
        return carry + jnp.sum(oh, axis=1, keepdims=True)

    counts = lax.fori_loop(0, nblk, count_block, jnp.zeros((N_EXPERTS, 1), F32))
    padded = ((counts.astype(jnp.int32) + (MOE_TILE - 1)) >> MOE_TILE_SHIFT) << MOE_TILE_SHIFT
    padded = jnp.broadcast_to(padded, (N_EXPERTS, LANES))
    sub = lax.broadcasted_iota(jnp.int32, (N_EXPERTS, LANES), 0)
    lane = lax.broadcasted_iota(jnp.int32, (N_EXPERTS, LANES), 1)
    start = jnp.zeros((N_EXPERTS, LANES), jnp.int32)
    run = jnp.zeros((1, LANES), jnp.int32)
    for e in range(N_EXPERTS):
        start = jnp.where(sub == e, run, start)
        run = run + padded[e:e + 1, :]
    ended = jnp.where(start + padded <= lane * MOE_TILE, 1, 0)
    tile_expert = jnp.minimum(jnp.sum(ended, axis=0, keepdims=True), N_EXPERTS - 1)
    first_pad = start + jnp.broadcast_to(counts.astype(jnp.int32), (N_EXPERTS, LANES))
    first_pad = jnp.sum(jnp.where(sub == lane, first_pad, 0), axis=0, keepdims=True)
    sub8 = lax.broadcasted_iota(jnp.int32, (SUBLANES, LANES), 0)
    meta_ref[...] = jnp.where(sub8 == 0, tile_expert,
                              jnp.where(sub8 == 1, run >> MOE_TILE_SHIFT, jnp.where(sub8 == 2, first_pad, 0)))

    start_f = start[:, 0:1].astype(F32)
    sub8b = lax.broadcasted_iota(jnp.int32, (SUBLANES, PLAN_BLOCK), 0)

    def place_block(c, _):
        cols, oh0, oh1 = one_hots(c)
        base = cnt_s[:, cols] + start_f
        p0 = jnp.sum(oh0 * base, axis=0, keepdims=True).astype(jnp.int32)
        p1 = jnp.sum(oh1 * base, axis=0, keepdims=True).astype(jnp.int32)
        pos_ref[:, cols] = jnp.where(sub8b == 0, p0, jnp.where(sub8b == 1, p1, 0))
        return 0

    lax.fori_loop(0, nblk, place_block, 0)


def _plan(ids):
    t = ids.shape[1]
    return pl.pallas_call(
        _plan_kernel,
        out_shape=[jax.ShapeDtypeStruct((SUBLANES, t), jnp.int32),
                   jax.ShapeDtypeStruct((SUBLANES, LANES), jnp.int32)],
        scratch_shapes=[pltpu.VMEM((N_EXPERTS, t), F32)],
        compiler_params=pltpu.CompilerParams(vmem_limit_bytes=VMEM_LIMIT),
        name="moe_plan",
    )(ids)


def _dispatch_kernel(pos_ref, pad_ref, nt_ref, x_ref, xs_hbm, zeros, sems):
    tm = x_ref.shape[0] // ROW_TILES
    t = pos_ref.shape[0] // TOP_K
    base = pl.program_id(0) * tm
    first = pl.program_id(0) == 0
    tile_rows = MOE_TILE * ROW_TILES
    unused = [(j, pltpu.make_async_copy(zeros, xs_hbm.at[pl.ds(j * tile_rows, tile_rows), :], sems.at[1]))
              for j in range(xs_hbm.shape[0] // tile_rows)]

    @pl.when(first)
    def _():
        zeros[...] = jnp.zeros(zeros.shape, F32)
        fills = [pltpu.make_async_copy(zeros, xs_hbm.at[pl.ds(pad_ref[e] * ROW_TILES, tile_rows), :], sems.at[0])
                 for e in range(N_EXPERTS)]
        for f in fills:
            f.start()
        for f in fills:
            f.wait()
        for j, fill in unused:
            pl.when(j >= nt_ref[0])(fill.start)

    def body(j, _):
        for k in range(TOP_K):
            pltpu.make_async_copy(_token_tile(x_ref, j), _token_tile(xs_hbm, pos_ref[k * t + base + j]),
                                  sems.at[0]).start()
        return 0

    lax.fori_loop(0, tm, body, 0, unroll=8)
    for k in range(TOP_K):
        pltpu.make_async_copy(x_ref, xs_hbm.at[pl.ds(0, tm * ROW_TILES), :], sems.at[0]).wait()

    @pl.when(first)
    def _():
        for j, fill in unused:
            pl.when(j >= nt_ref[0])(fill.wait)


def _dispatch(pos, first_pad, ntiles, x1t, tm=512):
    t = x1t.shape[0] // ROW_TILES
    return pl.pallas_call(
        _dispatch_kernel,
        grid_spec=pltpu.PrefetchScalarGridSpec(
            num_scalar_prefetch=3, grid=(t // tm,),
            in_specs=[pl.BlockSpec((tm * ROW_TILES, LANES), lambda i, pos, pad, nt: (i, 0))],
            out_specs=pl.BlockSpec(memory_space=pl.ANY),
            scratch_shapes=[pltpu.VMEM((MOE_TILE * ROW_TILES, LANES), F32), pltpu.SemaphoreType.DMA((2,))]),
        out_shape=jax.ShapeDtypeStruct((_moe_rows(t) * ROW_TILES, LANES), F32),
        compiler_params=_cparams(("arbitrary",)),
        name="moe_dispatch",
    )(pos, first_pad, ntiles, x1t)


def _ffn_kernel(te_ref, nt_ref, xs_ref, w1_ref, w3_ref, w2_ref, ys_ref):
    del te_ref
    in_use = pl.program_id(0) < nt_ref[0]

    @pl.when(in_use)
    def _():
        xb = _load_token_tiles(xs_ref, MOE_TILE).astype(BF16)
        h1 = jnp.dot(xb, w1_ref[0], preferred_element_type=F32)
        h3 = jnp.dot(xb, w3_ref[0], preferred_element_type=F32)
        h = (h1 * jax.nn.sigmoid(h1) * h3).astype(BF16)
        _store_token_tiles(ys_ref, jnp.dot(h, w2_ref[0], preferred_element_type=F32))

    @pl.when(jnp.logical_not(in_use))
    def _():
        ys_ref[...] = jnp.zeros(ys_ref.shape, F32)


def _ffn(tile_expert, ntiles, xs, w1, w3, w2):
    ntile = xs.shape[0] // (MOE_TILE * ROW_TILES)
    tile = lambda j, te, nt: (jnp.minimum(j, nt[0] - 1), 0)
    expert = lambda j, te, nt: (te[jnp.minimum(j, nt[0] - 1)], 0, 0)
    return pl.pallas_call(
        _ffn_kernel,
        grid_spec=pltpu.PrefetchScalarGridSpec(
            num_scalar_prefetch=2, grid=(ntile,),
            in_specs=[pl.BlockSpec((MOE_TILE * ROW_TILES, LANES), tile),
                      pl.BlockSpec((1, D_MODEL, D_EXPERT), expert),
                      pl.BlockSpec((1, D_MODEL, D_EXPERT), expert),
                      pl.BlockSpec((1, D_EXPERT, D_MODEL), expert)],
            out_specs=pl.BlockSpec((MOE_TILE * ROW_TILES, LANES), lambda j, te, nt: (j, 0))),
        out_shape=jax.ShapeDtypeStruct(xs.shape, F32),
        compiler_params=_cparams(("arbitrary",)),
        name="moe_ffn",
    )(tile_expert, ntiles, xs, w1, w3, w2)


def _combine_kernel(pos_ref, ys_hbm, x1t_ref, rc_ref, lg_ref, lb_ref, o_ref, ob_ref, gath, sems):
    tm = o_ref.shape[0]
    t = pos_ref.shape[0] // TOP_K
    i = pl.program_id(0)
    ntile = pl.num_programs(0)

    def issue(tile, slot):
        def body(j, _):
            for k in range(TOP_K):
                pltpu.make_async_copy(_token_tile(ys_hbm, pos_ref[k * t + tile * tm + j]),
                                      _token_tile(gath.at[slot, k], j), sems.at[slot]).start()
            return 0
        lax.fori_loop(0, tm, body, 0, unroll=8)

    @pl.when(i == 0)
    def _():
        issue(0, 0)

    @pl.when(i + 1 < ntile)
    def _():
        issue(i + 1, (i + 1) % 2)

    slot = i % 2
    for k in range(TOP_K):
        pltpu.make_async_copy(ys_hbm.at[pl.ds(0, tm * ROW_TILES), :], gath.at[slot, k], sems.at[slot]).wait()
    rc = rc_ref[...]
    y = (rc[:, 0:1] * _load_token_tiles(gath.at[slot, 0], tm)
         + rc[:, 1:2] * _load_token_tiles(gath.at[slot, 1], tm))
    x2 = _layer_norm(DEEPNORM_ALPHA * _load_token_tiles(x1t_ref, tm) + y, lg_ref[...], lb_ref[...])
    o_ref[...] = x2
    ob_ref[...] = x2.astype(BF16)


def _combine(pos, ys, x1t, rcols, lg, lb, tm=256):
    t = x1t.shape[0] // ROW_TILES
    row = lambda w: pl.BlockSpec((tm, w), lambda i, pos: (i, 0))
    full = lambda shape: pl.BlockSpec(shape, lambda i, pos: (0,) * len(shape))
    return pl.pallas_call(
        _combine_kernel,
        grid_spec=pltpu.PrefetchScalarGridSpec(
            num_scalar_prefetch=1, grid=(t // tm,),
            in_specs=[pl.BlockSpec(memory_space=pl.ANY),
                      pl.BlockSpec((tm * ROW_TILES, LANES), lambda i, pos: (i, 0)), row(LANES),
                      full((1, D_MODEL)), full((1, D_MODEL))],
            out_specs=[row(D_MODEL), row(D_MODEL)],
            scratch_shapes=[pltpu.VMEM((2, TOP_K, tm * ROW_TILES, LANES), F32), pltpu.SemaphoreType.DMA((2,))]),
        out_shape=[jax.ShapeDtypeStruct((t, D_MODEL), F32), jax.ShapeDtypeStruct((t, D_MODEL), BF16)],
        compiler_params=_cparams(("arbitrary",)),
        name="moe_combine",
    )(pos, ys, x1t, rcols, lg, lb)


def _moe(x1t, rcols, ids, w1, w3, w2, lg, lb):
    pos8, meta = _plan(ids)
    pos = pos8[:TOP_K].reshape(-1)
    nt_max = _moe_rows(x1t.shape[0] // ROW_TILES) // MOE_TILE
    ntiles = meta[1, :1]
    xs = _dispatch(pos, meta[2, :N_EXPERTS], ntiles, x1t)
    ys = _ffn(meta[0, :nt_max], ntiles, xs, w1, w3, w2)
    return _combine(pos, ys, x1t, rcols, lg, lb)


def _split_hi_lo(w):
    hi = w.astype(BF16)
    return hi, (w - hi.astype(F32)).astype(BF16)


def _prepare(w_in, gate_b, conv_a_w, conv_a_b, lru_wa, lru_ba, lru_wx, lru_bx, lru_lambda,
             conv_b_w, conv_b_b, dt_bias, a_log, d_skip, ssm_norm_w, forget_b,
             w_branch_a, w_branch_b, w_branch_c, w_out, ln1_g, ln1_b,
             router_w, router_b, w1, w3, w2, ln2_g, ln2_b):
    depth = w_in.shape[0]
    s0 = D_MODEL
    o_ax, o_ag, o_bz, o_xbc = 0, s0, 2 * s0, 3 * s0
    o_dt = o_xbc + SSM_CONV_CH
    o_qkv = o_dt + SSM_HEADS
    o_f = o_qkv + 3 * D_MODEL
    o_gate = o_f + ATTN_HEADS
    w_q = w_in[:, :, o_qkv:o_qkv + D_MODEL] * (ATTN_HEAD_DIM ** -0.5 * LOG2E)
    w_main = jnp.concatenate([w_q, w_in[:, :, o_qkv + D_MODEL:o_f], w_in[:, :, o_gate:], w_in[:, :, o_xbc:o_dt],
                              w_in[:, :, o_ax:o_ag], w_in[:, :, o_ag:o_bz], w_in[:, :, o_bz:o_xbc]],
                             axis=-1).astype(BF16)
    w_small = jnp.concatenate([w_in[:, :, o_dt:o_qkv], w_in[:, :, o_f:o_gate],
                               jnp.zeros((depth, D_MODEL, N_SMALL - SSM_HEADS - ATTN_HEADS), F32)],
                              axis=-1).astype(BF16)
    w_gates = jnp.concatenate([lru_wa, lru_wx], axis=-1).astype(BF16)
    pad_heads = lambda v, off: jnp.pad(v, ((0, 0), (off, N_SMALL - off - v.shape[1])))[:, None, :]
    dtb_p = pad_heads(dt_bias, 0)
    alog_p = pad_heads(a_log, 0)
    fb_p = pad_heads(forget_b, SSM_HEADS)
    dskip_e = jnp.repeat(d_skip, SSM_HEAD_DIM, axis=-1)[:, None, :]
    row = lambda v: v[:, None, :]
    wa_b, wb_b, wc_b, wo_b = (w.astype(BF16) for w in (w_branch_a, w_branch_b, w_branch_c, w_out))
    w1_b, w3_b, w2_b = w1.astype(BF16), w3.astype(BF16), w2.astype(BF16)
    rwh, rwl = _split_hi_lo(router_w.T)
    return dict(
        w_main=w_main, w_small=w_small, conv_a_w=conv_a_w, conv_a_b=row(conv_a_b), w_gates=w_gates,
        lru_ba=row(lru_ba), lru_bx=row(lru_bx), lru_lambda=row(lru_lambda),
        conv_b_w=conv_b_w, conv_b_b=row(conv_b_b), dtb=dtb_p, alog=alog_p, dskip=dskip_e,
        ssm_norm_w=row(ssm_norm_w), fb=fb_p, wa=wa_b, wb=wb_b, wc=wc_b, wo=wo_b, gate_b=row(gate_b),
        ln1_g=row(ln1_g), ln1_b=row(ln1_b), rwh=rwh, rwl=rwl, rb=router_b[:, None],
        w1=w1_b, w3=w3_b, w2=w2_b, ln2_g=row(ln2_g), ln2_b=row(ln2_b))


def _layer(l, xf, xb, p, bsz, seq):
    proj = _matmul(xb, p['w_main'][l], BF16, 1024, 1024, "in_proj")
    small = _matmul(xb, p['w_small'][l], F32, 1024, N_SMALL, "in_proj_small")
    ha = _branch_a(proj, p['conv_a_w'][l], p['conv_a_b'][l], p['w_gates'][l], p['lru_ba'][l], p['lru_bx'][l],
                   p['lru_lambda'][l], bsz, seq)
    hb = _branch_b(proj, small, p['conv_b_w'][l], p['conv_b_b'][l], p['dtb'][l], p['alog'][l], p['dskip'][l],
                   p['ssm_norm_w'][l], bsz, seq)
    cum = _fox_cum(small, p['fb'][l], bsz, seq)
    hc = _branch_c(proj, cum, bsz, seq)
    x1, rcols, ids = _merge(ha, hb, hc, proj, xf, p['wa'][l], p['wb'][l], p['wc'][l], p['wo'][l],
                            p['gate_b'][l], p['ln1_g'][l], p['ln1_b'][l], p['rwh'], p['rwl'], p['rb'])
    x2, x2b = _moe(x1, rcols, ids, p['w1'][l], p['w3'][l], p['w2'][l], p['ln2_g'][l], p['ln2_b'][l])
    return dict(proj=proj, small=small, ha=ha, hb=hb, cum=cum, hc=hc, x1=x1, rcols=rcols, ids=ids,
                x2=x2, x2b=x2b)


def kernel(x, w_in, gate_b, conv_a_w, conv_a_b, lru_wa, lru_ba, lru_wx, lru_bx, lru_lambda,
           conv_b_w, conv_b_b, dt_bias, a_log, d_skip, ssm_norm_w, forget_b,
           w_branch_a, w_branch_b, w_branch_c, w_out, ln1_g, ln1_b,
           router_w, router_b, w1, w3, w2, ln2_g, ln2_b):
    bsz, seq, d = x.shape
    p = _prepare(w_in, gate_b, conv_a_w, conv_a_b, lru_wa, lru_ba, lru_wx, lru_bx, lru_lambda,
                 conv_b_w, conv_b_b, dt_bias, a_log, d_skip, ssm_norm_w, forget_b,
                 w_branch_a, w_branch_b, w_branch_c, w_out, ln1_g, ln1_b,
                 router_w, router_b, w1, w3, w2, ln2_g, ln2_b)
    xf = x.reshape(bsz * seq, d)
    xb = xf.astype(BF16)
    for l in range(w_in.shape[0]):
        stages = _layer(l, xf, xb, p, bsz, seq)
        xf, xb = stages['x2'], stages['x2b']
    return xf.reshape(bsz, seq, d)
```

```python
import functools

import jax
import jax.numpy as jnp
from jax import lax
from jax.experimental import pallas as pl
from jax.experimental.pallas import tpu as pltpu

F32 = jnp.float32
BF16 = jnp.bfloat16

D_MODEL = 1024
DEPTH = 4
RNN_HEADS = 8
RNN_BLOCK = 128
CONV_WIDTH = 4
LRU_C = 8.0
SSM_HEADS = 16
SSM_HEAD_DIM = 64
SSM_GROUPS = 4
SSM_STATE = 128
SSM_CHUNK = 128
SSM_CONV_CH = 2048
ATTN_HEADS = 16
ATTN_HEAD_DIM = 64
N_EXPERTS = 16
EXPERTS_PER_GROUP = 4
D_EXPERT = 512
LN_EPS = 1e-5
RMS_EPS = 1e-6
DEEPNORM_ALPHA = (2 * DEPTH) ** 0.25

LANES = 128
SUBLANES = 8
VMEM_LIMIT = 48 * 1024 * 1024

COL_QKV = 0
COL_GATE = 3072
COL_XBC = 6144
COL_AX = 8192
COL_AGATE = 9216
COL_BZ = 10240
N_MAIN = 11264
N_SMALL = 128


def _cparams(sem):
    return pltpu.CompilerParams(dimension_semantics=sem, vmem_limit_bytes=VMEM_LIMIT)


def _mm_kernel(x_ref, w_ref, o_ref):
    o_ref[...] = jnp.dot(x_ref[...], w_ref[...], preferred_element_type=F32).astype(o_ref.dtype)


def _matmul(x, w, out_dtype, tm, tn, name):
    m, k = x.shape
    n = w.shape[1]
    return pl.pallas_call(
        _mm_kernel,
        grid=(m // tm, n // tn),
        in_specs=[pl.BlockSpec((tm, k), lambda i, j: (i, 0)),
                  pl.BlockSpec((k, tn), lambda i, j: (0, j))],
        out_specs=pl.BlockSpec((tm, tn), lambda i, j: (i, j)),
        out_shape=jax.ShapeDtypeStruct((m, n), out_dtype),
        compiler_params=_cparams(("parallel", "parallel")),
        name=name,
    )(x, w)


CONV_BAND = 128
BF16_ROWS = 16


def _causal_conv(x, hist, cw_ref, cb_ref, first):
    ts, ch = x.shape

    @pl.when(first)
    def _():
        hist[...] = jnp.zeros(hist.shape, BF16)

    ext = jnp.concatenate([hist[...], x], axis=0)
    hist[...] = x[ts - BF16_ROWS:ts, :]
    taps = CONV_WIDTH - 1
    out_row = lax.broadcasted_iota(jnp.int32, (taps * CONV_BAND, CONV_BAND + BF16_ROWS), 0)
    in_row = lax.broadcasted_iota(jnp.int32, (taps * CONV_BAND, CONV_BAND + BF16_ROWS), 1)
    tap = out_row // CONV_BAND
    shift = jnp.where(in_row == (out_row - tap * CONV_BAND) + BF16_ROWS - taps + tap, 1.0, 0.0).astype(BF16)
    bands = []
    for b0 in range(0, ts, CONV_BAND):
        shifted = jnp.dot(shift, ext[b0:b0 + CONV_BAND + BF16_ROWS, :], preferred_element_type=F32)
        y = cb_ref[...] + cw_ref[taps:taps + 1, :] * x[b0:b0 + CONV_BAND, :].astype(F32)
        for k in range(taps):
            y = y + cw_ref[k:k + 1, :] * shifted[k * CONV_BAND:(k + 1) * CONV_BAND, :]
        bands.append(y)
    return bands[0] if len(bands) == 1 else jnp.concatenate(bands, axis=0)


def _bf16_pieces(x, parts):
    pieces = []
    rest = x
    for _ in range(parts):
        piece = rest.astype(BF16)
        pieces.append(piece)
        rest = rest - piece.astype(F32)
    return pieces


def _select_dot(x, w, parts):
    pieces = _bf16_pieces(x, parts)
    return jnp.dot(jnp.concatenate(pieces, axis=1), jnp.concatenate([w] * parts, axis=0),
                   preferred_element_type=F32)


def _cumsum_rows(x, parts):
    n = x.shape[0]
    ri = lax.broadcasted_iota(jnp.int32, (n, n), 0)
    ci = lax.broadcasted_iota(jnp.int32, (n, n), 1)
    tril = jnp.where(ri >= ci, 1.0, 0.0).astype(BF16)
    pieces = _bf16_pieces(x, parts)
    return jnp.dot(jnp.concatenate([tril] * parts, axis=1), jnp.concatenate(pieces, axis=0),
                   preferred_element_type=F32)


ROW_TILES = D_MODEL // LANES


def _store_token_tiles(ref, x):
    rows = x.shape[0]
    for s in range(ROW_TILES):
        ref[pl.ds(s, rows, stride=ROW_TILES), :] = x[:, s * LANES:(s + 1) * LANES]


def _load_token_tiles(ref, rows):
    return jnp.concatenate([ref[pl.ds(s, rows, stride=ROW_TILES), :] for s in range(ROW_TILES)], axis=1)


def _token_tile(ref, row):
    return ref.at[pl.ds(pl.multiple_of(row * ROW_TILES, ROW_TILES), ROW_TILES), :]


def _rglru_kernel(x_ref, g_ref, cw_ref, cb_ref, wg_ref, ba_ref, bx_ref, lam_ref, o_ref, hist, hcar):
    s = pl.program_id(1)
    ts = x_ref.shape[0]
    first = s == 0

    @pl.when(first)
    def _():
        hcar[...] = jnp.zeros(hcar.shape, F32)

    xa = _causal_conv(x_ref[...], hist, cw_ref, cb_ref, first)
    xab = xa.astype(BF16)
    r_parts, i_parts = [], []
    for h in range(RNN_HEADS):
        pre = jnp.dot(xab[:, h * RNN_BLOCK:(h + 1) * RNN_BLOCK], wg_ref[h], preferred_element_type=F32)
        r_parts.append(pre[:, :RNN_BLOCK])
        i_parts.append(pre[:, RNN_BLOCK:])
    r_gate = jax.nn.sigmoid(jnp.concatenate(r_parts, axis=1) + ba_ref[...])
    i_gate = jax.nn.sigmoid(jnp.concatenate(i_parts, axis=1) + bx_ref[...])
    log_a = (-LRU_C) * r_gate * jax.nn.softplus(-lam_ref[...])
    a = jnp.exp(log_a)
    mult = jnp.sqrt(1.0 - jnp.exp(2.0 * log_a))
    u = (xa * i_gate) * mult

    ng = ts // SUBLANES
    a3 = a.reshape(ng, SUBLANES, D_MODEL)
    b3 = u.reshape(ng, SUBLANES, D_MODEL)
    row = lax.broadcasted_iota(jnp.int32, a3.shape, 1)
    d = 1
    while d < SUBLANES:
        valid = row >= d
        a_s = jnp.where(valid, pltpu.roll(a3, d, axis=1), 1.0)
        b_s = jnp.where(valid, pltpu.roll(b3, d, axis=1), 0.0)
        b3 = a3 * b_s + b3
        a3 = a3 * a_s
        d *= 2
    h_in = hcar[SUBLANES - 1:SUBLANES, :]
    groups = []
    for gi in range(ng):
        hg = b3[gi] + a3[gi] * h_in
        groups.append(hg)
        h_in = hg[SUBLANES - 1:SUBLANES, :]
    h = jnp.concatenate(groups, axis=0)
    hcar[...] = groups[-1]
    o_ref[...] = (h * jax.nn.gelu(g_ref[...].astype(F32))).astype(o_ref.dtype)


def _branch_a(proj, cw, cb, wg, ba, bx, lam, bsz, seq, ts=256):
    nst = seq // ts
    full = lambda shape: pl.BlockSpec(shape, lambda b, s: (0,) * len(shape))
    return pl.pallas_call(
        _rglru_kernel,
        grid=(bsz, nst),
        in_specs=[pl.BlockSpec((ts, D_MODEL), lambda b, s: (b * nst + s, COL_AX // D_MODEL)),
                  pl.BlockSpec((ts, D_MODEL), lambda b, s: (b * nst + s, COL_AGATE // D_MODEL)),
                  full((CONV_WIDTH, D_MODEL)), full((1, D_MODEL)),
                  full((RNN_HEADS, RNN_BLOCK, 2 * RNN_BLOCK)),
                  full((1, D_MODEL)), full((1, D_MODEL)), full((1, D_MODEL))],
        out_specs=pl.BlockSpec((ts, D_MODEL), lambda b, s: (b * nst + s, 0)),
        out_shape=jax.ShapeDtypeStruct((bsz * seq, D_MODEL), BF16),
        scratch_shapes=[pltpu.VMEM((BF16_ROWS, D_MODEL), BF16), pltpu.VMEM((SUBLANES, D_MODEL), F32)],
        compiler_params=_cparams(("parallel", "arbitrary")),
        name="rglru",
    )(proj, proj, cw, cb, wg, ba, bx, lam)


def _ssd_kernel(z_ref, xbc_ref, dtf_ref, cw_ref, cb_ref, dtb_ref, alog_ref, dskip_ref, nw_ref,
                o_ref, hist, state):
    c = pl.program_id(1)
    L = SSM_CHUNK
    first = c == 0

    @pl.when(first)
    def _():
        state[...] = jnp.zeros(state.shape, F32)

    conv = _causal_conv(xbc_ref[...], hist, cw_ref, cb_ref, first)
    act = conv * jax.nn.sigmoid(conv)
    xs = act[:, :D_MODEL]
    bm = act[:, D_MODEL:D_MODEL + SSM_GROUPS * SSM_STATE]
    cm = act[:, D_MODEL + SSM_GROUPS * SSM_STATE:]

    lane = lax.broadcasted_iota(jnp.int32, (L, LANES), 1)
    head_lane = lane < SSM_HEADS
    dt = jnp.where(head_lane, jax.nn.softplus(dtf_ref[...] + dtb_ref[...]), 0.0)
    a_dt = dt * (-jnp.exp(alog_ref[...]))
    ri = lax.broadcasted_iota(jnp.int32, (L, L), 0)
    ci = lax.broadcasted_iota(jnp.int32, (L, L), 1)
    causal = ri >= ci
    cs = _cumsum_rows(a_dt, 3)
    cs_t = cs.T
    tot = cs[L - 1:L, :]
    dstate = jnp.exp(tot - cs)
    exp_cs = jnp.exp(cs)

    er = lax.broadcasted_iota(jnp.int32, (LANES, D_MODEL), 0)
    ec = lax.broadcasted_iota(jnp.int32, (LANES, D_MODEL), 1)
    expand = jnp.where(ec // SSM_HEAD_DIM == er, 1.0, 0.0).astype(BF16)
    dt_e = _select_dot(dt, expand, 2)
    dtds_e = _select_dot(dt * dstate, expand, 2)
    tot_e = _select_dot(jnp.broadcast_to(jnp.exp(tot), (SUBLANES, LANES)), expand, 3)[0:1, :]
    xdt = xs * dt_e
    xdt_end = (xs * dtds_e).astype(BF16)

    lo_half = lax.broadcasted_iota(jnp.int32, (2 * L, LANES), 1) < SSM_HEAD_DIM
    heads_per_group = SSM_HEADS // SSM_GROUPS
    y_parts = []
    new_states = []
    for g in range(SSM_GROUPS):
        bg = bm[:, g * SSM_STATE:(g + 1) * SSM_STATE]
        cg = cm[:, g * SSM_STATE:(g + 1) * SSM_STATE]
        cb = lax.dot_general(cg.astype(BF16), bg.astype(BF16), (((1,), (1,)), ((), ())),
                             preferred_element_type=F32)
        lhs = []
        for e in range(heads_per_group):
            hd = g * heads_per_group + e
            colb = jnp.broadcast_to(cs[:, hd:hd + 1], (L, L))
            rowb = jnp.broadcast_to(cs_t[hd:hd + 1, :], (L, L))
            decay = jnp.exp(jnp.where(causal, colb - rowb, -jnp.inf))
            m = (cb * decay).astype(BF16)
            c_off = (cg * jnp.broadcast_to(exp_cs[:, hd:hd + 1], (L, L))).astype(BF16)
            lhs.append(jnp.concatenate([m, c_off], axis=1))
        for j in range(heads_per_group // 2):
            col = (g * heads_per_group + 2 * j) * SSM_HEAD_DIM
            rhs = jnp.concatenate([xdt[:, col:col + LANES], state[:, col:col + LANES]], axis=0).astype(BF16)
            zero = jnp.zeros_like(rhs)
            y_parts.append(jnp.dot(lhs[2 * j], jnp.where(lo_half, rhs, zero), preferred_element_type=F32)
                           + jnp.dot(lhs[2 * j + 1], jnp.where(lo_half, zero, rhs), preferred_element_type=F32))
        gw = heads_per_group * SSM_HEAD_DIM
        new_states.append(jnp.dot(bg.T.astype(BF16), xdt_end[:, g * gw:(g + 1) * gw],
                                  preferred_element_type=F32))
    y = jnp.concatenate(y_parts, axis=1)
    state[...] = state[...] * tot_e + jnp.concatenate(new_states, axis=1)

    y = y + xs * dskip_ref[...]
    z = z_ref[...].astype(F32)
    gy = y * (z * jax.nn.sigmoid(z))
    gw = D_MODEL // SSM_GROUPS
    outs = []
    for g in range(SSM_GROUPS):
        gg = gy[:, g * gw:(g + 1) * gw]
        ms = jnp.mean(gg * gg, axis=-1, keepdims=True)
        outs.append(gg * lax.rsqrt(ms + RMS_EPS))
    o_ref[...] = (jnp.concatenate(outs, axis=1) * nw_ref[...]).astype(o_ref.dtype)


def _branch_b(proj, small, cw, cb, dtb, alog, dskip_e, nw, bsz, seq):
    L = SSM_CHUNK
    nc = seq // L
    full = lambda shape: pl.BlockSpec(shape, lambda b, c: (0,) * len(shape))
    return pl.pallas_call(
        _ssd_kernel,
        grid=(bsz, nc),
        in_specs=[pl.BlockSpec((L, D_MODEL), lambda b, c: (b * nc + c, COL_BZ // D_MODEL)),
                  pl.BlockSpec((L, SSM_CONV_CH), lambda b, c: (b * nc + c, COL_XBC // SSM_CONV_CH)),
                  pl.BlockSpec((L, N_SMALL), lambda b, c: (b * nc + c, 0)),
                  full((CONV_WIDTH, SSM_CONV_CH)), full((1, SSM_CONV_CH)),
                  full((1, N_SMALL)), full((1, N_SMALL)), full((1, D_MODEL)), full((1, D_MODEL))],
        out_specs=pl.BlockSpec((L, D_MODEL), lambda b, c: (b * nc + c, 0)),
        out_shape=jax.ShapeDtypeStruct((bsz * seq, D_MODEL), BF16),
        scratch_shapes=[pltpu.VMEM((BF16_ROWS, SSM_CONV_CH), BF16), pltpu.VMEM((SSM_STATE, D_MODEL), F32)],
        compiler_params=_cparams(("parallel", "arbitrary")),
        name="ssd",
    )(proj, proj, small, cw, cb, dtb, alog, dskip_e, nw)


CUM_BLOCK = 256
LOG2E = 1.4426950408889634
BIAS_PARTS = 3


def _fox_cum_kernel(dtf_ref, fb_ref, o_ref):
    seq = dtf_ref.shape[0]
    lane = lax.broadcasted_iota(jnp.int32, (CUM_BLOCK, LANES), 1)
    live = (lane >= SSM_HEADS) & (lane < SSM_HEADS + ATTN_HEADS)
    carry = jnp.zeros((1, LANES), F32)
    for i in range(seq // CUM_BLOCK):
        rows = slice(i * CUM_BLOCK, (i + 1) * CUM_BLOCK)
        logf = jnp.where(live, jax.nn.log_sigmoid(dtf_ref[rows, :] + fb_ref[...]), 0.0)
        cb = _cumsum_rows(logf, 3) + carry
        carry = cb[CUM_BLOCK - 1:CUM_BLOCK, :]
        o_ref[rows, :] = cb


def _fox_cum(small, fb, bsz, seq):
    return pl.pallas_call(
        _fox_cum_kernel,
        grid=(bsz,),
        in_specs=[pl.BlockSpec((seq, N_SMALL), lambda b: (b, 0)),
                  pl.BlockSpec((1, N_SMALL), lambda b: (0, 0))],
        out_specs=pl.BlockSpec((seq, N_SMALL), lambda b: (b, 0)),
        out_shape=jax.ShapeDtypeStruct((bsz * seq, N_SMALL), F32),
        compiler_params=_cparams(("parallel",)),
        name="fox_cum",
    )(small, fb)


def _split3(x):
    hi = x.astype(BF16)
    r1 = x - hi.astype(F32)
    mid = r1.astype(BF16)
    lo = (r1 - mid.astype(F32)).astype(BF16)
    return hi, mid, lo


def _fox_attn_kernel(q_ref, k_ref, v_ref, cum_ref, o_ref, k0_s, k1_s, v0_s, v1_s, aug_s, *, tq):
    seq = q_ref.shape[0]
    hd = ATTN_HEAD_DIM
    pair = pl.program_id(1)
    lane = lax.broadcasted_iota(jnp.int32, (seq, LANES), 1)
    lo_half = lane < hd

    sr = lax.broadcasted_iota(jnp.int32, (LANES, LANES), 0)
    sc = lax.broadcasted_iota(jnp.int32, (LANES, LANES), 1)
    src0 = SSM_HEADS + 2 * pair
    parts = _split3(cum_ref[...] * (-LOG2E))
    aug = jnp.zeros((seq, LANES), F32)
    for j, part in enumerate(parts):
        sel = jnp.where(((sr == src0) & (sc == hd + j)) | ((sr == src0 + 1) & (sc == j)), 1.0, 0.0).astype(BF16)
        aug = aug + jnp.dot(part, sel, preferred_element_type=F32)
    aug_s[...] = aug
    half_lane = lane % hd
    k_ones = (half_lane >= BIAS_PARTS) & (half_lane < 2 * BIAS_PARTS)
    aug_k = jnp.where(k_ones, 1.0, aug).astype(BF16)
    k = k_ref[...]
    v = v_ref[...]
    zero = jnp.zeros_like(v)
    k0_s[...] = jnp.where(lo_half, k, aug_k)
    k1_s[...] = jnp.where(lo_half, aug_k, k)
    v0_s[...] = jnp.where(lo_half, v, zero)
    v1_s[...] = jnp.where(lo_half, zero, v)

    lane_q = lax.broadcasted_iota(jnp.int32, (tq, LANES), 1)
    lo_half_q = lane_q < hd
    half_q = lane_q % hd
    q_ones = half_q < BIAS_PARTS
    q_const = (half_q >= BIAS_PARTS) & (half_q < 2 * BIAS_PARTS)
    tri = lax.broadcasted_iota(jnp.int32, (tq, tq), 1) <= lax.broadcasted_iota(jnp.int32, (tq, tq), 0)
    nt = (((1,), (1,)), ((), ()))

    for qi in reversed(range(seq // tq)):
        q0 = qi * tq
        q = q_ref[q0:q0 + tq, :]
        c_row = jnp.broadcast_to(pltpu.roll(-aug_s[q0:q0 + 1, :], BIAS_PARTS, axis=1), (tq, LANES))
        q_aug = jnp.where(q_ones, 1.0, jnp.where(q_const, c_row, 0.0)).astype(BF16)
        q_heads = (jnp.where(lo_half_q, q, q_aug), jnp.where(lo_half_q, q_aug, q))
        out = None
        for hh, (k_s, v_s) in enumerate(((k0_s, v0_s), (k1_s, v1_s))):
            s_diag = lax.dot_general(q_heads[hh], k_s[q0:q0 + tq, :], nt, preferred_element_type=F32)
            s_diag = jnp.where(tri, s_diag, -jnp.inf)
            m = jnp.max(s_diag, axis=-1, keepdims=True)
            if qi > 0:
                s_off = lax.dot_general(q_heads[hh], k_s[0:q0, :], nt, preferred_element_type=F32)
                m = jnp.maximum(m, jnp.max(s_off, axis=-1, keepdims=True))
            p_diag = jnp.exp2(s_diag - m)
            l = jnp.sum(p_diag, axis=-1, keepdims=True)
            acc = jnp.dot(p_diag.astype(BF16), v_s[q0:q0 + tq, :], preferred_element_type=F32)
            if qi > 0:
                p_off = jnp.exp2(s_off - m)
                l = l + jnp.sum(p_off, axis=-1, keepdims=True)
                acc = acc + jnp.dot(p_off.astype(BF16), v_s[0:q0, :], preferred_element_type=F32)
            acc = acc * (1.0 / l)
            out = acc if out is None else out + acc
        o_ref[q0:q0 + tq, :] = out.astype(o_ref.dtype)


def _branch_c(proj, cum, bsz, seq, tq=512):
    npair = ATTN_HEADS // 2
    kv_scratch = pltpu.VMEM((seq, LANES), BF16)
    return pl.pallas_call(
        functools.partial(_fox_attn_kernel, tq=tq),
        grid=(bsz, npair),
        in_specs=[pl.BlockSpec((seq, LANES), lambda b, p: (b, COL_QKV // LANES + p)),
                  pl.BlockSpec((seq, LANES), lambda b, p: (b, COL_QKV // LANES + npair + p)),
                  pl.BlockSpec((seq, LANES), lambda b, p: (b, COL_QKV // LANES + 2 * npair + p)),
                  pl.BlockSpec((seq, N_SMALL), lambda b, p: (b, 0))],
        out_specs=pl.BlockSpec((seq, LANES), lambda b, p: (b, p)),
        out_shape=jax.ShapeDtypeStruct((bsz * seq, D_MODEL), BF16),
        scratch_shapes=[kv_scratch, kv_scratch, kv_scratch, kv_scratch, pltpu.VMEM((seq, LANES), F32)],
        compiler_params=_cparams(("parallel", "parallel")),
        name="fox_attn",
    )(proj, proj, proj, cum)


def _layer_norm(x, g, b):
    mu = jnp.mean(x, axis=-1, keepdims=True)
    xc = x - mu
    var = jnp.mean(xc * xc, axis=-1, keepdims=True)
    return xc * lax.rsqrt(var + LN_EPS) * g + b


def _top2_sum(a, b, c, d):
    hi1, lo1 = jnp.maximum(a, b), jnp.minimum(a, b)
    hi2, lo2 = jnp.maximum(c, d), jnp.minimum(c, d)
    return jnp.maximum(hi1, hi2) + jnp.maximum(jnp.minimum(hi1, hi2), jnp.maximum(lo1, lo2))


def _route_rows(logits_t):
    rows = [logits_t[e:e + 1, :] for e in range(N_EXPERTS)]
    mx = functools.reduce(jnp.maximum, rows)
    ex = [jnp.exp(r - mx) for r in rows]
    den = functools.reduce(jnp.add, ex)
    probs = [e / den for e in ex]
    ngroups = N_EXPERTS // EXPERTS_PER_GROUP
    scores = [_top2_sum(*probs[EXPERTS_PER_GROUP * g:EXPERTS_PER_GROUP * (g + 1)]) for g in range(ngroups)]
    best_g = jnp.zeros_like(mx, dtype=jnp.int32)
    best_s = scores[0]
    for g in range(1, ngroups):
        better = scores[g] > best_s
        best_g = jnp.where(better, g, best_g)
        best_s = jnp.where(better, scores[g], best_s)
    masked = [jnp.where(best_g == e // EXPERTS_PER_GROUP, probs[e], -1.0) for e in range(N_EXPERTS)]
    v1, i1 = masked[0], jnp.zeros_like(best_g)
    for e in range(1, N_EXPERTS):
        better = masked[e] > v1
        i1 = jnp.where(better, e, i1)
        v1 = jnp.where(better, masked[e], v1)
    v2, i2 = jnp.full_like(v1, -2.0), jnp.zeros_like(best_g)
    for e in range(N_EXPERTS):
        better = (masked[e] > v2) & (i1 != e)
        i2 = jnp.where(better, e, i2)
        v2 = jnp.where(better, masked[e], v2)
    tot = v1 + v2
    return i1, i2, v1 / tot, v2 / tot


def _merge_kernel(ha_ref, hb_ref, hc_ref, gate_ref, x_ref, wa_ref, wb_ref, wc_ref, wo_ref, gb_ref,
                  lg_ref, lb_ref, rwh_ref, rwl_ref, rb_ref,
                  x1t_ref, rcols_ref, ids_ref):
    tm = x_ref.shape[0]
    ya = jnp.dot(ha_ref[...], wa_ref[...], preferred_element_type=F32)
    yb = jnp.dot(hb_ref[...], wb_ref[...], preferred_element_type=F32)
    yc = jnp.dot(hc_ref[...], wc_ref[...], preferred_element_type=F32)
    g = jax.nn.sigmoid(gate_ref[...].astype(F32) + gb_ref[...])
    mixed_in = (g[:, :D_MODEL] * ya + g[:, D_MODEL:2 * D_MODEL] * yb + g[:, 2 * D_MODEL:] * yc).astype(BF16)
    mixed = jnp.dot(mixed_in, wo_ref[...], preferred_element_type=F32)
    x1 = _layer_norm(DEEPNORM_ALPHA * x_ref[...] + mixed, lg_ref[...], lb_ref[...])
    _store_token_tiles(x1t_ref, x1)
    x1h = x1.astype(BF16)
    x1l = (x1 - x1h.astype(F32)).astype(BF16)
    nt = (((1,), (1,)), ((), ()))
    logits_t = (lax.dot_general(rwh_ref[...], x1h, nt, preferred_element_type=F32)
                + lax.dot_general(rwl_ref[...], x1h, nt, preferred_element_type=F32)
                + lax.dot_general(rwh_ref[...], x1l, nt, preferred_element_type=F32)
                + rb_ref[...])
    i1, i2, w1, w2 = _route_rows(logits_t)
    sub = lax.broadcasted_iota(jnp.int32, (SUBLANES, tm), 0)
    ids_ref[...] = jnp.where(sub == 0, i1, jnp.where(sub == 1, i2, 0))
    wrows = jnp.where(sub == 0, w1, jnp.where(sub == 1, w2, 0.0))
    wrows = jnp.concatenate([wrows, jnp.zeros((LANES - SUBLANES, tm), F32)], axis=0)
    rcols_ref[...] = wrows.T


def _merge(ha, hb, hc, proj, x, wa, wb, wc, wo, gb, lg, lb, rwh, rwl, rb, tm=512):
    t = x.shape[0]
    full = lambda shape: pl.BlockSpec(shape, lambda i: (0,) * len(shape))
    row = lambda w: pl.BlockSpec((tm, w), lambda i: (i, 0))
    return pl.pallas_call(
        _merge_kernel,
        grid=(t // tm,),
        in_specs=[row(D_MODEL), row(D_MODEL), row(D_MODEL),
                  pl.BlockSpec((tm, 3 * D_MODEL), lambda i: (i, COL_GATE // (3 * D_MODEL))),
                  row(D_MODEL),
                  full((D_MODEL, D_MODEL)), full((D_MODEL, D_MODEL)), full((D_MODEL, D_MODEL)),
                  full((D_MODEL, D_MODEL)), full((1, 3 * D_MODEL)),
                  full((1, D_MODEL)), full((1, D_MODEL)),
                  full((N_EXPERTS, D_MODEL)), full((N_EXPERTS, D_MODEL)), full((N_EXPERTS, 1))],
        out_specs=[pl.BlockSpec((tm * ROW_TILES, LANES), lambda i: (i, 0)), row(LANES),
                   pl.BlockSpec((SUBLANES, tm), lambda i: (0, i))],
        out_shape=[jax.ShapeDtypeStruct((t * ROW_TILES, LANES), F32), jax.ShapeDtypeStruct((t, LANES), F32),
                   jax.ShapeDtypeStruct((SUBLANES, t), jnp.int32)],
        compiler_params=_cparams(("parallel",)),
        name="merge",
    )(ha, hb, hc, proj, x, wa, wb, wc, wo, gb, lg, lb, rwh, rwl, rb)


TOP_K = 2
MOE_TILE = 512
MOE_TILE_SHIFT = 9
PLAN_BLOCK = 256


def _moe_rows(t):
    return TOP_K * t + (N_EXPERTS + 1) * MOE_TILE


def _plan_kernel(ids_ref, pos_ref, meta_ref, cnt_s):
    t = ids_ref.shape[1]
    nblk = t // PLAN_BLOCK
    sub_e = lax.broadcasted_iota(jnp.int32, (N_EXPERTS, PLAN_BLOCK), 0)
    ur = lax.broadcasted_iota(jnp.int32, (PLAN_BLOCK, PLAN_BLOCK), 0)
    uc = lax.broadcasted_iota(jnp.int32, (PLAN_BLOCK, PLAN_BLOCK), 1)
    before = jnp.where(ur < uc, 1.0, 0.0).astype(BF16)

    def one_hots(c):
        cols = pl.ds(pl.multiple_of(c * PLAN_BLOCK, PLAN_BLOCK), PLAN_BLOCK)
        ids = ids_ref[:, cols]
        oh0 = jnp.where(ids[0:1, :] == sub_e, 1.0, 0.0)
        oh1 = jnp.where(ids[1:2, :] == sub_e, 1.0, 0.0)
        return cols, oh0, oh1

    def count_block(c, carry):
        cols, oh0, oh1 = one_hots(c)
        oh = oh0 + oh1
        cnt_s[:, cols] = jnp.dot(oh.astype(BF16), before, preferred_element_type=F32) + carry
        return carry + jnp.sum(oh, axis=1, keepdims=True)

    counts = lax.fori_loop(0, nblk, count_block, jnp.zeros((N_EXPERTS, 1), F32))
    padded = ((counts.astype(jnp.int32) + (MOE_TILE - 1)) >> MOE_TILE_SHIFT) << MOE_TILE_SHIFT
    padded = jnp.broadcast_to(padded, (N_EXPERTS, LANES))
    sub = lax.broadcasted_iota(jnp.int32, (N_EXPERTS, LANES), 0)
    lane = lax.broadcasted_iota(jnp.int32, (N_EXPERTS, LANES), 1)
    start = jnp.zeros((N_EXPERTS, LANES), jnp.int32)
    run = jnp.zeros((1, LANES), jnp.int32)
    for e in range(N_EXPERTS):
        start = jnp.where(sub == e, run, start)
        run = run + padded[e:e + 1, :]
    ended = jnp.where(start + padded <= lane * MOE_TILE, 1, 0)
    tile_expert = jnp.minimum(jnp.sum(ended, axis=0, keepdims=True), N_EXPERTS - 1)
    first_pad = start + jnp.broadcast_to(counts.astype(jnp.int32), (N_EXPERTS, LANES))
    first_pad = jnp.sum(jnp.where(sub == lane, first_pad, 0), axis=0, keepdims=True)
    sub8 = lax.broadcasted_iota(jnp.int32, (SUBLANES, LANES), 0)
    meta_ref[...] = jnp.where(sub8 == 0, tile_expert,
                              jnp.where(sub8 == 1, run >> MOE_TILE_SHIFT, jnp.where(sub8 == 2, first_pad, 0)))

    start_f = start[:, 0:1].astype(F32)
    sub8b = lax.broadcasted_iota(jnp.int32, (SUBLANES, PLAN_BLOCK), 0)

    def place_block(c, _):
        cols, oh0, oh1 = one_hots(c)
        base = cnt_s[:, cols] + start_f
        p0 = jnp.sum(oh0 * base, axis=0, keepdims=True).astype(jnp.int32)
        p1 = jnp.sum(oh1 * base, axis=0, keepdims=True).astype(jnp.int32)
        pos_ref[:, cols] = jnp.where(sub8b == 0, p0, jnp.where(sub8b == 1, p1, 0))
        return 0

    lax.fori_loop(0, nblk, place_block, 0)


def _plan(ids):
    t = ids.shape[1]
    return pl.pallas_call(
        _plan_kernel,
        out_shape=[jax.ShapeDtypeStruct((SUBLANES, t), jnp.int32),
                   jax.ShapeDtypeStruct((SUBLANES, LANES), jnp.int32)],
        scratch_shapes=[pltpu.VMEM((N_EXPERTS, t), F32)],
        compiler_params=pltpu.CompilerParams(vmem_limit_bytes=VMEM_LIMIT),
        name="moe_plan",
    )(ids)


def _dispatch_kernel(pos_ref, pad_ref, nt_ref, x_ref, xs_hbm, zeros, sems):
    tm = x_ref.shape[0] // ROW_TILES
    t = pos_ref.shape[0] // TOP_K
    base = pl.program_id(0) * tm
    first = pl.program_id(0) == 0
    tile_rows = MOE_TILE * ROW_TILES
    unused = [(j, pltpu.make_async_copy(zeros, xs_hbm.at[pl.ds(j * tile_rows, tile_rows), :], sems.at[1]))
              for j in range(xs_hbm.shape[0] // tile_rows)]

    @pl.when(first)
    def _():
        zeros[...] = jnp.zeros(zeros.shape, F32)
        fills = [pltpu.make_async_copy(zeros, xs_hbm.at[pl.ds(pad_ref[e] * ROW_TILES, tile_rows), :], sems.at[0])
                 for e in range(N_EXPERTS)]
        for f in fills:
            f.start()
        for f in fills:
            f.wait()
        for j, fill in unused:
            pl.when(j >= nt_ref[0])(fill.start)

    def body(j, _):
        for k in range(TOP_K):
            pltpu.make_async_copy(_token_tile(x_ref, j), _token_tile(xs_hbm, pos_ref[k * t + base + j]),
                                  sems.at[0]).start(priority=k)
        return 0

    lax.fori_loop(0, tm, body, 0, unroll=8)
    for k in range(TOP_K):
        pltpu.make_async_copy(x_ref, xs_hbm.at[pl.ds(0, tm * ROW_TILES), :], sems.at[0]).wait()

    @pl.when(first)
    def _():
        for j, fill in unused:
            pl.when(j >= nt_ref[0])(fill.wait)


def _dispatch(pos, first_pad, ntiles, x1t, tm=512):
    t = x1t.shape[0] // ROW_TILES
    return pl.pallas_call(
        _dispatch_kernel,
        grid_spec=pltpu.PrefetchScalarGridSpec(
            num_scalar_prefetch=3, grid=(t // tm,),
            in_specs=[pl.BlockSpec((tm * ROW_TILES, LANES), lambda i, pos, pad, nt: (i, 0))],
            out_specs=pl.BlockSpec(memory_space=pl.ANY),
            scratch_shapes=[pltpu.VMEM((MOE_TILE * ROW_TILES, LANES), F32), pltpu.SemaphoreType.DMA((2,))]),
        out_shape=jax.ShapeDtypeStruct((_moe_rows(t) * ROW_TILES, LANES), F32),
        compiler_params=_cparams(("arbitrary",)),
        name="moe_dispatch",
    )(pos, first_pad, ntiles, x1t)


def _ffn_kernel(te_ref, nt_ref, xs_ref, w1_ref, w3_ref, w2_ref, ys_ref, w1b, w3b, w2b):
    j = pl.program_id(0)
    in_use = j < nt_ref[0]

    @pl.when(in_use & ((j == 0) | (te_ref[j] != te_ref[jnp.maximum(j - 1, 0)])))
    def _():
        w1b[...] = w1_ref[0].astype(BF16)
        w3b[...] = w3_ref[0].astype(BF16)
        w2b[...] = w2_ref[0].astype(BF16)

    @pl.when(in_use)
    def _():
        xb = _load_token_tiles(xs_ref, MOE_TILE).astype(BF16)
        h1 = jnp.dot(xb, w1b[...], preferred_element_type=F32)
        h3 = jnp.dot(xb, w3b[...], preferred_element_type=F32)
        h = (h1 * jax.nn.sigmoid(h1) * h3).astype(BF16)
        _store_token_tiles(ys_ref, jnp.dot(h, w2b[...], preferred_element_type=F32))

    @pl.when(jnp.logical_not(in_use))
    def _():
        ys_ref[...] = jnp.zeros(ys_ref.shape, F32)


def _ffn(tile_expert, ntiles, xs, w1, w3, w2):
    ntile = xs.shape[0] // (MOE_TILE * ROW_TILES)
    tile = lambda j, te, nt: (jnp.minimum(j, nt[0] - 1), 0)
    expert = lambda j, te, nt: (te[jnp.minimum(j, nt[0] - 1)], 0, 0)
    return pl.pallas_call(
        _ffn_kernel,
        grid_spec=pltpu.PrefetchScalarGridSpec(
            num_scalar_prefetch=2, grid=(ntile,),
            in_specs=[pl.BlockSpec((MOE_TILE * ROW_TILES, LANES), tile),
                      pl.BlockSpec((1, D_MODEL, D_EXPERT), expert),
                      pl.BlockSpec((1, D_MODEL, D_EXPERT), expert),
                      pl.BlockSpec((1, D_EXPERT, D_MODEL), expert)],
            out_specs=pl.BlockSpec((MOE_TILE * ROW_TILES, LANES), lambda j, te, nt: (j, 0)),
            scratch_shapes=[pltpu.VMEM((D_MODEL, D_EXPERT), BF16), pltpu.VMEM((D_MODEL, D_EXPERT), BF16),
                            pltpu.VMEM((D_EXPERT, D_MODEL), BF16)]),
        out_shape=jax.ShapeDtypeStruct(xs.shape, F32),
        compiler_params=_cparams(("arbitrary",)),
        name="moe_ffn",
    )(tile_expert, ntiles, xs, w1, w3, w2)


def _combine_kernel(pos_ref, ys_hbm, x1t_ref, rc_ref, lg_ref, lb_ref, o_ref, ob_ref, gath, sems):
    tm = o_ref.shape[0]
    t = pos_ref.shape[0] // TOP_K
    i = pl.program_id(0)
    ntile = pl.num_programs(0)

    def issue(tile, slot):
        def body(j, _):
            for k in range(TOP_K):
                pltpu.make_async_copy(_token_tile(ys_hbm, pos_ref[k * t + tile * tm + j]),
                                      _token_tile(gath.at[slot, k], j), sems.at[slot]).start(priority=k)
            return 0
        lax.fori_loop(0, tm, body, 0, unroll=8)

    @pl.when(i == 0)
    def _():
        issue(0, 0)

    @pl.when(i + 1 < ntile)
    def _():
        issue(i + 1, (i + 1) % 2)

    slot = i % 2
    for k in range(TOP_K):
        pltpu.make_async_copy(ys_hbm.at[pl.ds(0, tm * ROW_TILES), :], gath.at[slot, k], sems.at[slot]).wait()
    rc = rc_ref[...]
    y = (rc[:, 0:1] * _load_token_tiles(gath.at[slot, 0], tm)
         + rc[:, 1:2] * _load_token_tiles(gath.at[slot, 1], tm))
    x2 = _layer_norm(DEEPNORM_ALPHA * _load_token_tiles(x1t_ref, tm) + y, lg_ref[...], lb_ref[...])
    o_ref[...] = x2
    ob_ref[...] = x2.astype(BF16)


def _combine(pos, ys, x1t, rcols, lg, lb, tm=256):
    t = x1t.shape[0] // ROW_TILES
    row = lambda w: pl.BlockSpec((tm, w), lambda i, pos: (i, 0))
    full = lambda shape: pl.BlockSpec(shape, lambda i, pos: (0,) * len(shape))
    return pl.pallas_call(
        _combine_kernel,
        grid_spec=pltpu.PrefetchScalarGridSpec(
            num_scalar_prefetch=1, grid=(t // tm,),
            in_specs=[pl.BlockSpec(memory_space=pl.ANY),
                      pl.BlockSpec((tm * ROW_TILES, LANES), lambda i, pos: (i, 0)), row(LANES),
                      full((1, D_MODEL)), full((1, D_MODEL))],
            out_specs=[row(D_MODEL), row(D_MODEL)],
            scratch_shapes=[pltpu.VMEM((2, TOP_K, tm * ROW_TILES, LANES), F32), pltpu.SemaphoreType.DMA((2,))]),
        out_shape=[jax.ShapeDtypeStruct((t, D_MODEL), F32), jax.ShapeDtypeStruct((t, D_MODEL), BF16)],
        compiler_params=_cparams(("arbitrary",)),
        name="moe_combine",
    )(pos, ys, x1t, rcols, lg, lb)


def _moe(x1t, rcols, ids, w1, w3, w2, lg, lb):
    pos8, meta = _plan(ids)
    pos = pos8[:TOP_K].reshape(-1)
    nt_max = _moe_rows(x1t.shape[0] // ROW_TILES) // MOE_TILE
    ntiles = meta[1, :1]
    xs = _dispatch(pos, meta[2, :N_EXPERTS], ntiles, x1t)
    ys = _ffn(meta[0, :nt_max], ntiles, xs, w1, w3, w2)
    return _combine(pos, ys, x1t, rcols, lg, lb)


def _split_hi_lo(w):
    hi = w.astype(BF16)
    return hi, (w - hi.astype(F32)).astype(BF16)


IN_AX, IN_AGATE, IN_BZ, IN_XBC = 0, D_MODEL, 2 * D_MODEL, 3 * D_MODEL
IN_DT = IN_XBC + SSM_CONV_CH
IN_QKV = IN_DT + SSM_HEADS
IN_F = IN_QKV + 3 * D_MODEL
IN_GATE = IN_F + ATTN_HEADS
P_IN = IN_GATE + 3 * D_MODEL
W_PREP_ROWS = 128


def _cols_from(w_ref, start, width):
    off = start % LANES
    if off == 0:
        return w_ref[0, :, start:start + width]
    ext = w_ref[0, :, start - off:start - off + width + LANES]
    return pltpu.roll(ext, width + LANES - off, axis=1)[:, :width]


def _w_prep_kernel(w_ref, tail_ref, main_ref, small_ref):
    def put(col, val):
        main_ref[0, :, col:col + val.shape[1]] = val.astype(BF16)

    put(COL_QKV, _cols_from(w_ref, IN_QKV, D_MODEL) * (ATTN_HEAD_DIM ** -0.5 * LOG2E))
    put(COL_QKV + D_MODEL, _cols_from(w_ref, IN_QKV + D_MODEL, 2 * D_MODEL))
    put(COL_XBC, _cols_from(w_ref, IN_XBC, SSM_CONV_CH))
    put(COL_AX, _cols_from(w_ref, IN_AX, D_MODEL))
    put(COL_AGATE, _cols_from(w_ref, IN_AGATE, D_MODEL))
    put(COL_BZ, _cols_from(w_ref, IN_BZ, D_MODEL))
    gate_w = 3 * D_MODEL
    off = IN_GATE % LANES
    put(COL_GATE, _cols_from(w_ref, IN_GATE, gate_w - LANES))
    last = w_ref[0, :, IN_GATE - off + gate_w - LANES:IN_GATE - off + gate_w]
    lane = lax.broadcasted_iota(jnp.int32, last.shape, 1)
    put(COL_GATE + gate_w - LANES,
        jnp.where(lane < LANES - off, pltpu.roll(last, LANES - off, axis=1),
                  pltpu.roll(tail_ref[0], LANES - off, axis=1)))
    dt_tile = w_ref[0, :, IN_DT:IN_DT + LANES]
    f_tile = w_ref[0, :, IN_F - SSM_HEADS:IN_F - SSM_HEADS + LANES]
    small_ref[0] = jnp.where(lane < SSM_HEADS, dt_tile,
                             jnp.where(lane < SSM_HEADS + ATTN_HEADS, f_tile, 0.0)).astype(BF16)


def _w_prep(w_in):
    depth, k, n = w_in.shape
    assert n == P_IN and IN_DT % LANES == 0 and (IN_F - SSM_HEADS) % LANES == 0
    return pl.pallas_call(
        _w_prep_kernel,
        grid=(depth, k // W_PREP_ROWS),
        in_specs=[pl.BlockSpec((1, W_PREP_ROWS, n), lambda l, r: (l, r, 0)),
                  pl.BlockSpec((1, W_PREP_ROWS, LANES), lambda l, r: (l, r, n // LANES))],
        out_specs=[pl.BlockSpec((1, W_PREP_ROWS, N_MAIN), lambda l, r: (l, r, 0)),
                   pl.BlockSpec((1, W_PREP_ROWS, N_SMALL), lambda l, r: (l, r, 0))],
        out_shape=[jax.ShapeDtypeStruct((depth, k, N_MAIN), BF16), jax.ShapeDtypeStruct((depth, k, N_SMALL), BF16)],
        compiler_params=_cparams(("parallel", "parallel")),
        name="w_prep",
    )(w_in, w_in)


def _prepare(w_in, gate_b, conv_a_w, conv_a_b, lru_wa, lru_ba, lru_wx, lru_bx, lru_lambda,
             conv_b_w, conv_b_b, dt_bias, a_log, d_skip, ssm_norm_w, forget_b,
             w_branch_a, w_branch_b, w_branch_c, w_out, ln1_g, ln1_b,
             router_w, router_b, w1, w3, w2, ln2_g, ln2_b):
    w_main, w_small = _w_prep(w_in)
    w_gates = jnp.concatenate([lru_wa, lru_wx], axis=-1).astype(BF16)
    pad_heads = lambda v, off: jnp.pad(v, ((0, 0), (off, N_SMALL - off - v.shape[1])))[:, None, :]
    dtb_p = pad_heads(dt_bias, 0)
    alog_p = pad_heads(a_log, 0)
    fb_p = pad_heads(forget_b, SSM_HEADS)
    dskip_e = jnp.repeat(d_skip, SSM_HEAD_DIM, axis=-1)[:, None, :]
    row = lambda v: v[:, None, :]
    wa_b, wb_b, wc_b, wo_b = (w.astype(BF16) for w in (w_branch_a, w_branch_b, w_branch_c, w_out))
    rwh, rwl = _split_hi_lo(router_w.T)
    return dict(
        w_main=w_main, w_small=w_small, conv_a_w=conv_a_w, conv_a_b=row(conv_a_b), w_gates=w_gates,
        lru_ba=row(lru_ba), lru_bx=row(lru_bx), lru_lambda=row(lru_lambda),
        conv_b_w=conv_b_w, conv_b_b=row(conv_b_b), dtb=dtb_p, alog=alog_p, dskip=dskip_e,
        ssm_norm_w=row(ssm_norm_w), fb=fb_p, wa=wa_b, wb=wb_b, wc=wc_b, wo=wo_b, gate_b=row(gate_b),
        ln1_g=row(ln1_g), ln1_b=row(ln1_b), rwh=rwh, rwl=rwl, rb=router_b[:, None],
        w1=w1, w3=w3, w2=w2, ln2_g=row(ln2_g), ln2_b=row(ln2_b))


def _layer(l, xf, xb, p, bsz, seq):
    proj = _matmul(xb, p['w_main'][l], BF16, 1024, 1024, "in_proj")
    small = _matmul(xb, p['w_small'][l], F32, 1024, N_SMALL, "in_proj_small")
    ha = _branch_a(proj, p['conv_a_w'][l], p['conv_a_b'][l], p['w_gates'][l], p['lru_ba'][l], p['lru_bx'][l],
                   p['lru_lambda'][l], bsz, seq)
    hb = _branch_b(proj, small, p['conv_b_w'][l], p['conv_b_b'][l], p['dtb'][l], p['alog'][l], p['dskip'][l],
                   p['ssm_norm_w'][l], bsz, seq)
    cum = _fox_cum(small, p['fb'][l], bsz, seq)
    hc = _branch_c(proj, cum, bsz, seq)
    x1, rcols, ids = _merge(ha, hb, hc, proj, xf, p['wa'][l], p['wb'][l], p['wc'][l], p['wo'][l],
                            p['gate_b'][l], p['ln1_g'][l], p['ln1_b'][l], p['rwh'], p['rwl'], p['rb'])
    x2, x2b = _moe(x1, rcols, ids, p['w1'][l], p['w3'][l], p['w2'][l], p['ln2_g'][l], p['ln2_b'][l])
    return dict(proj=proj, small=small, ha=ha, hb=hb, cum=cum, hc=hc, x1=x1, rcols=rcols, ids=ids,
                x2=x2, x2b=x2b)


def kernel(x, w_in, gate_b, conv_a_w, conv_a_b, lru_wa, lru_ba, lru_wx, lru_bx, lru_lambda,
           conv_b_w, conv_b_b, dt_bias, a_log, d_skip, ssm_norm_w, forget_b,
           w_branch_a, w_branch_b, w_branch_c, w_out, ln1_g, ln1_b,
           router_w, router_b, w1, w3, w2, ln2_g, ln2_b):
    bsz, seq, d = x.shape
    p = _prepare(w_in, gate_b, conv_a_w, conv_a_b, lru_wa, lru_ba, lru_wx, lru_bx, lru_lambda,
                 conv_b_w, conv_b_b, dt_bias, a_log, d_skip, ssm_norm_w, forget_b,
                 w_branch_a, w_branch_b, w_branch_c, w_out, ln1_g, ln1_b,
                 router_w, router_b, w1, w3, w2, ln2_g, ln2_b)
    xf = x.reshape(bsz * seq, d)
    xb = xf.astype(BF16)
    for l in range(w_in.shape[0]):
        stages = _layer(l, xf, xb, p, bsz, seq)
        xf, xb = stages['x2'], stages['x2b']
    return xf.reshape(bsz, seq, d)
```

```python
import functools

import jax
import jax.numpy as jnp
from jax import lax
from jax.experimental import pallas as pl
from jax.experimental.pallas import tpu as pltpu

F32 = jnp.float32
BF16 = jnp.bfloat16

D_MODEL = 1024
DEPTH = 4
RNN_HEADS = 8
RNN_BLOCK = 128
CONV_WIDTH = 4
LRU_C = 8.0
SSM_HEADS = 16
SSM_HEAD_DIM = 64
SSM_GROUPS = 4
SSM_STATE = 128
SSM_CHUNK = 128
SSM_CONV_CH = 2048
ATTN_HEADS = 16
ATTN_HEAD_DIM = 64
N_EXPERTS = 16
EXPERTS_PER_GROUP = 4
D_EXPERT = 512
LN_EPS = 1e-5
RMS_EPS = 1e-6
DEEPNORM_ALPHA = (2 * DEPTH) ** 0.25

LANES = 128
SUBLANES = 8
VMEM_LIMIT = 48 * 1024 * 1024

COL_QKV = 0
COL_GATE = 3072
COL_XBC = 6144
COL_AX = 8192
COL_AGATE = 9216
COL_BZ = 10240
N_MAIN = 11264
N_SMALL = 128


def _cparams(sem):
    return pltpu.CompilerParams(dimension_semantics=sem, vmem_limit_bytes=VMEM_LIMIT)


def _mm_kernel(x_ref, w_ref, o_ref):
    o_ref[...] = jnp.dot(x_ref[...], w_ref[...], preferred_element_type=F32).astype(o_ref.dtype)


def _matmul(x, w, layer, out_dtype, tm, tn, name):
    m, k = x.shape
    n = w.shape[2]
    return pl.pallas_call(
        _mm_kernel,
        grid=(m // tm, n // tn),
        in_specs=[pl.BlockSpec((tm, k), lambda i, j: (i, 0)),
                  pl.BlockSpec((None, k, tn), lambda i, j: (layer, 0, j))],
        out_specs=pl.BlockSpec((tm, tn), lambda i, j: (i, j)),
        out_shape=jax.ShapeDtypeStruct((m, n), out_dtype),
        compiler_params=_cparams(("parallel", "parallel")),
        name=name,
    )(x, w)


CONV_BAND = 128
BF16_ROWS = 16


def _causal_conv(x, hist, cw_ref, cb_ref, first):
    ts, ch = x.shape

    @pl.when(first)
    def _():
        hist[...] = jnp.zeros(hist.shape, BF16)

    ext = jnp.concatenate([hist[...], x], axis=0)
    hist[...] = x[ts - BF16_ROWS:ts, :]
    taps = CONV_WIDTH - 1
    out_row = lax.broadcasted_iota(jnp.int32, (taps * CONV_BAND, CONV_BAND + BF16_ROWS), 0)
    in_row = lax.broadcasted_iota(jnp.int32, (taps * CONV_BAND, CONV_BAND + BF16_ROWS), 1)
    tap = out_row // CONV_BAND
    shift = jnp.where(in_row == (out_row - tap * CONV_BAND) + BF16_ROWS - taps + tap, 1.0, 0.0).astype(BF16)
    bands = []
    for b0 in range(0, ts, CONV_BAND):
        shifted = jnp.dot(shift, ext[b0:b0 + CONV_BAND + BF16_ROWS, :], preferred_element_type=F32)
        y = cb_ref[...] + cw_ref[taps:taps + 1, :] * x[b0:b0 + CONV_BAND, :].astype(F32)
        for k in range(taps):
            y = y + cw_ref[k:k + 1, :] * shifted[k * CONV_BAND:(k + 1) * CONV_BAND, :]
        bands.append(y)
    return bands[0] if len(bands) == 1 else jnp.concatenate(bands, axis=0)


def _bf16_pieces(x, parts):
    pieces = []
    rest = x
    for _ in range(parts):
        piece = rest.astype(BF16)
        pieces.append(piece)
        rest = rest - piece.astype(F32)
    return pieces


def _select_dot(x, w, parts):
    pieces = _bf16_pieces(x, parts)
    return jnp.dot(jnp.concatenate(pieces, axis=1), jnp.concatenate([w] * parts, axis=0),
                   preferred_element_type=F32)


def _cumsum_rows(x, parts):
    n = x.shape[0]
    ri = lax.broadcasted_iota(jnp.int32, (n, n), 0)
    ci = lax.broadcasted_iota(jnp.int32, (n, n), 1)
    tril = jnp.where(ri >= ci, 1.0, 0.0).astype(BF16)
    pieces = _bf16_pieces(x, parts)
    return jnp.dot(jnp.concatenate([tril] * parts, axis=1), jnp.concatenate(pieces, axis=0),
                   preferred_element_type=F32)


ROW_TILES = D_MODEL // LANES


def _store_token_tiles(ref, x):
    rows = x.shape[0]
    for s in range(ROW_TILES):
        ref[pl.ds(s, rows, stride=ROW_TILES), :] = x[:, s * LANES:(s + 1) * LANES]


def _load_token_tiles(ref, rows):
    return jnp.concatenate([ref[pl.ds(s, rows, stride=ROW_TILES), :] for s in range(ROW_TILES)], axis=1)


def _token_tile(ref, row):
    return ref.at[pl.ds(pl.multiple_of(row * ROW_TILES, ROW_TILES), ROW_TILES), :]


def _rglru_kernel(x_ref, g_ref, cw_ref, cb_ref, wg_ref, ba_ref, bx_ref, lam_ref, o_ref, hist, hcar):
    s = pl.program_id(1)
    ts = x_ref.shape[0]
    first = s == 0

    @pl.when(first)
    def _():
        hcar[...] = jnp.zeros(hcar.shape, F32)

    xa = _causal_conv(x_ref[...], hist, cw_ref, cb_ref, first)
    xab = xa.astype(BF16)
    r_parts, i_parts = [], []
    for h in range(RNN_HEADS):
        pre = jnp.dot(xab[:, h * RNN_BLOCK:(h + 1) * RNN_BLOCK], wg_ref[h], preferred_element_type=F32)
        r_parts.append(pre[:, :RNN_BLOCK])
        i_parts.append(pre[:, RNN_BLOCK:])
    r_gate = jax.nn.sigmoid(jnp.concatenate(r_parts, axis=1) + ba_ref[...])
    i_gate = jax.nn.sigmoid(jnp.concatenate(i_parts, axis=1) + bx_ref[...])
    log_a = (-LRU_C) * r_gate * jax.nn.softplus(-lam_ref[...])
    a = jnp.exp(log_a)
    one_minus_a2 = 1.0 - jnp.exp(2.0 * log_a)
    mult = one_minus_a2 * lax.rsqrt(jnp.maximum(one_minus_a2, 1e-30))
    u = (xa * i_gate) * mult

    ng = ts // SUBLANES
    a3 = a.reshape(ng, SUBLANES, D_MODEL)
    b3 = u.reshape(ng, SUBLANES, D_MODEL)
    row = lax.broadcasted_iota(jnp.int32, a3.shape, 1)
    d = 1
    while d < SUBLANES:
        valid = row >= d
        a_s = jnp.where(valid, pltpu.roll(a3, d, axis=1), 1.0)
        b_s = jnp.where(valid, pltpu.roll(b3, d, axis=1), 0.0)
        b3 = a3 * b_s + b3
        a3 = a3 * a_s
        d *= 2
    h_in = hcar[SUBLANES - 1:SUBLANES, :]
    groups = []
    for gi in range(ng):
        hg = b3[gi] + a3[gi] * h_in
        groups.append(hg)
        h_in = hg[SUBLANES - 1:SUBLANES, :]
    h = jnp.concatenate(groups, axis=0)
    hcar[...] = groups[-1]
    o_ref[...] = (h * jax.nn.gelu(g_ref[...].astype(F32))).astype(o_ref.dtype)


def _branch_a(proj, cw, cb, wg, ba, bx, lam, bsz, seq, ts=256):
    nst = seq // ts
    full = lambda shape: pl.BlockSpec(shape, lambda b, s: (0,) * len(shape))
    return pl.pallas_call(
        _rglru_kernel,
        grid=(bsz, nst),
        in_specs=[pl.BlockSpec((ts, D_MODEL), lambda b, s: (b * nst + s, COL_AX // D_MODEL)),
                  pl.BlockSpec((ts, D_MODEL), lambda b, s: (b * nst + s, COL_AGATE // D_MODEL)),
                  full((CONV_WIDTH, D_MODEL)), full((1, D_MODEL)),
                  full((RNN_HEADS, RNN_BLOCK, 2 * RNN_BLOCK)),
                  full((1, D_MODEL)), full((1, D_MODEL)), full((1, D_MODEL))],
        out_specs=pl.BlockSpec((ts, D_MODEL), lambda b, s: (b * nst + s, 0)),
        out_shape=jax.ShapeDtypeStruct((bsz * seq, D_MODEL), BF16),
        scratch_shapes=[pltpu.VMEM((BF16_ROWS, D_MODEL), BF16), pltpu.VMEM((SUBLANES, D_MODEL), F32)],
        compiler_params=_cparams(("parallel", "arbitrary")),
        name="rglru",
    )(proj, proj, cw, cb, wg, ba, bx, lam)


def _ssd_kernel(z_ref, xbc_ref, dtf_ref, cw_ref, cb_ref, dtb_ref, alog_ref, dskip_ref, nw_ref,
                o_ref, hist, state):
    c = pl.program_id(1)
    L = SSM_CHUNK
    first = c == 0

    @pl.when(first)
    def _():
        state[...] = jnp.zeros(state.shape, F32)

    conv = _causal_conv(xbc_ref[...], hist, cw_ref, cb_ref, first)
    act = conv * jax.nn.sigmoid(conv)
    xs = act[:, :D_MODEL]
    bm = act[:, D_MODEL:D_MODEL + SSM_GROUPS * SSM_STATE]
    cm = act[:, D_MODEL + SSM_GROUPS * SSM_STATE:]

    lane = lax.broadcasted_iota(jnp.int32, (L, LANES), 1)
    head_lane = lane < SSM_HEADS
    dt = jnp.where(head_lane, jax.nn.softplus(dtf_ref[...] + dtb_ref[...]), 0.0)
    a_dt = dt * (-jnp.exp(alog_ref[...]))
    ri = lax.broadcasted_iota(jnp.int32, (L, L), 0)
    ci = lax.broadcasted_iota(jnp.int32, (L, L), 1)
    causal = ri >= ci
    cs = _cumsum_rows(a_dt, 3)
    cs_t = cs.T
    tot = cs[L - 1:L, :]
    dstate = jnp.exp(tot - cs)
    exp_cs = jnp.exp(cs)

    er = lax.broadcasted_iota(jnp.int32, (LANES, D_MODEL), 0)
    ec = lax.broadcasted_iota(jnp.int32, (LANES, D_MODEL), 1)
    expand = jnp.where(ec // SSM_HEAD_DIM == er, 1.0, 0.0).astype(BF16)
    dt_e = _select_dot(dt, expand, 2)
    dtds_e = _select_dot(dt * dstate, expand, 2)
    tot_e = _select_dot(jnp.broadcast_to(jnp.exp(tot), (SUBLANES, LANES)), expand, 3)[0:1, :]
    xdt = xs * dt_e
    xdt_end = (xs * dtds_e).astype(BF16)

    lo_half = lax.broadcasted_iota(jnp.int32, (2 * L, LANES), 1) < SSM_HEAD_DIM
    heads_per_group = SSM_HEADS // SSM_GROUPS
    y_parts = []
    new_states = []
    for g in range(SSM_GROUPS):
        bg = bm[:, g * SSM_STATE:(g + 1) * SSM_STATE]
        cg = cm[:, g * SSM_STATE:(g + 1) * SSM_STATE]
        cb = lax.dot_general(cg.astype(BF16), bg.astype(BF16), (((1,), (1,)), ((), ())),
                             preferred_element_type=F32)
        lhs = []
        for e in range(heads_per_group):
            hd = g * heads_per_group + e
            colb = jnp.broadcast_to(cs[:, hd:hd + 1], (L, L))
            rowb = jnp.broadcast_to(cs_t[hd:hd + 1, :], (L, L))
            decay = jnp.exp(jnp.where(causal, colb - rowb, -jnp.inf))
            m = (cb * decay).astype(BF16)
            c_off = (cg * jnp.broadcast_to(exp_cs[:, hd:hd + 1], (L, L))).astype(BF16)
            lhs.append(jnp.concatenate([m, c_off], axis=1))
        for j in range(heads_per_group // 2):
            col = (g * heads_per_group + 2 * j) * SSM_HEAD_DIM
            rhs = jnp.concatenate([xdt[:, col:col + LANES], state[:, col:col + LANES]], axis=0).astype(BF16)
            zero = jnp.zeros_like(rhs)
            y_parts.append(jnp.dot(lhs[2 * j], jnp.where(lo_half, rhs, zero), preferred_element_type=F32)
                           + jnp.dot(lhs[2 * j + 1], jnp.where(lo_half, zero, rhs), preferred_element_type=F32))
        gw = heads_per_group * SSM_HEAD_DIM
        new_states.append(jnp.dot(bg.T.astype(BF16), xdt_end[:, g * gw:(g + 1) * gw],
                                  preferred_element_type=F32))
    y = jnp.concatenate(y_parts, axis=1)
    state[...] = state[...] * tot_e + jnp.concatenate(new_states, axis=1)

    y = y + xs * dskip_ref[...]
    z = z_ref[...].astype(F32)
    gy = y * (z * jax.nn.sigmoid(z))
    gw = D_MODEL // SSM_GROUPS
    outs = []
    for g in range(SSM_GROUPS):
        gg = gy[:, g * gw:(g + 1) * gw]
        ms = jnp.mean(gg * gg, axis=-1, keepdims=True)
        outs.append(gg * lax.rsqrt(ms + RMS_EPS))
    o_ref[...] = (jnp.concatenate(outs, axis=1) * nw_ref[...]).astype(o_ref.dtype)


def _branch_b(proj, small, cw, cb, dtb, alog, dskip_e, nw, bsz, seq):
    L = SSM_CHUNK
    nc = seq // L
    full = lambda shape: pl.BlockSpec(shape, lambda b, c: (0,) * len(shape))
    return pl.pallas_call(
        _ssd_kernel,
        grid=(bsz, nc),
        in_specs=[pl.BlockSpec((L, D_MODEL), lambda b, c: (b * nc + c, COL_BZ // D_MODEL)),
                  pl.BlockSpec((L, SSM_CONV_CH), lambda b, c: (b * nc + c, COL_XBC // SSM_CONV_CH)),
                  pl.BlockSpec((L, N_SMALL), lambda b, c: (b * nc + c, 0)),
                  full((CONV_WIDTH, SSM_CONV_CH)), full((1, SSM_CONV_CH)),
                  full((1, N_SMALL)), full((1, N_SMALL)), full((1, D_MODEL)), full((1, D_MODEL))],
        out_specs=pl.BlockSpec((L, D_MODEL), lambda b, c: (b * nc + c, 0)),
        out_shape=jax.ShapeDtypeStruct((bsz * seq, D_MODEL), BF16),
        scratch_shapes=[pltpu.VMEM((BF16_ROWS, SSM_CONV_CH), BF16), pltpu.VMEM((SSM_STATE, D_MODEL), F32)],
        compiler_params=_cparams(("parallel", "arbitrary")),
        name="ssd",
    )(proj, proj, small, cw, cb, dtb, alog, dskip_e, nw)


CUM_BLOCK = 256
LOG2E = 1.4426950408889634
BIAS_PARTS = 3


def _fox_cum_kernel(dtf_ref, fb_ref, o_ref):
    seq = dtf_ref.shape[0]
    npair = ATTN_HEADS // 2
    hd = ATTN_HEAD_DIM
    lane = lax.broadcasted_iota(jnp.int32, (CUM_BLOCK, LANES), 1)
    live = (lane >= SSM_HEADS) & (lane < SSM_HEADS + ATTN_HEADS)
    sr = lax.broadcasted_iota(jnp.int32, (BIAS_PARTS * LANES, npair * LANES), 0)
    sc = lax.broadcasted_iota(jnp.int32, (BIAS_PARTS * LANES, npair * LANES), 1)
    piece, src = sr // LANES, sr % LANES
    head0 = SSM_HEADS + 2 * (sc // LANES)
    dst = sc % LANES
    place = jnp.where(((src == head0) & (dst == hd + piece)) | ((src == head0 + 1) & (dst == piece)),
                      1.0, 0.0).astype(BF16)
    half = lax.broadcasted_iota(jnp.int32, (CUM_BLOCK, npair * LANES), 1) % hd
    ones = (half >= BIAS_PARTS) & (half < 2 * BIAS_PARTS)
    carry = jnp.zeros((1, LANES), F32)
    for i in range(seq // CUM_BLOCK):
        rows = slice(i * CUM_BLOCK, (i + 1) * CUM_BLOCK)
        logf = jnp.where(live, jax.nn.log_sigmoid(dtf_ref[rows, :] + fb_ref[...]), 0.0)
        cb = _cumsum_rows(logf, 3) + carry
        carry = cb[CUM_BLOCK - 1:CUM_BLOCK, :]
        pieces = jnp.concatenate(_bf16_pieces(cb * (-LOG2E), BIAS_PARTS), axis=1)
        aug = jnp.dot(pieces, place, preferred_element_type=F32)
        o_ref[rows, :] = jnp.where(ones, 1.0, aug).astype(BF16)


def _fox_cum(small, fb, bsz, seq):
    width = (ATTN_HEADS // 2) * LANES
    return pl.pallas_call(
        _fox_cum_kernel,
        grid=(bsz,),
        in_specs=[pl.BlockSpec((seq, N_SMALL), lambda b: (b, 0)),
                  pl.BlockSpec((1, N_SMALL), lambda b: (0, 0))],
        out_specs=pl.BlockSpec((seq, width), lambda b: (b, 0)),
        out_shape=jax.ShapeDtypeStruct((bsz * seq, width), BF16),
        compiler_params=_cparams(("parallel",)),
        name="fox_cum",
    )(small, fb)


def _fox_attn_kernel(q_ref, k_ref, v_ref, aug_ref, o_ref, k0_s, k1_s, v0_s, v1_s, *, tq):
    seq = q_ref.shape[0]
    hd = ATTN_HEAD_DIM
    lane = lax.broadcasted_iota(jnp.int32, (seq, LANES), 1)
    lo_half = lane < hd
    aug_k = aug_ref[...]
    k = k_ref[...]
    v = v_ref[...]
    zero = jnp.zeros_like(v)
    k0_s[...] = jnp.where(lo_half, k, aug_k)
    k1_s[...] = jnp.where(lo_half, aug_k, k)
    v0_s[...] = jnp.where(lo_half, v, zero)
    v1_s[...] = jnp.where(lo_half, zero, v)

    lane_q = lax.broadcasted_iota(jnp.int32, (tq, LANES), 1)
    lo_half_q = lane_q < hd
    half_q = lane_q % hd
    q_ones = half_q < BIAS_PARTS
    q_const = (half_q >= BIAS_PARTS) & (half_q < 2 * BIAS_PARTS)
    tri = lax.broadcasted_iota(jnp.int32, (tq, tq), 1) <= lax.broadcasted_iota(jnp.int32, (tq, tq), 0)
    nt = (((1,), (1,)), ((), ()))

    for qi in reversed(range(seq // tq)):
        q0 = qi * tq
        q = q_ref[q0:q0 + tq, :]
        row0 = aug_ref[q0:q0 + BF16_ROWS, :][0:1, :].astype(F32)
        c_row = jnp.broadcast_to(pltpu.roll(-row0, BIAS_PARTS, axis=1), (tq, LANES))
        q_aug = jnp.where(q_ones, 1.0, jnp.where(q_const, c_row, 0.0)).astype(BF16)
        q_heads = (jnp.where(lo_half_q, q, q_aug), jnp.where(lo_half_q, q_aug, q))
        out = None
        for hh, (k_s, v_s) in enumerate(((k0_s, v0_s), (k1_s, v1_s))):
            s_diag = lax.dot_general(q_heads[hh], k_s[q0:q0 + tq, :], nt, preferred_element_type=F32)
            s_diag = jnp.where(tri, s_diag, -jnp.inf)
            m = jnp.max(s_diag, axis=-1, keepdims=True)
            if qi > 0:
                s_off = lax.dot_general(q_heads[hh], k_s[0:q0, :], nt, preferred_element_type=F32)
                m = jnp.maximum(m, jnp.max(s_off, axis=-1, keepdims=True))
            p_diag = jnp.exp2(s_diag - m)
            l = jnp.sum(p_diag, axis=-1, keepdims=True)
            acc = jnp.dot(p_diag.astype(BF16), v_s[q0:q0 + tq, :], preferred_element_type=F32)
            if qi > 0:
                p_off = jnp.exp2(s_off - m)
                l = l + jnp.sum(p_off, axis=-1, keepdims=True)
                acc = acc + jnp.dot(p_off.astype(BF16), v_s[0:q0, :], preferred_element_type=F32)
            acc = acc * (1.0 / l)
            out = acc if out is None else out + acc
        o_ref[q0:q0 + tq, :] = out.astype(o_ref.dtype)


def _branch_c(proj, cum, bsz, seq, tq=512):
    npair = ATTN_HEADS // 2
    kv_scratch = pltpu.VMEM((seq, LANES), BF16)
    return pl.pallas_call(
        functools.partial(_fox_attn_kernel, tq=tq),
        grid=(bsz, npair),
        in_specs=[pl.BlockSpec((seq, LANES), lambda b, p: (b, COL_QKV // LANES + p)),
                  pl.BlockSpec((seq, LANES), lambda b, p: (b, COL_QKV // LANES + npair + p)),
                  pl.BlockSpec((seq, LANES), lambda b, p: (b, COL_QKV // LANES + 2 * npair + p)),
                  pl.BlockSpec((seq, LANES), lambda b, p: (b, p))],
        out_specs=pl.BlockSpec((seq, LANES), lambda b, p: (b, p)),
        out_shape=jax.ShapeDtypeStruct((bsz * seq, D_MODEL), BF16),
        scratch_shapes=[kv_scratch, kv_scratch, kv_scratch, kv_scratch],
        compiler_params=_cparams(("parallel", "parallel")),
        name="fox_attn",
    )(proj, proj, proj, cum)


def _layer_norm(x, g, b):
    mu = jnp.mean(x, axis=-1, keepdims=True)
    xc = x - mu
    var = jnp.mean(xc * xc, axis=-1, keepdims=True)
    return xc * lax.rsqrt(var + LN_EPS) * g + b


def _top2_sum(a, b, c, d):
    hi1, lo1 = jnp.maximum(a, b), jnp.minimum(a, b)
    hi2, lo2 = jnp.maximum(c, d), jnp.minimum(c, d)
    return jnp.maximum(hi1, hi2) + jnp.maximum(jnp.minimum(hi1, hi2), jnp.maximum(lo1, lo2))


def _route_rows(logits_t):
    rows = [logits_t[e:e + 1, :] for e in range(N_EXPERTS)]
    mx = functools.reduce(jnp.maximum, rows)
    ex = [jnp.exp(r - mx) for r in rows]
    den = functools.reduce(jnp.add, ex)
    probs = [e / den for e in ex]
    ngroups = N_EXPERTS // EXPERTS_PER_GROUP
    scores = [_top2_sum(*probs[EXPERTS_PER_GROUP * g:EXPERTS_PER_GROUP * (g + 1)]) for g in range(ngroups)]
    best_g = jnp.zeros_like(mx, dtype=jnp.int32)
    best_s = scores[0]
    for g in range(1, ngroups):
        better = scores[g] > best_s
        best_g = jnp.where(better, g, best_g)
        best_s = jnp.where(better, scores[g], best_s)
    masked = [jnp.where(best_g == e // EXPERTS_PER_GROUP, probs[e], -1.0) for e in range(N_EXPERTS)]
    v1, i1 = masked[0], jnp.zeros_like(best_g)
    for e in range(1, N_EXPERTS):
        better = masked[e] > v1
        i1 = jnp.where(better, e, i1)
        v1 = jnp.where(better, masked[e], v1)
    v2, i2 = jnp.full_like(v1, -2.0), jnp.zeros_like(best_g)
    for e in range(N_EXPERTS):
        better = (masked[e] > v2) & (i1 != e)
        i2 = jnp.where(better, e, i2)
        v2 = jnp.where(better, masked[e], v2)
    tot = v1 + v2
    return i1, i2, v1 / tot, v2 / tot


def _merge_kernel(ha_ref, hb_ref, hc_ref, gate_ref, x_ref, wa_ref, wb_ref, wc_ref, wo_ref, gb_ref,
                  lg_ref, lb_ref, rwh_ref, rwl_ref, rb_ref,
                  x1t_ref, rcols_ref, ids_ref):
    tm = x_ref.shape[0]
    ya = jnp.dot(ha_ref[...], wa_ref[...], preferred_element_type=F32)
    yb = jnp.dot(hb_ref[...], wb_ref[...], preferred_element_type=F32)
    yc = jnp.dot(hc_ref[...], wc_ref[...], preferred_element_type=F32)
    g = jax.nn.sigmoid(gate_ref[...].astype(F32) + gb_ref[...])
    mixed_in = (g[:, :D_MODEL] * ya + g[:, D_MODEL:2 * D_MODEL] * yb + g[:, 2 * D_MODEL:] * yc).astype(BF16)
    mixed = jnp.dot(mixed_in, wo_ref[...], preferred_element_type=F32)
    x1 = _layer_norm(DEEPNORM_ALPHA * x_ref[...] + mixed, lg_ref[...], lb_ref[...])
    _store_token_tiles(x1t_ref, x1)
    x1h = x1.astype(BF16)
    x1l = (x1 - x1h.astype(F32)).astype(BF16)
    nt = (((1,), (1,)), ((), ()))
    logits_t = (lax.dot_general(rwh_ref[...], x1h, nt, preferred_element_type=F32)
                + lax.dot_general(rwl_ref[...], x1h, nt, preferred_element_type=F32)
                + lax.dot_general(rwh_ref[...], x1l, nt, preferred_element_type=F32)
                + rb_ref[...])
    i1, i2, w1, w2 = _route_rows(logits_t)
    sub = lax.broadcasted_iota(jnp.int32, (SUBLANES, tm), 0)
    ids_ref[...] = jnp.where(sub == 0, i1, jnp.where(sub == 1, i2, 0))
    wrows = jnp.where(sub == 0, w1, jnp.where(sub == 1, w2, 0.0))
    wrows = jnp.concatenate([wrows, jnp.zeros((LANES - SUBLANES, tm), F32)], axis=0)
    rcols_ref[...] = wrows.T


def _merge(ha, hb, hc, proj, x, wa, wb, wc, wo, gb, lg, lb, rwh, rwl, rb, tm=512):
    t = x.shape[0]
    full = lambda shape: pl.BlockSpec(shape, lambda i: (0,) * len(shape))
    row = lambda w: pl.BlockSpec((tm, w), lambda i: (i, 0))
    return pl.pallas_call(
        _merge_kernel,
        grid=(t // tm,),
        in_specs=[row(D_MODEL), row(D_MODEL), row(D_MODEL),
                  pl.BlockSpec((tm, 3 * D_MODEL), lambda i: (i, COL_GATE // (3 * D_MODEL))),
                  row(D_MODEL),
                  full((D_MODEL, D_MODEL)), full((D_MODEL, D_MODEL)), full((D_MODEL, D_MODEL)),
                  full((D_MODEL, D_MODEL)), full((1, 3 * D_MODEL)),
                  full((1, D_MODEL)), full((1, D_MODEL)),
                  full((N_EXPERTS, D_MODEL)), full((N_EXPERTS, D_MODEL)), full((N_EXPERTS, 1))],
        out_specs=[pl.BlockSpec((tm * ROW_TILES, LANES), lambda i: (i, 0)), row(LANES),
                   pl.BlockSpec((SUBLANES, tm), lambda i: (0, i))],
        out_shape=[jax.ShapeDtypeStruct((t * ROW_TILES, LANES), F32), jax.ShapeDtypeStruct((t, LANES), F32),
                   jax.ShapeDtypeStruct((SUBLANES, t), jnp.int32)],
        compiler_params=_cparams(("parallel",)),
        name="merge",
    )(ha, hb, hc, proj, x, wa, wb, wc, wo, gb, lg, lb, rwh, rwl, rb)


TOP_K = 2
MOE_TILE = 512
MOE_TILE_SHIFT = 9
PLAN_BLOCK = 256


def _moe_rows(t):
    return TOP_K * t + (N_EXPERTS + 1) * MOE_TILE


def _plan_kernel(ids_ref, pos_ref, meta_ref, cnt_s):
    t = ids_ref.shape[1]
    nblk = t // PLAN_BLOCK
    sub_e = lax.broadcasted_iota(jnp.int32, (N_EXPERTS, PLAN_BLOCK), 0)
    ur = lax.broadcasted_iota(jnp.int32, (PLAN_BLOCK, PLAN_BLOCK), 0)
    uc = lax.broadcasted_iota(jnp.int32, (PLAN_BLOCK, PLAN_BLOCK), 1)
    before = jnp.where(ur < uc, 1.0, 0.0).astype(BF16)

    def one_hots(c):
        cols = pl.ds(pl.multiple_of(c * PLAN_BLOCK, PLAN_BLOCK), PLAN_BLOCK)
        ids = ids_ref[:, cols]
        oh0 = jnp.where(ids[0:1, :] == sub_e, 1.0, 0.0)
        oh1 = jnp.where(ids[1:2, :] == sub_e, 1.0, 0.0)
        return cols, oh0, oh1

    def count_block(c, carry):
        cols, oh0, oh1 = one_hots(c)
        oh = oh0 + oh1
        cnt_s[:, cols] = jnp.dot(oh.astype(BF16), before, preferred_element_type=F32) + carry
        return carry + jnp.sum(oh, axis=1, keepdims=True)

    counts = lax.fori_loop(0, nblk, count_block, jnp.zeros((N_EXPERTS, 1), F32))
    padded = ((counts.astype(jnp.int32) + (MOE_TILE - 1)) >> MOE_TILE_SHIFT) << MOE_TILE_SHIFT
    padded = jnp.broadcast_to(padded, (N_EXPERTS, LANES))
    sub = lax.broadcasted_iota(jnp.int32, (N_EXPERTS, LANES), 0)
    lane = lax.broadcasted_iota(jnp.int32, (N_EXPERTS, LANES), 1)
    start = jnp.zeros((N_EXPERTS, LANES), jnp.int32)
    run = jnp.zeros((1, LANES), jnp.int32)
    for e in range(N_EXPERTS):
        start = jnp.where(sub == e, run, start)
        run = run + padded[e:e + 1, :]
    ended = jnp.where(start + padded <= lane * MOE_TILE, 1, 0)
    tile_expert = jnp.minimum(jnp.sum(ended, axis=0, keepdims=True), N_EXPERTS - 1)
    first_pad = start + jnp.broadcast_to(counts.astype(jnp.int32), (N_EXPERTS, LANES))
    first_pad = jnp.sum(jnp.where(sub == lane, first_pad, 0), axis=0, keepdims=True)
    sub8 = lax.broadcasted_iota(jnp.int32, (SUBLANES, LANES), 0)
    meta_ref[...] = jnp.where(sub8 == 0, tile_expert,
                              jnp.where(sub8 == 1, run >> MOE_TILE_SHIFT, jnp.where(sub8 == 2, first_pad, 0)))

    start_f = start[:, 0:1].astype(F32)
    sub8b = lax.broadcasted_iota(jnp.int32, (SUBLANES, PLAN_BLOCK), 0)

    def place_block(c, _):
        cols, oh0, oh1 = one_hots(c)
        base = cnt_s[:, cols] + start_f
        p0 = jnp.sum(oh0 * base, axis=0, keepdims=True).astype(jnp.int32)
        p1 = jnp.sum(oh1 * base, axis=0, keepdims=True).astype(jnp.int32)
        pos_ref[:, cols] = jnp.where(sub8b == 0, p0, jnp.where(sub8b == 1, p1, 0))
        return 0

    lax.fori_loop(0, nblk, place_block, 0)


def _plan(ids):
    t = ids.shape[1]
    return pl.pallas_call(
        _plan_kernel,
        out_shape=[jax.ShapeDtypeStruct((SUBLANES, t), jnp.int32),
                   jax.ShapeDtypeStruct((SUBLANES, LANES), jnp.int32)],
        scratch_shapes=[pltpu.VMEM((N_EXPERTS, t), F32)],
        compiler_params=pltpu.CompilerParams(vmem_limit_bytes=VMEM_LIMIT),
        name="moe_plan",
    )(ids)


def _dispatch_kernel(pos_ref, pad_ref, nt_ref, x_ref, xs_hbm, zeros, sems):
    tm = x_ref.shape[0] // ROW_TILES
    t = pos_ref.shape[0] // TOP_K
    base = pl.program_id(0) * tm
    first = pl.program_id(0) == 0
    tile_rows = MOE_TILE * ROW_TILES
    unused = [(j, pltpu.make_async_copy(zeros, xs_hbm.at[pl.ds(j * tile_rows, tile_rows), :], sems.at[1]))
              for j in range(xs_hbm.shape[0] // tile_rows)]

    @pl.when(first)
    def _():
        zeros[...] = jnp.zeros(zeros.shape, F32)
        fills = [pltpu.make_async_copy(zeros, xs_hbm.at[pl.ds(pad_ref[e] * ROW_TILES, tile_rows), :], sems.at[0])
                 for e in range(N_EXPERTS)]
        for f in fills:
            f.start()
        for f in fills:
            f.wait()
        for j, fill in unused:
            pl.when(j >= nt_ref[0])(fill.start)

    def body(j, _):
        for k in range(TOP_K):
            pltpu.make_async_copy(_token_tile(x_ref, j), _token_tile(xs_hbm, pos_ref[k * t + base + j]),
                                  sems.at[0]).start(priority=k)
        return 0

    lax.fori_loop(0, tm, body, 0, unroll=8)
    for k in range(TOP_K):
        pltpu.make_async_copy(x_ref, xs_hbm.at[pl.ds(0, tm * ROW_TILES), :], sems.at[0]).wait()

    @pl.when(first)
    def _():
        for j, fill in unused:
            pl.when(j >= nt_ref[0])(fill.wait)


def _dispatch(pos, first_pad, ntiles, x1t, tm=512):
    t = x1t.shape[0] // ROW_TILES
    return pl.pallas_call(
        _dispatch_kernel,
        grid_spec=pltpu.PrefetchScalarGridSpec(
            num_scalar_prefetch=3, grid=(t // tm,),
            in_specs=[pl.BlockSpec((tm * ROW_TILES, LANES), lambda i, pos, pad, nt: (i, 0))],
            out_specs=pl.BlockSpec(memory_space=pl.ANY),
            scratch_shapes=[pltpu.VMEM((MOE_TILE * ROW_TILES, LANES), F32), pltpu.SemaphoreType.DMA((2,))]),
        out_shape=jax.ShapeDtypeStruct((_moe_rows(t) * ROW_TILES, LANES), F32),
        compiler_params=_cparams(("arbitrary",)),
        name="moe_dispatch",
    )(pos, first_pad, ntiles, x1t)


def _ffn_kernel(te_ref, nt_ref, xs_ref, w1_ref, w3_ref, w2_ref, ys_ref, w1b, w3b, w2b):
    j = pl.program_id(0)
    in_use = j < nt_ref[0]

    @pl.when(in_use & ((j == 0) | (te_ref[j] != te_ref[jnp.maximum(j - 1, 0)])))
    def _():
        w1b[...] = w1_ref[0].astype(BF16)
        w3b[...] = w3_ref[0].astype(BF16)
        w2b[...] = w2_ref[0].astype(BF16)

    @pl.when(in_use)
    def _():
        xb = _load_token_tiles(xs_ref, MOE_TILE).astype(BF16)
        h1 = jnp.dot(xb, w1b[...], preferred_element_type=F32)
        h3 = jnp.dot(xb, w3b[...], preferred_element_type=F32)
        h = (h1 * jax.nn.sigmoid(h1) * h3).astype(BF16)
        _store_token_tiles(ys_ref, jnp.dot(h, w2b[...], preferred_element_type=F32))

    @pl.when(jnp.logical_not(in_use))
    def _():
        ys_ref[...] = jnp.zeros(ys_ref.shape, F32)


def _ffn(tile_expert, ntiles, xs, layer, w1, w3, w2):
    ntile = xs.shape[0] // (MOE_TILE * ROW_TILES)
    tile = lambda j, te, nt: (jnp.minimum(j, nt[0] - 1), 0)
    expert = lambda j, te, nt: (layer, te[jnp.minimum(j, nt[0] - 1)], 0, 0)
    return pl.pallas_call(
        _ffn_kernel,
        grid_spec=pltpu.PrefetchScalarGridSpec(
            num_scalar_prefetch=2, grid=(ntile,),
            in_specs=[pl.BlockSpec((MOE_TILE * ROW_TILES, LANES), tile),
                      pl.BlockSpec((None, 1, D_MODEL, D_EXPERT), expert),
                      pl.BlockSpec((None, 1, D_MODEL, D_EXPERT), expert),
                      pl.BlockSpec((None, 1, D_EXPERT, D_MODEL), expert)],
            out_specs=pl.BlockSpec((MOE_TILE * ROW_TILES, LANES), lambda j, te, nt: (j, 0)),
            scratch_shapes=[pltpu.VMEM((D_MODEL, D_EXPERT), BF16), pltpu.VMEM((D_MODEL, D_EXPERT), BF16),
                            pltpu.VMEM((D_EXPERT, D_MODEL), BF16)]),
        out_shape=jax.ShapeDtypeStruct(xs.shape, F32),
        compiler_params=_cparams(("arbitrary",)),
        name="moe_ffn",
    )(tile_expert, ntiles, xs, w1, w3, w2)


def _combine_kernel(pos_ref, ys_hbm, x1t_ref, rc_ref, lg_ref, lb_ref, o_ref, ob_ref, gath, sems):
    tm = o_ref.shape[0]
    t = pos_ref.shape[0] // TOP_K
    i = pl.program_id(0)
    ntile = pl.num_programs(0)

    def issue(tile, slot):
        def body(j, _):
            for k in range(TOP_K):
                pltpu.make_async_copy(_token_tile(ys_hbm, pos_ref[k * t + tile * tm + j]),
                                      _token_tile(gath.at[slot, k], j), sems.at[slot]).start(priority=k)
            return 0
        lax.fori_loop(0, tm, body, 0, unroll=8)

    @pl.when(i == 0)
    def _():
        issue(0, 0)

    @pl.when(i + 1 < ntile)
    def _():
        issue(i + 1, (i + 1) % 2)

    slot = i % 2
    for k in range(TOP_K):
        pltpu.make_async_copy(ys_hbm.at[pl.ds(0, tm * ROW_TILES), :], gath.at[slot, k], sems.at[slot]).wait()
    rc = rc_ref[...]
    y = (rc[:, 0:1] * _load_token_tiles(gath.at[slot, 0], tm)
         + rc[:, 1:2] * _load_token_tiles(gath.at[slot, 1], tm))
    x2 = _layer_norm(DEEPNORM_ALPHA * _load_token_tiles(x1t_ref, tm) + y, lg_ref[...], lb_ref[...])
    o_ref[...] = x2
    ob_ref[...] = x2.astype(BF16)


def _combine(pos, ys, x1t, rcols, lg, lb, tm=256):
    t = x1t.shape[0] // ROW_TILES
    row = lambda w: pl.BlockSpec((tm, w), lambda i, pos: (i, 0))
    full = lambda shape: pl.BlockSpec(shape, lambda i, pos: (0,) * len(shape))
    return pl.pallas_call(
        _combine_kernel,
        grid_spec=pltpu.PrefetchScalarGridSpec(
            num_scalar_prefetch=1, grid=(t // tm,),
            in_specs=[pl.BlockSpec(memory_space=pl.ANY),
                      pl.BlockSpec((tm * ROW_TILES, LANES), lambda i, pos: (i, 0)), row(LANES),
                      full((1, D_MODEL)), full((1, D_MODEL))],
            out_specs=[row(D_MODEL), row(D_MODEL)],
            scratch_shapes=[pltpu.VMEM((2, TOP_K, tm * ROW_TILES, LANES), F32), pltpu.SemaphoreType.DMA((2,))]),
        out_shape=[jax.ShapeDtypeStruct((t, D_MODEL), F32), jax.ShapeDtypeStruct((t, D_MODEL), BF16)],
        compiler_params=_cparams(("arbitrary",)),
        name="moe_combine",
    )(pos, ys, x1t, rcols, lg, lb)


def _moe(x1t, rcols, ids, layer, w1, w3, w2, lg, lb):
    pos8, meta = _plan(ids)
    pos = pos8[:TOP_K].reshape(-1)
    nt_max = _moe_rows(x1t.shape[0] // ROW_TILES) // MOE_TILE
    ntiles = meta[1, :1]
    xs = _dispatch(pos, meta[2, :N_EXPERTS], ntiles, x1t)
    ys = _ffn(meta[0, :nt_max], ntiles, xs, layer, w1, w3, w2)
    return _combine(pos, ys, x1t, rcols, lg, lb)


def _split_hi_lo(w):
    hi = w.astype(BF16)
    return hi, (w - hi.astype(F32)).astype(BF16)


IN_AX, IN_AGATE, IN_BZ, IN_XBC = 0, D_MODEL, 2 * D_MODEL, 3 * D_MODEL
IN_DT = IN_XBC + SSM_CONV_CH
IN_QKV = IN_DT + SSM_HEADS
IN_F = IN_QKV + 3 * D_MODEL
IN_GATE = IN_F + ATTN_HEADS
P_IN = IN_GATE + 3 * D_MODEL
W_PREP_COLS = D_MODEL
W_PREP_SRC = (IN_QKV, IN_QKV + D_MODEL, IN_QKV + 2 * D_MODEL, IN_GATE, IN_GATE + D_MODEL, IN_GATE + 2 * D_MODEL,
              IN_XBC, IN_XBC + D_MODEL, IN_AX, IN_AGATE, IN_BZ)


def _w_prep_kernel(src_ref, wt_hbm, main_ref, small_ref, buf, sbuf, sems):
    layer = pl.program_id(0)
    j = pl.program_id(1)
    rows = pl.ds(pl.multiple_of(src_ref[j], SUBLANES), W_PREP_COLS)
    block = pltpu.make_async_copy(wt_hbm.at[layer, rows, :], buf, sems.at[0])
    block.start()

    @pl.when(j == 0)
    def _():
        sbuf[...] = jnp.zeros(sbuf.shape, F32)
        parts = [pltpu.make_async_copy(wt_hbm.at[layer, pl.ds(src, n), :], sbuf.at[pl.ds(dst, n), :], sems.at[1])
                 for src, dst, n in ((IN_DT, 0, SSM_HEADS), (IN_F, SSM_HEADS, ATTN_HEADS))]
        for c in parts:
            c.start()
        for c in parts:
            c.wait()
        small_ref[0] = sbuf[...].T.astype(BF16)

    block.wait()
    scale = jnp.where(j == 0, ATTN_HEAD_DIM ** -0.5 * LOG2E, 1.0)
    main_ref[0] = (buf[...].T * scale).astype(BF16)


def _w_prep(w_in):
    depth, k, n = w_in.shape
    assert n == P_IN and COL_QKV == 0 and len(W_PREP_SRC) * W_PREP_COLS == N_MAIN
    w_t = jnp.swapaxes(w_in, 1, 2)
    return pl.pallas_call(
        _w_prep_kernel,
        grid_spec=pltpu.PrefetchScalarGridSpec(
            num_scalar_prefetch=1, grid=(depth, len(W_PREP_SRC)),
            in_specs=[pl.BlockSpec(memory_space=pl.ANY)],
            out_specs=[pl.BlockSpec((1, k, W_PREP_COLS), lambda l, j, src: (l, 0, j)),
                       pl.BlockSpec((1, k, N_SMALL), lambda l, j, src: (l, 0, 0))],
            scratch_shapes=[pltpu.VMEM((W_PREP_COLS, k), F32), pltpu.VMEM((N_SMALL, k), F32),
                            pltpu.SemaphoreType.DMA((2,))]),
        out_shape=[jax.ShapeDtypeStruct((depth, k, N_MAIN), BF16), jax.ShapeDtypeStruct((depth, k, N_SMALL), BF16)],
        compiler_params=_cparams(("arbitrary", "arbitrary")),
        name="w_prep",
    )(jnp.asarray(W_PREP_SRC, jnp.int32), w_t)


def _prepare(w_in, gate_b, conv_a_w, conv_a_b, lru_wa, lru_ba, lru_wx, lru_bx, lru_lambda,
             conv_b_w, conv_b_b, dt_bias, a_log, d_skip, ssm_norm_w, forget_b,
             w_branch_a, w_branch_b, w_branch_c, w_out, ln1_g, ln1_b,
             router_w, router_b, w1, w3, w2, ln2_g, ln2_b):
    w_main, w_small = _w_prep(w_in)
    w_gates = jnp.concatenate([lru_wa, lru_wx], axis=-1).astype(BF16)
    pad_heads = lambda v, off: jnp.pad(v, ((0, 0), (off, N_SMALL - off - v.shape[1])))[:, None, :]
    dtb_p = pad_heads(dt_bias, 0)
    alog_p = pad_heads(a_log, 0)
    fb_p = pad_heads(forget_b, SSM_HEADS)
    dskip_e = jnp.repeat(d_skip, SSM_HEAD_DIM, axis=-1)[:, None, :]
    row = lambda v: v[:, None, :]
    wa_b, wb_b, wc_b, wo_b = (w.astype(BF16) for w in (w_branch_a, w_branch_b, w_branch_c, w_out))
    rwh, rwl = _split_hi_lo(router_w.T)
    return dict(
        w_main=w_main, w_small=w_small, conv_a_w=conv_a_w, conv_a_b=row(conv_a_b), w_gates=w_gates,
        lru_ba=row(lru_ba), lru_bx=row(lru_bx), lru_lambda=row(lru_lambda),
        conv_b_w=conv_b_w, conv_b_b=row(conv_b_b), dtb=dtb_p, alog=alog_p, dskip=dskip_e,
        ssm_norm_w=row(ssm_norm_w), fb=fb_p, wa=wa_b, wb=wb_b, wc=wc_b, wo=wo_b, gate_b=row(gate_b),
        ln1_g=row(ln1_g), ln1_b=row(ln1_b), rwh=rwh, rwl=rwl, rb=router_b[:, None],
        w1=w1, w3=w3, w2=w2, ln2_g=row(ln2_g), ln2_b=row(ln2_b))


def _layer(l, xf, xb, p, bsz, seq):
    proj = _matmul(xb, p['w_main'], l, BF16, 2048, 1024, "in_proj")
    small = _matmul(xb, p['w_small'], l, F32, 1024, N_SMALL, "in_proj_small")
    ha = _branch_a(proj, p['conv_a_w'][l], p['conv_a_b'][l], p['w_gates'][l], p['lru_ba'][l], p['lru_bx'][l],
                   p['lru_lambda'][l], bsz, seq)
    hb = _branch_b(proj, small, p['conv_b_w'][l], p['conv_b_b'][l], p['dtb'][l], p['alog'][l], p['dskip'][l],
                   p['ssm_norm_w'][l], bsz, seq)
    cum = _fox_cum(small, p['fb'][l], bsz, seq)
    hc = _branch_c(proj, cum, bsz, seq)
    x1, rcols, ids = _merge(ha, hb, hc, proj, xf, p['wa'][l], p['wb'][l], p['wc'][l], p['wo'][l],
                            p['gate_b'][l], p['ln1_g'][l], p['ln1_b'][l], p['rwh'], p['rwl'], p['rb'])
    x2, x2b = _moe(x1, rcols, ids, l, p['w1'], p['w3'], p['w2'], p['ln2_g'][l], p['ln2_b'][l])
    return dict(proj=proj, small=small, ha=ha, hb=hb, cum=cum, hc=hc, x1=x1, rcols=rcols, ids=ids,
                x2=x2, x2b=x2b)


def kernel(x, w_in, gate_b, conv_a_w, conv_a_b, lru_wa, lru_ba, lru_wx, lru_bx, lru_lambda,
           conv_b_w, conv_b_b, dt_bias, a_log, d_skip, ssm_norm_w, forget_b,
           w_branch_a, w_branch_b, w_branch_c, w_out, ln1_g, ln1_b,
           router_w, router_b, w1, w3, w2, ln2_g, ln2_b):
    bsz, seq, d = x.shape
    p = _prepare(w_in, gate_b, conv_a_w, conv_a_b, lru_wa, lru_ba, lru_wx, lru_bx, lru_lambda,
                 conv_b_w, conv_b_b, dt_bias, a_log, d_skip, ssm_norm_w, forget_b,
                 w_branch_a, w_branch_b, w_branch_c, w_out, ln1_g, ln1_b,
                 router_w, router_b, w1, w3, w2, ln2_g, ln2_b)
    xf = x.reshape(bsz * seq, d)
    xb = xf.astype(BF16)
    for l in range(w_in.shape[0]):
        stages = _layer(l, xf, xb, p, bsz, seq)
        xf, xb = stages['x2'], stages['x2b']
    return xf.reshape(bsz, seq, d)
```

```python
import functools

import jax
import jax.numpy as jnp
from jax import lax
from jax.experimental import pallas as pl
from jax.experimental.pallas import tpu as pltpu

F32 = jnp.float32
BF16 = jnp.bfloat16

D_MODEL = 1024
DEPTH = 4
RNN_HEADS = 8
RNN_BLOCK = 128
CONV_WIDTH = 4
LRU_C = 8.0
SSM_HEADS = 16
SSM_HEAD_DIM = 64
SSM_GROUPS = 4
SSM_STATE = 128
SSM_CHUNK = 128
SSM_CONV_CH = 2048
ATTN_HEADS = 16
ATTN_HEAD_DIM = 64
N_EXPERTS = 16
EXPERTS_PER_GROUP = 4
D_EXPERT = 512
LN_EPS = 1e-5
RMS_EPS = 1e-6
DEEPNORM_ALPHA = (2 * DEPTH) ** 0.25

LANES = 128
SUBLANES = 8
VMEM_LIMIT = 48 * 1024 * 1024

COL_QKV = 0
COL_GATE = 3072
COL_XBC = 6144
COL_AX = 8192
COL_AGATE = 9216
COL_BZ = 10240
N_MAIN = 11264
N_SMALL = 128


def _cparams(sem):
    return pltpu.CompilerParams(dimension_semantics=sem, vmem_limit_bytes=VMEM_LIMIT)


def _mm_kernel(x_ref, w_ref, o_ref):
    o_ref[...] = jnp.dot(x_ref[...], w_ref[...], preferred_element_type=F32).astype(o_ref.dtype)


def _matmul(x, w, layer, out_dtype, tm, tn, name):
    m, k = x.shape
    n = w.shape[2]
    return pl.pallas_call(
        _mm_kernel,
        grid=(m // tm, n // tn),
        in_specs=[pl.BlockSpec((tm, k), lambda i, j: (i, 0)),
                  pl.BlockSpec((None, k, tn), lambda i, j: (layer, 0, j))],
        out_specs=pl.BlockSpec((tm, tn), lambda i, j: (i, j)),
        out_shape=jax.ShapeDtypeStruct((m, n), out_dtype),
        compiler_params=_cparams(("parallel", "parallel")),
        name=name,
    )(x, w)


CONV_BAND = 128
BF16_ROWS = 16


def _causal_conv(x, hist, cw_ref, cb_ref, first):
    ts, ch = x.shape

    @pl.when(first)
    def _():
        hist[...] = jnp.zeros(hist.shape, BF16)

    ext = jnp.concatenate([hist[...], x], axis=0)
    hist[...] = x[ts - BF16_ROWS:ts, :]
    taps = CONV_WIDTH - 1
    out_row = lax.broadcasted_iota(jnp.int32, (taps * CONV_BAND, CONV_BAND + BF16_ROWS), 0)
    in_row = lax.broadcasted_iota(jnp.int32, (taps * CONV_BAND, CONV_BAND + BF16_ROWS), 1)
    tap = out_row // CONV_BAND
    shift = jnp.where(in_row == (out_row - tap * CONV_BAND) + BF16_ROWS - taps + tap, 1.0, 0.0).astype(BF16)
    bands = []
    for b0 in range(0, ts, CONV_BAND):
        shifted = jnp.dot(shift, ext[b0:b0 + CONV_BAND + BF16_ROWS, :], preferred_element_type=F32)
        y = cb_ref[...] + cw_ref[taps:taps + 1, :] * x[b0:b0 + CONV_BAND, :].astype(F32)
        for k in range(taps):
            y = y + cw_ref[k:k + 1, :] * shifted[k * CONV_BAND:(k + 1) * CONV_BAND, :]
        bands.append(y)
    return bands[0] if len(bands) == 1 else jnp.concatenate(bands, axis=0)


def _bf16_pieces(x, parts):
    pieces = []
    rest = x
    for _ in range(parts):
        piece = rest.astype(BF16)
        pieces.append(piece)
        rest = rest - piece.astype(F32)
    return pieces


def _select_dot(x, w, parts):
    pieces = _bf16_pieces(x, parts)
    return jnp.dot(jnp.concatenate(pieces, axis=1), jnp.concatenate([w] * parts, axis=0),
                   preferred_element_type=F32)


def _cumsum_rows(x, parts):
    n = x.shape[0]
    ri = lax.broadcasted_iota(jnp.int32, (n, n), 0)
    ci = lax.broadcasted_iota(jnp.int32, (n, n), 1)
    tril = jnp.where(ri >= ci, 1.0, 0.0).astype(BF16)
    pieces = _bf16_pieces(x, parts)
    return jnp.dot(jnp.concatenate([tril] * parts, axis=1), jnp.concatenate(pieces, axis=0),
                   preferred_element_type=F32)


ROW_TILES = D_MODEL // LANES


def _store_token_tiles(ref, x):
    rows = x.shape[0]
    for s in range(ROW_TILES):
        ref[pl.ds(s, rows, stride=ROW_TILES), :] = x[:, s * LANES:(s + 1) * LANES]


def _load_token_tiles(ref, rows):
    return jnp.concatenate([ref[pl.ds(s, rows, stride=ROW_TILES), :] for s in range(ROW_TILES)], axis=1)


def _token_tile(ref, row):
    return ref.at[pl.ds(pl.multiple_of(row * ROW_TILES, ROW_TILES), ROW_TILES), :]


def _rglru_kernel(x_ref, g_ref, cw_ref, cb_ref, wg_ref, ba_ref, bx_ref, lam_ref, o_ref, hist, hcar):
    s = pl.program_id(1)
    ts = x_ref.shape[0]
    first = s == 0

    @pl.when(first)
    def _():
        hcar[...] = jnp.zeros(hcar.shape, F32)

    xa = _causal_conv(x_ref[...], hist, cw_ref, cb_ref, first)
    xab = xa.astype(BF16)
    r_parts, i_parts = [], []
    for h in range(RNN_HEADS):
        pre = jnp.dot(xab[:, h * RNN_BLOCK:(h + 1) * RNN_BLOCK], wg_ref[h], preferred_element_type=F32)
        r_parts.append(pre[:, :RNN_BLOCK])
        i_parts.append(pre[:, RNN_BLOCK:])
    r_gate = jax.nn.sigmoid(jnp.concatenate(r_parts, axis=1) + ba_ref[...])
    i_gate = jax.nn.sigmoid(jnp.concatenate(i_parts, axis=1) + bx_ref[...])
    log_a = (-LRU_C) * r_gate * jax.nn.softplus(-lam_ref[...])
    a = jnp.exp(log_a)
    one_minus_a2 = 1.0 - jnp.exp(2.0 * log_a)
    mult = one_minus_a2 * lax.rsqrt(jnp.maximum(one_minus_a2, 1e-30))
    u = (xa * i_gate) * mult

    ng = ts // SUBLANES
    a3 = a.reshape(ng, SUBLANES, D_MODEL)
    b3 = u.reshape(ng, SUBLANES, D_MODEL)
    row = lax.broadcasted_iota(jnp.int32, a3.shape, 1)
    d = 1
    while d < SUBLANES:
        valid = row >= d
        a_s = jnp.where(valid, pltpu.roll(a3, d, axis=1), 1.0)
        b_s = jnp.where(valid, pltpu.roll(b3, d, axis=1), 0.0)
        b3 = a3 * b_s + b3
        a3 = a3 * a_s
        d *= 2
    h_in = hcar[SUBLANES - 1:SUBLANES, :]
    groups = []
    for gi in range(ng):
        hg = b3[gi] + a3[gi] * h_in
        groups.append(hg)
        h_in = hg[SUBLANES - 1:SUBLANES, :]
    h = jnp.concatenate(groups, axis=0)
    hcar[...] = groups[-1]
    o_ref[...] = (h * jax.nn.gelu(g_ref[...].astype(F32))).astype(o_ref.dtype)


def _branch_a(proj, cw, cb, wg, ba, bx, lam, bsz, seq, ts=256):
    nst = seq // ts
    full = lambda shape: pl.BlockSpec(shape, lambda b, s: (0,) * len(shape))
    return pl.pallas_call(
        _rglru_kernel,
        grid=(bsz, nst),
        in_specs=[pl.BlockSpec((ts, D_MODEL), lambda b, s: (b * nst + s, COL_AX // D_MODEL)),
                  pl.BlockSpec((ts, D_MODEL), lambda b, s: (b * nst + s, COL_AGATE // D_MODEL)),
                  full((CONV_WIDTH, D_MODEL)), full((1, D_MODEL)),
                  full((RNN_HEADS, RNN_BLOCK, 2 * RNN_BLOCK)),
                  full((1, D_MODEL)), full((1, D_MODEL)), full((1, D_MODEL))],
        out_specs=pl.BlockSpec((ts, D_MODEL), lambda b, s: (b * nst + s, 0)),
        out_shape=jax.ShapeDtypeStruct((bsz * seq, D_MODEL), BF16),
        scratch_shapes=[pltpu.VMEM((BF16_ROWS, D_MODEL), BF16), pltpu.VMEM((SUBLANES, D_MODEL), F32)],
        compiler_params=_cparams(("parallel", "arbitrary")),
        name="rglru",
    )(proj, proj, cw, cb, wg, ba, bx, lam)


def _ssd_kernel(z_ref, xbc_ref, dtf_ref, cw_ref, cb_ref, dtb_ref, alog_ref, dskip_ref, nw_ref,
                o_ref, hist, state):
    first = pl.program_id(1) == 0

    @pl.when(first)
    def _():
        state[...] = jnp.zeros(state.shape, F32)

    conv = _causal_conv(xbc_ref[...], hist, cw_ref, cb_ref, first)
    for r0 in range(0, conv.shape[0], SSM_CHUNK):
        rows = slice(r0, r0 + SSM_CHUNK)
        o_ref[rows, :] = _ssd_chunk(conv[rows], dtf_ref[rows, :], z_ref[rows, :].astype(F32),
                                    dtb_ref, alog_ref, dskip_ref, nw_ref, state).astype(o_ref.dtype)


def _ssd_chunk(conv, dtf, z, dtb_ref, alog_ref, dskip_ref, nw_ref, state):
    L = SSM_CHUNK
    act = conv * jax.nn.sigmoid(conv)
    xs = act[:, :D_MODEL]
    bm = act[:, D_MODEL:D_MODEL + SSM_GROUPS * SSM_STATE]
    cm = act[:, D_MODEL + SSM_GROUPS * SSM_STATE:]

    lane = lax.broadcasted_iota(jnp.int32, (L, LANES), 1)
    head_lane = lane < SSM_HEADS
    dt = jnp.where(head_lane, jax.nn.softplus(dtf + dtb_ref[...]), 0.0)
    a_dt = dt * (-jnp.exp(alog_ref[...]))
    ri = lax.broadcasted_iota(jnp.int32, (L, L), 0)
    ci = lax.broadcasted_iota(jnp.int32, (L, L), 1)
    causal = ri >= ci
    cs = _cumsum_rows(a_dt, 3)
    cs_t = cs.T
    tot = cs[L - 1:L, :]
    dstate = jnp.exp(tot - cs)
    exp_cs = jnp.exp(cs)

    er = lax.broadcasted_iota(jnp.int32, (LANES, D_MODEL), 0)
    ec = lax.broadcasted_iota(jnp.int32, (LANES, D_MODEL), 1)
    expand = jnp.where(ec // SSM_HEAD_DIM == er, 1.0, 0.0).astype(BF16)
    dt_e = _select_dot(dt, expand, 2)
    dtds_e = _select_dot(dt * dstate, expand, 2)
    tot_e = _select_dot(jnp.broadcast_to(jnp.exp(tot), (SUBLANES, LANES)), expand, 3)[0:1, :]
    xdt = xs * dt_e
    xdt_end = (xs * dtds_e).astype(BF16)

    lo_half = lax.broadcasted_iota(jnp.int32, (2 * L, LANES), 1) < SSM_HEAD_DIM
    heads_per_group = SSM_HEADS // SSM_GROUPS
    y_parts = []
    new_states = []
    for g in range(SSM_GROUPS):
        bg = bm[:, g * SSM_STATE:(g + 1) * SSM_STATE]
        cg = cm[:, g * SSM_STATE:(g + 1) * SSM_STATE]
        cb = lax.dot_general(cg.astype(BF16), bg.astype(BF16), (((1,), (1,)), ((), ())),
                             preferred_element_type=F32)
        lhs = []
        for e in range(heads_per_group):
            hd = g * heads_per_group + e
            colb = jnp.broadcast_to(cs[:, hd:hd + 1], (L, L))
            rowb = jnp.broadcast_to(cs_t[hd:hd + 1, :], (L, L))
            decay = jnp.exp(jnp.where(causal, colb - rowb, -jnp.inf))
            m = (cb * decay).astype(BF16)
            c_off = (cg * jnp.broadcast_to(exp_cs[:, hd:hd + 1], (L, L))).astype(BF16)
            lhs.append(jnp.concatenate([m, c_off], axis=1))
        for j in range(heads_per_group // 2):
            col = (g * heads_per_group + 2 * j) * SSM_HEAD_DIM
            rhs = jnp.concatenate([xdt[:, col:col + LANES], state[:, col:col + LANES]], axis=0).astype(BF16)
            zero = jnp.zeros_like(rhs)
            y_parts.append(jnp.dot(lhs[2 * j], jnp.where(lo_half, rhs, zero), preferred_element_type=F32)
                           + jnp.dot(lhs[2 * j + 1], jnp.where(lo_half, zero, rhs), preferred_element_type=F32))
        gw = heads_per_group * SSM_HEAD_DIM
        new_states.append(jnp.dot(bg.T.astype(BF16), xdt_end[:, g * gw:(g + 1) * gw],
                                  preferred_element_type=F32))
    y = jnp.concatenate(y_parts, axis=1)
    state[...] = state[...] * tot_e + jnp.concatenate(new_states, axis=1)

    y = y + xs * dskip_ref[...]
    gy = y * (z * jax.nn.sigmoid(z))
    gw = D_MODEL // SSM_GROUPS
    outs = []
    for g in range(SSM_GROUPS):
        gg = gy[:, g * gw:(g + 1) * gw]
        ms = jnp.mean(gg * gg, axis=-1, keepdims=True)
        outs.append(gg * lax.rsqrt(ms + RMS_EPS))
    return jnp.concatenate(outs, axis=1) * nw_ref[...]


def _branch_b(proj, small, cw, cb, dtb, alog, dskip_e, nw, bsz, seq, chunks_per_step=4):
    L = chunks_per_step * SSM_CHUNK
    nc = seq // L
    full = lambda shape: pl.BlockSpec(shape, lambda b, c: (0,) * len(shape))
    return pl.pallas_call(
        _ssd_kernel,
        grid=(bsz, nc),
        in_specs=[pl.BlockSpec((L, D_MODEL), lambda b, c: (b * nc + c, COL_BZ // D_MODEL)),
                  pl.BlockSpec((L, SSM_CONV_CH), lambda b, c: (b * nc + c, COL_XBC // SSM_CONV_CH)),
                  pl.BlockSpec((L, N_SMALL), lambda b, c: (b * nc + c, 0)),
                  full((CONV_WIDTH, SSM_CONV_CH)), full((1, SSM_CONV_CH)),
                  full((1, N_SMALL)), full((1, N_SMALL)), full((1, D_MODEL)), full((1, D_MODEL))],
        out_specs=pl.BlockSpec((L, D_MODEL), lambda b, c: (b * nc + c, 0)),
        out_shape=jax.ShapeDtypeStruct((bsz * seq, D_MODEL), BF16),
        scratch_shapes=[pltpu.VMEM((BF16_ROWS, SSM_CONV_CH), BF16), pltpu.VMEM((SSM_STATE, D_MODEL), F32)],
        compiler_params=_cparams(("parallel", "arbitrary")),
        name="ssd",
    )(proj, proj, small, cw, cb, dtb, alog, dskip_e, nw)


CUM_BLOCK = 256
LOG2E = 1.4426950408889634
BIAS_PARTS = 3


def _fox_cum_kernel(dtf_ref, fb_ref, o_ref):
    seq = dtf_ref.shape[0]
    npair = ATTN_HEADS // 2
    hd = ATTN_HEAD_DIM
    lane = lax.broadcasted_iota(jnp.int32, (CUM_BLOCK, LANES), 1)
    live = (lane >= SSM_HEADS) & (lane < SSM_HEADS + ATTN_HEADS)
    sr = lax.broadcasted_iota(jnp.int32, (BIAS_PARTS * LANES, npair * LANES), 0)
    sc = lax.broadcasted_iota(jnp.int32, (BIAS_PARTS * LANES, npair * LANES), 1)
    piece, src = sr // LANES, sr % LANES
    head0 = SSM_HEADS + 2 * (sc // LANES)
    dst = sc % LANES
    place = jnp.where(((src == head0) & (dst == hd + piece)) | ((src == head0 + 1) & (dst == piece)),
                      1.0, 0.0).astype(BF16)
    half = lax.broadcasted_iota(jnp.int32, (CUM_BLOCK, npair * LANES), 1) % hd
    ones = (half >= BIAS_PARTS) & (half < 2 * BIAS_PARTS)
    carry = jnp.zeros((1, LANES), F32)
    for i in range(seq // CUM_BLOCK):
        rows = slice(i * CUM_BLOCK, (i + 1) * CUM_BLOCK)
        logf = jnp.where(live, jax.nn.log_sigmoid(dtf_ref[rows, :] + fb_ref[...]), 0.0)
        cb = _cumsum_rows(logf, 3) + carry
        carry = cb[CUM_BLOCK - 1:CUM_BLOCK, :]
        pieces = jnp.concatenate(_bf16_pieces(cb * (-LOG2E), BIAS_PARTS), axis=1)
        aug = jnp.dot(pieces, place, preferred_element_type=F32)
        o_ref[rows, :] = jnp.where(ones, 1.0, aug).astype(BF16)


def _fox_cum(small, fb, bsz, seq):
    width = (ATTN_HEADS // 2) * LANES
    return pl.pallas_call(
        _fox_cum_kernel,
        grid=(bsz,),
        in_specs=[pl.BlockSpec((seq, N_SMALL), lambda b: (b, 0)),
                  pl.BlockSpec((1, N_SMALL), lambda b: (0, 0))],
        out_specs=pl.BlockSpec((seq, width), lambda b: (b, 0)),
        out_shape=jax.ShapeDtypeStruct((bsz * seq, width), BF16),
        compiler_params=_cparams(("parallel",)),
        name="fox_cum",
    )(small, fb)


def _fox_attn_kernel(q_ref, k_ref, v_ref, aug_ref, o_ref, k0_s, k1_s, v0_s, v1_s, *, tq):
    seq = q_ref.shape[0]
    hd = ATTN_HEAD_DIM
    lane = lax.broadcasted_iota(jnp.int32, (seq, LANES), 1)
    lo_half = lane < hd
    aug_k = aug_ref[...]
    k = k_ref[...]
    v = v_ref[...]
    zero = jnp.zeros_like(v)
    k0_s[...] = jnp.where(lo_half, k, aug_k)
    k1_s[...] = jnp.where(lo_half, aug_k, k)
    v0_s[...] = jnp.where(lo_half, v, zero)
    v1_s[...] = jnp.where(lo_half, zero, v)

    lane_q = lax.broadcasted_iota(jnp.int32, (tq, LANES), 1)
    lo_half_q = lane_q < hd
    half_q = lane_q % hd
    q_ones = half_q < BIAS_PARTS
    q_const = (half_q >= BIAS_PARTS) & (half_q < 2 * BIAS_PARTS)
    tri = lax.broadcasted_iota(jnp.int32, (tq, tq), 1) <= lax.broadcasted_iota(jnp.int32, (tq, tq), 0)
    nt = (((1,), (1,)), ((), ()))

    for qi in reversed(range(seq // tq)):
        q0 = qi * tq
        q = q_ref[q0:q0 + tq, :]
        row0 = aug_ref[q0:q0 + BF16_ROWS, :][0:1, :].astype(F32)
        c_row = jnp.broadcast_to(pltpu.roll(-row0, BIAS_PARTS, axis=1), (tq, LANES))
        q_aug = jnp.where(q_ones, 1.0, jnp.where(q_const, c_row, 0.0)).astype(BF16)
        q_heads = (jnp.where(lo_half_q, q, q_aug), jnp.where(lo_half_q, q_aug, q))
        out = None
        for hh, (k_s, v_s) in enumerate(((k0_s, v0_s), (k1_s, v1_s))):
            s_diag = lax.dot_general(q_heads[hh], k_s[q0:q0 + tq, :], nt, preferred_element_type=F32)
            s_diag = jnp.where(tri, s_diag, -jnp.inf)
            m = jnp.max(s_diag, axis=-1, keepdims=True)
            if qi > 0:
                s_off = lax.dot_general(q_heads[hh], k_s[0:q0, :], nt, preferred_element_type=F32)
                m = jnp.maximum(m, jnp.max(s_off, axis=-1, keepdims=True))
            p_diag = jnp.exp2(s_diag - m)
            l = jnp.sum(p_diag, axis=-1, keepdims=True)
            acc = jnp.dot(p_diag.astype(BF16), v_s[q0:q0 + tq, :], preferred_element_type=F32)
            if qi > 0:
                p_off = jnp.exp2(s_off - m)
                l = l + jnp.sum(p_off, axis=-1, keepdims=True)
                acc = acc + jnp.dot(p_off.astype(BF16), v_s[0:q0, :], preferred_element_type=F32)
            acc = acc * (1.0 / l)
            out = acc if out is None else out + acc
        o_ref[q0:q0 + tq, :] = out.astype(o_ref.dtype)


def _branch_c(proj, cum, bsz, seq, tq=512):
    npair = ATTN_HEADS // 2
    kv_scratch = pltpu.VMEM((seq, LANES), BF16)
    return pl.pallas_call(
        functools.partial(_fox_attn_kernel, tq=tq),
        grid=(bsz, npair),
        in_specs=[pl.BlockSpec((seq, LANES), lambda b, p: (b, COL_QKV // LANES + p)),
                  pl.BlockSpec((seq, LANES), lambda b, p: (b, COL_QKV // LANES + npair + p)),
                  pl.BlockSpec((seq, LANES), lambda b, p: (b, COL_QKV // LANES + 2 * npair + p)),
                  pl.BlockSpec((seq, LANES), lambda b, p: (b, p))],
        out_specs=pl.BlockSpec((seq, LANES), lambda b, p: (b, p)),
        out_shape=jax.ShapeDtypeStruct((bsz * seq, D_MODEL), BF16),
        scratch_shapes=[kv_scratch, kv_scratch, kv_scratch, kv_scratch],
        compiler_params=_cparams(("parallel", "parallel")),
        name="fox_attn",
    )(proj, proj, proj, cum)


def _layer_norm(x, g, b):
    mu = jnp.mean(x, axis=-1, keepdims=True)
    xc = x - mu
    var = jnp.mean(xc * xc, axis=-1, keepdims=True)
    return xc * lax.rsqrt(var + LN_EPS) * g + b


def _top2_sum(a, b, c, d):
    hi1, lo1 = jnp.maximum(a, b), jnp.minimum(a, b)
    hi2, lo2 = jnp.maximum(c, d), jnp.minimum(c, d)
    return jnp.maximum(hi1, hi2) + jnp.maximum(jnp.minimum(hi1, hi2), jnp.maximum(lo1, lo2))


def _route_rows(logits_t):
    rows = [logits_t[e:e + 1, :] for e in range(N_EXPERTS)]
    mx = functools.reduce(jnp.maximum, rows)
    ex = [jnp.exp(r - mx) for r in rows]
    den = functools.reduce(jnp.add, ex)
    probs = [e / den for e in ex]
    ngroups = N_EXPERTS // EXPERTS_PER_GROUP
    scores = [_top2_sum(*probs[EXPERTS_PER_GROUP * g:EXPERTS_PER_GROUP * (g + 1)]) for g in range(ngroups)]
    best_g = jnp.zeros_like(mx, dtype=jnp.int32)
    best_s = scores[0]
    for g in range(1, ngroups):
        better = scores[g] > best_s
        best_g = jnp.where(better, g, best_g)
        best_s = jnp.where(better, scores[g], best_s)
    masked = [jnp.where(best_g == e // EXPERTS_PER_GROUP, probs[e], -1.0) for e in range(N_EXPERTS)]
    v1, i1 = masked[0], jnp.zeros_like(best_g)
    for e in range(1, N_EXPERTS):
        better = masked[e] > v1
        i1 = jnp.where(better, e, i1)
        v1 = jnp.where(better, masked[e], v1)
    v2, i2 = jnp.full_like(v1, -2.0), jnp.zeros_like(best_g)
    for e in range(N_EXPERTS):
        better = (masked[e] > v2) & (i1 != e)
        i2 = jnp.where(better, e, i2)
        v2 = jnp.where(better, masked[e], v2)
    tot = v1 + v2
    return i1, i2, v1 / tot, v2 / tot


def _merge_kernel(ha_ref, hb_ref, hc_ref, gate_ref, x_ref, wa_ref, wb_ref, wc_ref, wo_ref, gb_ref,
                  lg_ref, lb_ref, rwh_ref, rwl_ref, rb_ref,
                  x1t_ref, rcols_ref, ids_ref):
    tm = x_ref.shape[0]
    ya = jnp.dot(ha_ref[...], wa_ref[...], preferred_element_type=F32)
    yb = jnp.dot(hb_ref[...], wb_ref[...], preferred_element_type=F32)
    yc = jnp.dot(hc_ref[...], wc_ref[...], preferred_element_type=F32)
    g = jax.nn.sigmoid(gate_ref[...].astype(F32) + gb_ref[...])
    mixed_in = (g[:, :D_MODEL] * ya + g[:, D_MODEL:2 * D_MODEL] * yb + g[:, 2 * D_MODEL:] * yc).astype(BF16)
    mixed = jnp.dot(mixed_in, wo_ref[...], preferred_element_type=F32)
    x1 = _layer_norm(DEEPNORM_ALPHA * x_ref[...] + mixed, lg_ref[...], lb_ref[...])
    _store_token_tiles(x1t_ref, x1)
    x1h = x1.astype(BF16)
    x1l = (x1 - x1h.astype(F32)).astype(BF16)
    nt = (((1,), (1,)), ((), ()))
    logits_t = (lax.dot_general(rwh_ref[...], x1h, nt, preferred_element_type=F32)
                + lax.dot_general(rwl_ref[...], x1h, nt, preferred_element_type=F32)
                + lax.dot_general(rwh_ref[...], x1l, nt, preferred_element_type=F32)
                + rb_ref[...])
    i1, i2, w1, w2 = _route_rows(logits_t)
    sub = lax.broadcasted_iota(jnp.int32, (SUBLANES, tm), 0)
    ids_ref[...] = jnp.where(sub == 0, i1, jnp.where(sub == 1, i2, 0))
    wrows = jnp.where(sub == 0, w1, jnp.where(sub == 1, w2, 0.0))
    wrows = jnp.concatenate([wrows, jnp.zeros((LANES - SUBLANES, tm), F32)], axis=0)
    rcols_ref[...] = wrows.T


def _merge(ha, hb, hc, proj, x, wa, wb, wc, wo, gb, lg, lb, rwh, rwl, rb, tm=512):
    t = x.shape[0]
    full = lambda shape: pl.BlockSpec(shape, lambda i: (0,) * len(shape))
    row = lambda w: pl.BlockSpec((tm, w), lambda i: (i, 0))
    return pl.pallas_call(
        _merge_kernel,
        grid=(t // tm,),
        in_specs=[row(D_MODEL), row(D_MODEL), row(D_MODEL),
                  pl.BlockSpec((tm, 3 * D_MODEL), lambda i: (i, COL_GATE // (3 * D_MODEL))),
                  row(D_MODEL),
                  full((D_MODEL, D_MODEL)), full((D_MODEL, D_MODEL)), full((D_MODEL, D_MODEL)),
                  full((D_MODEL, D_MODEL)), full((1, 3 * D_MODEL)),
                  full((1, D_MODEL)), full((1, D_MODEL)),
                  full((N_EXPERTS, D_MODEL)), full((N_EXPERTS, D_MODEL)), full((N_EXPERTS, 1))],
        out_specs=[pl.BlockSpec((tm * ROW_TILES, LANES), lambda i: (i, 0)), row(LANES),
                   pl.BlockSpec((SUBLANES, tm), lambda i: (0, i))],
        out_shape=[jax.ShapeDtypeStruct((t * ROW_TILES, LANES), F32), jax.ShapeDtypeStruct((t, LANES), F32),
                   jax.ShapeDtypeStruct((SUBLANES, t), jnp.int32)],
        compiler_params=_cparams(("parallel",)),
        name="merge",
    )(ha, hb, hc, proj, x, wa, wb, wc, wo, gb, lg, lb, rwh, rwl, rb)


TOP_K = 2
MOE_TILE = 512
MOE_TILE_SHIFT = 9
PLAN_BLOCK = 256


def _moe_rows(t):
    return TOP_K * t + (N_EXPERTS + 1) * MOE_TILE


def _plan_kernel(ids_ref, pos_ref, meta_ref, cnt_s):
    t = ids_ref.shape[1]
    nblk = t // PLAN_BLOCK
    sub_e = lax.broadcasted_iota(jnp.int32, (N_EXPERTS, PLAN_BLOCK), 0)
    ur = lax.broadcasted_iota(jnp.int32, (PLAN_BLOCK, PLAN_BLOCK), 0)
    uc = lax.broadcasted_iota(jnp.int32, (PLAN_BLOCK, PLAN_BLOCK), 1)
    before = jnp.where(ur < uc, 1.0, 0.0).astype(BF16)

    def one_hots(c):
        cols = pl.ds(pl.multiple_of(c * PLAN_BLOCK, PLAN_BLOCK), PLAN_BLOCK)
        ids = ids_ref[:, cols]
        oh0 = jnp.where(ids[0:1, :] == sub_e, 1.0, 0.0)
        oh1 = jnp.where(ids[1:2, :] == sub_e, 1.0, 0.0)
        return cols, oh0, oh1

    def count_block(c, carry):
        cols, oh0, oh1 = one_hots(c)
        oh = oh0 + oh1
        cnt_s[:, cols] = jnp.dot(oh.astype(BF16), before, preferred_element_type=F32) + carry
        return carry + jnp.sum(oh, axis=1, keepdims=True)

    counts = lax.fori_loop(0, nblk, count_block, jnp.zeros((N_EXPERTS, 1), F32))
    padded = ((counts.astype(jnp.int32) + (MOE_TILE - 1)) >> MOE_TILE_SHIFT) << MOE_TILE_SHIFT
    padded = jnp.broadcast_to(padded, (N_EXPERTS, LANES))
    sub = lax.broadcasted_iota(jnp.int32, (N_EXPERTS, LANES), 0)
    lane = lax.broadcasted_iota(jnp.int32, (N_EXPERTS, LANES), 1)
    start = jnp.zeros((N_EXPERTS, LANES), jnp.int32)
    run = jnp.zeros((1, LANES), jnp.int32)
    for e in range(N_EXPERTS):
        start = jnp.where(sub == e, run, start)
        run = run + padded[e:e + 1, :]
    ended = jnp.where(start + padded <= lane * MOE_TILE, 1, 0)
    tile_expert = jnp.minimum(jnp.sum(ended, axis=0, keepdims=True), N_EXPERTS - 1)
    first_pad = start + jnp.broadcast_to(counts.astype(jnp.int32), (N_EXPERTS, LANES))
    first_pad = jnp.sum(jnp.where(sub == lane, first_pad, 0), axis=0, keepdims=True)
    sub8 = lax.broadcasted_iota(jnp.int32, (SUBLANES, LANES), 0)
    meta_ref[...] = jnp.where(sub8 == 0, tile_expert,
                              jnp.where(sub8 == 1, run >> MOE_TILE_SHIFT, jnp.where(sub8 == 2, first_pad, 0)))

    start_f = start[:, 0:1].astype(F32)
    sub8b = lax.broadcasted_iota(jnp.int32, (SUBLANES, PLAN_BLOCK), 0)

    def place_block(c, _):
        cols, oh0, oh1 = one_hots(c)
        base = cnt_s[:, cols] + start_f
        p0 = jnp.sum(oh0 * base, axis=0, keepdims=True).astype(jnp.int32)
        p1 = jnp.sum(oh1 * base, axis=0, keepdims=True).astype(jnp.int32)
        pos_ref[:, cols] = jnp.where(sub8b == 0, p0, jnp.where(sub8b == 1, p1, 0))
        return 0

    lax.fori_loop(0, nblk, place_block, 0)


def _plan(ids):
    t = ids.shape[1]
    return pl.pallas_call(
        _plan_kernel,
        out_shape=[jax.ShapeDtypeStruct((SUBLANES, t), jnp.int32),
                   jax.ShapeDtypeStruct((SUBLANES, LANES), jnp.int32)],
        scratch_shapes=[pltpu.VMEM((N_EXPERTS, t), F32)],
        compiler_params=pltpu.CompilerParams(vmem_limit_bytes=VMEM_LIMIT),
        name="moe_plan",
    )(ids)


def _dispatch_kernel(pos_ref, pad_ref, nt_ref, x_ref, xs_hbm, zeros, sems):
    tm = x_ref.shape[0] // ROW_TILES
    t = pos_ref.shape[0] // TOP_K
    base = pl.program_id(0) * tm
    first = pl.program_id(0) == 0
    tile_rows = MOE_TILE * ROW_TILES
    unused = [(j, pltpu.make_async_copy(zeros, xs_hbm.at[pl.ds(j * tile_rows, tile_rows), :], sems.at[1]))
              for j in range(xs_hbm.shape[0] // tile_rows)]

    @pl.when(first)
    def _():
        zeros[...] = jnp.zeros(zeros.shape, F32)
        fills = [pltpu.make_async_copy(zeros, xs_hbm.at[pl.ds(pad_ref[e] * ROW_TILES, tile_rows), :], sems.at[0])
                 for e in range(N_EXPERTS)]
        for f in fills:
            f.start()
        for f in fills:
            f.wait()
        for j, fill in unused:
            pl.when(j >= nt_ref[0])(fill.start)

    def body(j, _):
        for k in range(TOP_K):
            pltpu.make_async_copy(_token_tile(x_ref, j), _token_tile(xs_hbm, pos_ref[k * t + base + j]),
                                  sems.at[0]).start(priority=k)
        return 0

    lax.fori_loop(0, tm, body, 0, unroll=8)
    for k in range(TOP_K):
        pltpu.make_async_copy(x_ref, xs_hbm.at[pl.ds(0, tm * ROW_TILES), :], sems.at[0]).wait()

    @pl.when(first)
    def _():
        for j, fill in unused:
            pl.when(j >= nt_ref[0])(fill.wait)


def _dispatch(pos, first_pad, ntiles, x1t, tm=1024):
    t = x1t.shape[0] // ROW_TILES
    return pl.pallas_call(
        _dispatch_kernel,
        grid_spec=pltpu.PrefetchScalarGridSpec(
            num_scalar_prefetch=3, grid=(t // tm,),
            in_specs=[pl.BlockSpec((tm * ROW_TILES, LANES), lambda i, pos, pad, nt: (i, 0))],
            out_specs=pl.BlockSpec(memory_space=pl.ANY),
            scratch_shapes=[pltpu.VMEM((MOE_TILE * ROW_TILES, LANES), F32), pltpu.SemaphoreType.DMA((2,))]),
        out_shape=jax.ShapeDtypeStruct((_moe_rows(t) * ROW_TILES, LANES), F32),
        compiler_params=_cparams(("arbitrary",)),
        name="moe_dispatch",
    )(pos, first_pad, ntiles, x1t)


def _ffn_kernel(te_ref, nt_ref, xs_ref, w1_ref, w3_ref, w2_ref, ys_ref, w1b, w3b, w2b):
    j = pl.program_id(0)
    in_use = j < nt_ref[0]

    @pl.when(in_use & ((j == 0) | (te_ref[j] != te_ref[jnp.maximum(j - 1, 0)])))
    def _():
        w1b[...] = w1_ref[0].astype(BF16)
        w3b[...] = w3_ref[0].astype(BF16)
        w2b[...] = w2_ref[0].astype(BF16)

    @pl.when(in_use)
    def _():
        xb = _load_token_tiles(xs_ref, MOE_TILE).astype(BF16)
        h1 = jnp.dot(xb, w1b[...], preferred_element_type=F32)
        h3 = jnp.dot(xb, w3b[...], preferred_element_type=F32)
        h = (h1 * jax.nn.sigmoid(h1) * h3).astype(BF16)
        _store_token_tiles(ys_ref, jnp.dot(h, w2b[...], preferred_element_type=F32))

    @pl.when(jnp.logical_not(in_use))
    def _():
        ys_ref[...] = jnp.zeros(ys_ref.shape, F32)


def _ffn(tile_expert, ntiles, xs, layer, w1, w3, w2):
    ntile = xs.shape[0] // (MOE_TILE * ROW_TILES)
    tile = lambda j, te, nt: (jnp.minimum(j, nt[0] - 1), 0)
    expert = lambda j, te, nt: (layer, te[jnp.minimum(j, nt[0] - 1)], 0, 0)
    return pl.pallas_call(
        _ffn_kernel,
        grid_spec=pltpu.PrefetchScalarGridSpec(
            num_scalar_prefetch=2, grid=(ntile,),
            in_specs=[pl.BlockSpec((MOE_TILE * ROW_TILES, LANES), tile),
                      pl.BlockSpec((None, 1, D_MODEL, D_EXPERT), expert),
                      pl.BlockSpec((None, 1, D_MODEL, D_EXPERT), expert),
                      pl.BlockSpec((None, 1, D_EXPERT, D_MODEL), expert)],
            out_specs=pl.BlockSpec((MOE_TILE * ROW_TILES, LANES), lambda j, te, nt: (j, 0)),
            scratch_shapes=[pltpu.VMEM((D_MODEL, D_EXPERT), BF16), pltpu.VMEM((D_MODEL, D_EXPERT), BF16),
                            pltpu.VMEM((D_EXPERT, D_MODEL), BF16)]),
        out_shape=jax.ShapeDtypeStruct(xs.shape, F32),
        compiler_params=_cparams(("arbitrary",)),
        name="moe_ffn",
    )(tile_expert, ntiles, xs, w1, w3, w2)


def _combine_kernel(pos_ref, ys_hbm, x1t_ref, rc_ref, lg_ref, lb_ref, o_ref, ob_ref, gath, sems):
    tm = o_ref.shape[0]
    t = pos_ref.shape[0] // TOP_K
    i = pl.program_id(0)
    ntile = pl.num_programs(0)

    def issue(tile, slot):
        def body(j, _):
            for k in range(TOP_K):
                pltpu.make_async_copy(_token_tile(ys_hbm, pos_ref[k * t + tile * tm + j]),
                                      _token_tile(gath.at[slot, k], j), sems.at[slot]).start(priority=k)
            return 0
        lax.fori_loop(0, tm, body, 0, unroll=8)

    @pl.when(i == 0)
    def _():
        issue(0, 0)

    @pl.when(i + 1 < ntile)
    def _():
        issue(i + 1, (i + 1) % 2)

    slot = i % 2
    for k in range(TOP_K):
        pltpu.make_async_copy(ys_hbm.at[pl.ds(0, tm * ROW_TILES), :], gath.at[slot, k], sems.at[slot]).wait()
    rc = rc_ref[...]
    y = (rc[:, 0:1] * _load_token_tiles(gath.at[slot, 0], tm)
         + rc[:, 1:2] * _load_token_tiles(gath.at[slot, 1], tm))
    x2 = _layer_norm(DEEPNORM_ALPHA * _load_token_tiles(x1t_ref, tm) + y, lg_ref[...], lb_ref[...])
    o_ref[...] = x2
    ob_ref[...] = x2.astype(BF16)


def _combine(pos, ys, x1t, rcols, lg, lb, tm=512):
    t = x1t.shape[0] // ROW_TILES
    row = lambda w: pl.BlockSpec((tm, w), lambda i, pos: (i, 0))
    full = lambda shape: pl.BlockSpec(shape, lambda i, pos: (0,) * len(shape))
    return pl.pallas_call(
        _combine_kernel,
        grid_spec=pltpu.PrefetchScalarGridSpec(
            num_scalar_prefetch=1, grid=(t // tm,),
            in_specs=[pl.BlockSpec(memory_space=pl.ANY),
                      pl.BlockSpec((tm * ROW_TILES, LANES), lambda i, pos: (i, 0)), row(LANES),
                      full((1, D_MODEL)), full((1, D_MODEL))],
            out_specs=[row(D_MODEL), row(D_MODEL)],
            scratch_shapes=[pltpu.VMEM((2, TOP_K, tm * ROW_TILES, LANES), F32), pltpu.SemaphoreType.DMA((2,))]),
        out_shape=[jax.ShapeDtypeStruct((t, D_MODEL), F32), jax.ShapeDtypeStruct((t, D_MODEL), BF16)],
        compiler_params=_cparams(("arbitrary",)),
        name="moe_combine",
    )(pos, ys, x1t, rcols, lg, lb)


def _moe(x1t, rcols, ids, layer, w1, w3, w2, lg, lb):
    pos8, meta = _plan(ids)
    pos = pos8[:TOP_K].reshape(-1)
    nt_max = _moe_rows(x1t.shape[0] // ROW_TILES) // MOE_TILE
    ntiles = meta[1, :1]
    xs = _dispatch(pos, meta[2, :N_EXPERTS], ntiles, x1t)
    ys = _ffn(meta[0, :nt_max], ntiles, xs, layer, w1, w3, w2)
    return _combine(pos, ys, x1t, rcols, lg, lb)


def _split_hi_lo(w):
    hi = w.astype(BF16)
    return hi, (w - hi.astype(F32)).astype(BF16)


IN_AX, IN_AGATE, IN_BZ, IN_XBC = 0, D_MODEL, 2 * D_MODEL, 3 * D_MODEL
IN_DT = IN_XBC + SSM_CONV_CH
IN_QKV = IN_DT + SSM_HEADS
IN_F = IN_QKV + 3 * D_MODEL
IN_GATE = IN_F + ATTN_HEADS
P_IN = IN_GATE + 3 * D_MODEL
W_PREP_COLS = D_MODEL
W_PREP_SRC = (IN_QKV, IN_QKV + D_MODEL, IN_QKV + 2 * D_MODEL, IN_GATE, IN_GATE + D_MODEL, IN_GATE + 2 * D_MODEL,
              IN_XBC, IN_XBC + D_MODEL, IN_AX, IN_AGATE, IN_BZ)


def _w_prep_kernel(src_ref, wt_hbm, main_ref, small_ref, buf, sbuf, sems):
    layer = pl.program_id(0)
    j = pl.program_id(1)
    nblk = len(W_PREP_SRC)
    nstep = wt_hbm.shape[0] * nblk
    step = layer * nblk + j

    def block_copy(s):
        rows = pl.ds(pl.multiple_of(src_ref[s % nblk], SUBLANES), W_PREP_COLS)
        return pltpu.make_async_copy(wt_hbm.at[s // nblk, rows, :], buf.at[s % 2], sems.at[s % 2])

    @pl.when(step == 0)
    def _():
        block_copy(step).start()

    @pl.when(step + 1 < nstep)
    def _():
        block_copy(step + 1).start()

    @pl.when(j == 0)
    def _():
        sbuf[...] = jnp.zeros(sbuf.shape, F32)
        parts = [pltpu.make_async_copy(wt_hbm.at[layer, pl.ds(src, n), :], sbuf.at[pl.ds(dst, n), :], sems.at[2])
                 for src, dst, n in ((IN_DT, 0, SSM_HEADS), (IN_F, SSM_HEADS, ATTN_HEADS))]
        for c in parts:
            c.start()
        for c in parts:
            c.wait()
        small_ref[0] = sbuf[...].T.astype(BF16)

    block_copy(step).wait()
    scale = jnp.where(j == 0, ATTN_HEAD_DIM ** -0.5 * LOG2E, 1.0)
    main_ref[0] = (buf[step % 2].T * scale).astype(BF16)


def _w_prep(w_in):
    depth, k, n = w_in.shape
    assert n == P_IN and COL_QKV == 0 and len(W_PREP_SRC) * W_PREP_COLS == N_MAIN
    w_t = jnp.swapaxes(w_in, 1, 2)
    return pl.pallas_call(
        _w_prep_kernel,
        grid_spec=pltpu.PrefetchScalarGridSpec(
            num_scalar_prefetch=1, grid=(depth, len(W_PREP_SRC)),
            in_specs=[pl.BlockSpec(memory_space=pl.ANY)],
            out_specs=[pl.BlockSpec((1, k, W_PREP_COLS), lambda l, j, src: (l, 0, j)),
                       pl.BlockSpec((1, k, N_SMALL), lambda l, j, src: (l, 0, 0))],
            scratch_shapes=[pltpu.VMEM((2, W_PREP_COLS, k), F32), pltpu.VMEM((N_SMALL, k), F32),
                            pltpu.SemaphoreType.DMA((3,))]),
        out_shape=[jax.ShapeDtypeStruct((depth, k, N_MAIN), BF16), jax.ShapeDtypeStruct((depth, k, N_SMALL), BF16)],
        compiler_params=_cparams(("arbitrary", "arbitrary")),
        name="w_prep",
    )(jnp.asarray(W_PREP_SRC, jnp.int32), w_t)


def _prepare(w_in, gate_b, conv_a_w, conv_a_b, lru_wa, lru_ba, lru_wx, lru_bx, lru_lambda,
             conv_b_w, conv_b_b, dt_bias, a_log, d_skip, ssm_norm_w, forget_b,
             w_branch_a, w_branch_b, w_branch_c, w_out, ln1_g, ln1_b,
             router_w, router_b, w1, w3, w2, ln2_g, ln2_b):
    w_main, w_small = _w_prep(w_in)
    w_gates = jnp.concatenate([lru_wa, lru_wx], axis=-1).astype(BF16)
    pad_heads = lambda v, off: jnp.pad(v, ((0, 0), (off, N_SMALL - off - v.shape[1])))[:, None, :]
    dtb_p = pad_heads(dt_bias, 0)
    alog_p = pad_heads(a_log, 0)
    fb_p = pad_heads(forget_b, SSM_HEADS)
    dskip_e = jnp.repeat(d_skip, SSM_HEAD_DIM, axis=-1)[:, None, :]
    row = lambda v: v[:, None, :]
    wa_b, wb_b, wc_b, wo_b = (w.astype(BF16) for w in (w_branch_a, w_branch_b, w_branch_c, w_out))
    rwh, rwl = _split_hi_lo(router_w.T)
    return dict(
        w_main=w_main, w_small=w_small, conv_a_w=conv_a_w, conv_a_b=row(conv_a_b), w_gates=w_gates,
        lru_ba=row(lru_ba), lru_bx=row(lru_bx), lru_lambda=row(lru_lambda),
        conv_b_w=conv_b_w, conv_b_b=row(conv_b_b), dtb=dtb_p, alog=alog_p, dskip=dskip_e,
        ssm_norm_w=row(ssm_norm_w), fb=fb_p, wa=wa_b, wb=wb_b, wc=wc_b, wo=wo_b, gate_b=row(gate_b),
        ln1_g=row(ln1_g), ln1_b=row(ln1_b), rwh=rwh, rwl=rwl, rb=router_b[:, None],
        w1=w1, w3=w3, w2=w2, ln2_g=row(ln2_g), ln2_b=row(ln2_b))


def _layer(l, xf, xb, p, bsz, seq):
    proj = _matmul(xb, p['w_main'], l, BF16, 2048, 1024, "in_proj")
    small = _matmul(xb, p['w_small'], l, F32, 1024, N_SMALL, "in_proj_small")
    ha = _branch_a(proj, p['conv_a_w'][l], p['conv_a_b'][l], p['w_gates'][l], p['lru_ba'][l], p['lru_bx'][l],
                   p['lru_lambda'][l], bsz, seq)
    hb = _branch_b(proj, small, p['conv_b_w'][l], p['conv_b_b'][l], p['dtb'][l], p['alog'][l], p['dskip'][l],
                   p['ssm_norm_w'][l], bsz, seq)
    cum = _fox_cum(small, p['fb'][l], bsz, seq)
    hc = _branch_c(proj, cum, bsz, seq)
    x1, rcols, ids = _merge(ha, hb, hc, proj, xf, p['wa'][l], p['wb'][l], p['wc'][l], p['wo'][l],
                            p['gate_b'][l], p['ln1_g'][l], p['ln1_b'][l], p['rwh'], p['rwl'], p['rb'])
    x2, x2b = _moe(x1, rcols, ids, l, p['w1'], p['w3'], p['w2'], p['ln2_g'][l], p['ln2_b'][l])
    return dict(proj=proj, small=small, ha=ha, hb=hb, cum=cum, hc=hc, x1=x1, rcols=rcols, ids=ids,
                x2=x2, x2b=x2b)


def kernel(x, w_in, gate_b, conv_a_w, conv_a_b, lru_wa, lru_ba, lru_wx, lru_bx, lru_lambda,
           conv_b_w, conv_b_b, dt_bias, a_log, d_skip, ssm_norm_w, forget_b,
           w_branch_a, w_branch_b, w_branch_c, w_out, ln1_g, ln1_b,
           router_w, router_b, w1, w3, w2, ln2_g, ln2_b):
    bsz, seq, d = x.shape
    p = _prepare(w_in, gate_b, conv_a_w, conv_a_b, lru_wa, lru_ba, lru_wx, lru_bx, lru_lambda,
                 conv_b_w, conv_b_b, dt_bias, a_log, d_skip, ssm_norm_w, forget_b,
                 w_branch_a, w_branch_b, w_branch_c, w_out, ln1_g, ln1_b,
                 router_w, router_b, w1, w3, w2, ln2_g, ln2_b)
    xf = x.reshape(bsz * seq, d)
    xb = xf.astype(BF16)
    for l in range(w_in.shape[0]):
        stages = _layer(l, xf, xb, p, bsz, seq)
        xf, xb = stages['x2'], stages['x2b']
    return xf.reshape(bsz, seq, d)
```

```python
import functools

import jax
import jax.numpy as jnp
from jax import lax
from jax.experimental import pallas as pl
from jax.experimental.pallas import tpu as pltpu

F32 = jnp.float32
BF16 = jnp.bfloat16

D_MODEL = 1024
DEPTH = 4
RNN_HEADS = 8
RNN_BLOCK = 128
CONV_WIDTH = 4
LRU_C = 8.0
SSM_HEADS = 16
SSM_HEAD_DIM = 64
SSM_GROUPS = 4
SSM_STATE = 128
SSM_CHUNK = 128
SSM_CONV_CH = 2048
ATTN_HEADS = 16
ATTN_HEAD_DIM = 64
N_EXPERTS = 16
EXPERTS_PER_GROUP = 4
D_EXPERT = 512
LN_EPS = 1e-5
RMS_EPS = 1e-6
DEEPNORM_ALPHA = (2 * DEPTH) ** 0.25

LANES = 128
SUBLANES = 8
VMEM_LIMIT = 48 * 1024 * 1024

COL_QKV = 0
COL_GATE = 3072
COL_XBC = 6144
COL_AX = 8192
COL_AGATE = 9216
COL_BZ = 10240
N_MAIN = 11264
N_SMALL = 128


def _cparams(sem):
    return pltpu.CompilerParams(dimension_semantics=sem, vmem_limit_bytes=VMEM_LIMIT)


def _in_proj_kernel(x_ref, w_ref, ws_ref, o_ref, os_ref):
    x = x_ref[...]
    o_ref[...] = jnp.dot(x, w_ref[...], preferred_element_type=F32).astype(o_ref.dtype)

    @pl.when(pl.program_id(1) == 0)
    def _():
        os_ref[...] = jnp.dot(x, ws_ref[...], preferred_element_type=F32)


def _in_proj(x, w_main, w_small, layer, tm=2048, tn=1024):
    m, k = x.shape
    return pl.pallas_call(
        _in_proj_kernel,
        grid=(m // tm, N_MAIN // tn),
        in_specs=[pl.BlockSpec((tm, k), lambda i, j: (i, 0)),
                  pl.BlockSpec((None, k, tn), lambda i, j: (layer, 0, j)),
                  pl.BlockSpec((None, k, N_SMALL), lambda i, j: (layer, 0, 0))],
        out_specs=[pl.BlockSpec((tm, tn), lambda i, j: (i, j)),
                   pl.BlockSpec((tm, N_SMALL), lambda i, j: (i, 0))],
        out_shape=[jax.ShapeDtypeStruct((m, N_MAIN), BF16), jax.ShapeDtypeStruct((m, N_SMALL), F32)],
        compiler_params=_cparams(("parallel", "arbitrary")),
        name="in_proj",
    )(x, w_main, w_small)


CONV_BAND = 128
BF16_ROWS = 16


def _causal_conv(x, hist, cw_ref, cb_ref, first):
    ts, ch = x.shape

    @pl.when(first)
    def _():
        hist[...] = jnp.zeros(hist.shape, BF16)

    ext = jnp.concatenate([hist[...], x], axis=0)
    hist[...] = x[ts - BF16_ROWS:ts, :]
    taps = CONV_WIDTH - 1
    out_row = lax.broadcasted_iota(jnp.int32, (taps * CONV_BAND, CONV_BAND + BF16_ROWS), 0)
    in_row = lax.broadcasted_iota(jnp.int32, (taps * CONV_BAND, CONV_BAND + BF16_ROWS), 1)
    tap = out_row // CONV_BAND
    shift = jnp.where(in_row == (out_row - tap * CONV_BAND) + BF16_ROWS - taps + tap, 1.0, 0.0).astype(BF16)
    bands = []
    for b0 in range(0, ts, CONV_BAND):
        shifted = jnp.dot(shift, ext[b0:b0 + CONV_BAND + BF16_ROWS, :], preferred_element_type=F32)
        y = cb_ref[...] + cw_ref[taps:taps + 1, :] * x[b0:b0 + CONV_BAND, :].astype(F32)
        for k in range(taps):
            y = y + cw_ref[k:k + 1, :] * shifted[k * CONV_BAND:(k + 1) * CONV_BAND, :]
        bands.append(y)
    return bands[0] if len(bands) == 1 else jnp.concatenate(bands, axis=0)


def _bf16_pieces(x, parts):
    pieces = []
    rest = x
    for _ in range(parts):
        piece = rest.astype(BF16)
        pieces.append(piece)
        rest = rest - piece.astype(F32)
    return pieces


def _select_dot(x, w, parts):
    pieces = _bf16_pieces(x, parts)
    return jnp.dot(jnp.concatenate(pieces, axis=1), jnp.concatenate([w] * parts, axis=0),
                   preferred_element_type=F32)


def _cumsum_rows(x, parts):
    n = x.shape[0]
    ri = lax.broadcasted_iota(jnp.int32, (n, n), 0)
    ci = lax.broadcasted_iota(jnp.int32, (n, n), 1)
    tril = jnp.where(ri >= ci, 1.0, 0.0).astype(BF16)
    pieces = _bf16_pieces(x, parts)
    return jnp.dot(jnp.concatenate([tril] * parts, axis=1), jnp.concatenate(pieces, axis=0),
                   preferred_element_type=F32)


ROW_TILES = D_MODEL // LANES


def _store_token_tiles(ref, x):
    rows = x.shape[0]
    for s in range(ROW_TILES):
        ref[pl.ds(s, rows, stride=ROW_TILES), :] = x[:, s * LANES:(s + 1) * LANES]


def _load_token_tiles(ref, rows):
    return jnp.concatenate([ref[pl.ds(s, rows, stride=ROW_TILES), :] for s in range(ROW_TILES)], axis=1)


def _token_tile(ref, row):
    return ref.at[pl.ds(pl.multiple_of(row * ROW_TILES, ROW_TILES), ROW_TILES), :]


def _rglru_kernel(x_ref, g_ref, cw_ref, cb_ref, wg_ref, ba_ref, bx_ref, lam_ref, o_ref, hist, hcar):
    s = pl.program_id(1)
    ts = x_ref.shape[0]
    first = s == 0

    @pl.when(first)
    def _():
        hcar[...] = jnp.zeros(hcar.shape, F32)

    xa = _causal_conv(x_ref[...], hist, cw_ref, cb_ref, first)
    xab = xa.astype(BF16)
    r_parts, i_parts = [], []
    for h in range(RNN_HEADS):
        pre = jnp.dot(xab[:, h * RNN_BLOCK:(h + 1) * RNN_BLOCK], wg_ref[h], preferred_element_type=F32)
        r_parts.append(pre[:, :RNN_BLOCK])
        i_parts.append(pre[:, RNN_BLOCK:])
    r_gate = jax.nn.sigmoid(jnp.concatenate(r_parts, axis=1) + ba_ref[...])
    i_gate = jax.nn.sigmoid(jnp.concatenate(i_parts, axis=1) + bx_ref[...])
    log_a = (-LRU_C) * r_gate * jax.nn.softplus(-lam_ref[...])
    a = jnp.exp(log_a)
    one_minus_a2 = 1.0 - jnp.exp(2.0 * log_a)
    mult = one_minus_a2 * lax.rsqrt(jnp.maximum(one_minus_a2, 1e-30))
    u = (xa * i_gate) * mult

    ng = ts // SUBLANES
    a3 = a.reshape(ng, SUBLANES, D_MODEL)
    b3 = u.reshape(ng, SUBLANES, D_MODEL)
    row = lax.broadcasted_iota(jnp.int32, a3.shape, 1)
    d = 1
    while d < SUBLANES:
        valid = row >= d
        a_s = jnp.where(valid, pltpu.roll(a3, d, axis=1), 1.0)
        b_s = jnp.where(valid, pltpu.roll(b3, d, axis=1), 0.0)
        b3 = a3 * b_s + b3
        a3 = a3 * a_s
        d *= 2
    h_in = hcar[SUBLANES - 1:SUBLANES, :]
    groups = []
    for gi in range(ng):
        hg = b3[gi] + a3[gi] * h_in
        groups.append(hg)
        h_in = hg[SUBLANES - 1:SUBLANES, :]
    h = jnp.concatenate(groups, axis=0)
    hcar[...] = groups[-1]
    o_ref[...] = (h * jax.nn.gelu(g_ref[...].astype(F32))).astype(o_ref.dtype)


def _branch_a(proj, cw, cb, wg, ba, bx, lam, bsz, seq, ts=256):
    nst = seq // ts
    full = lambda shape: pl.BlockSpec(shape, lambda b, s: (0,) * len(shape))
    return pl.pallas_call(
        _rglru_kernel,
        grid=(bsz, nst),
        in_specs=[pl.BlockSpec((ts, D_MODEL), lambda b, s: (b * nst + s, COL_AX // D_MODEL)),
                  pl.BlockSpec((ts, D_MODEL), lambda b, s: (b * nst + s, COL_AGATE // D_MODEL)),
                  full((CONV_WIDTH, D_MODEL)), full((1, D_MODEL)),
                  full((RNN_HEADS, RNN_BLOCK, 2 * RNN_BLOCK)),
                  full((1, D_MODEL)), full((1, D_MODEL)), full((1, D_MODEL))],
        out_specs=pl.BlockSpec((ts, D_MODEL), lambda b, s: (b * nst + s, 0)),
        out_shape=jax.ShapeDtypeStruct((bsz * seq, D_MODEL), BF16),
        scratch_shapes=[pltpu.VMEM((BF16_ROWS, D_MODEL), BF16), pltpu.VMEM((SUBLANES, D_MODEL), F32)],
        compiler_params=_cparams(("parallel", "arbitrary")),
        name="rglru",
    )(proj, proj, cw, cb, wg, ba, bx, lam)


def _ssd_kernel(z_ref, xbc_ref, dtf_ref, cw_ref, cb_ref, dtb_ref, alog_ref, dskip_ref, nw_ref,
                o_ref, hist, state):
    first = pl.program_id(1) == 0

    @pl.when(first)
    def _():
        state[...] = jnp.zeros(state.shape, F32)

    conv = _causal_conv(xbc_ref[...], hist, cw_ref, cb_ref, first)
    for r0 in range(0, conv.shape[0], SSM_CHUNK):
        rows = slice(r0, r0 + SSM_CHUNK)
        o_ref[rows, :] = _ssd_chunk(conv[rows], dtf_ref[rows, :], z_ref[rows, :].astype(F32),
                                    dtb_ref, alog_ref, dskip_ref, nw_ref, state).astype(o_ref.dtype)


def _ssd_chunk(conv, dtf, z, dtb_ref, alog_ref, dskip_ref, nw_ref, state):
    L = SSM_CHUNK
    act = conv * jax.nn.sigmoid(conv)
    xs = act[:, :D_MODEL]
    bm = act[:, D_MODEL:D_MODEL + SSM_GROUPS * SSM_STATE]
    cm = act[:, D_MODEL + SSM_GROUPS * SSM_STATE:]

    lane = lax.broadcasted_iota(jnp.int32, (L, LANES), 1)
    head_lane = lane < SSM_HEADS
    dt = jnp.where(head_lane, jax.nn.softplus(dtf + dtb_ref[...]), 0.0)
    a_dt = dt * (-jnp.exp(alog_ref[...]))
    ri = lax.broadcasted_iota(jnp.int32, (L, L), 0)
    ci = lax.broadcasted_iota(jnp.int32, (L, L), 1)
    causal = ri >= ci
    cs = _cumsum_rows(a_dt, 3)
    cs_t = cs.T
    tot = cs[L - 1:L, :]
    dstate = jnp.exp(tot - cs)
    exp_cs = jnp.exp(cs)

    er = lax.broadcasted_iota(jnp.int32, (LANES, D_MODEL), 0)
    ec = lax.broadcasted_iota(jnp.int32, (LANES, D_MODEL), 1)
    expand = jnp.where(ec // SSM_HEAD_DIM == er, 1.0, 0.0).astype(BF16)
    dt_e = _select_dot(dt, expand, 2)
    dtds_e = _select_dot(dt * dstate, expand, 2)
    tot_e = _select_dot(jnp.broadcast_to(jnp.exp(tot), (SUBLANES, LANES)), expand, 3)[0:1, :]
    xdt = xs * dt_e
    xdt_end = (xs * dtds_e).astype(BF16)

    lo_half = lax.broadcasted_iota(jnp.int32, (2 * L, LANES), 1) < SSM_HEAD_DIM
    heads_per_group = SSM_HEADS // SSM_GROUPS
    y_parts = []
    new_states = []
    for g in range(SSM_GROUPS):
        bg = bm[:, g * SSM_STATE:(g + 1) * SSM_STATE]
        cg = cm[:, g * SSM_STATE:(g + 1) * SSM_STATE]
        cb = lax.dot_general(cg.astype(BF16), bg.astype(BF16), (((1,), (1,)), ((), ())),
                             preferred_element_type=F32)
        lhs = []
        for e in range(heads_per_group):
            hd = g * heads_per_group + e
            colb = jnp.broadcast_to(cs[:, hd:hd + 1], (L, L))
            rowb = jnp.broadcast_to(cs_t[hd:hd + 1, :], (L, L))
            decay = jnp.exp(jnp.where(causal, colb - rowb, -jnp.inf))
            m = (cb * decay).astype(BF16)
            c_off = (cg * jnp.broadcast_to(exp_cs[:, hd:hd + 1], (L, L))).astype(BF16)
            lhs.append(jnp.concatenate([m, c_off], axis=1))
        for j in range(heads_per_group // 2):
            col = (g * heads_per_group + 2 * j) * SSM_HEAD_DIM
            rhs = jnp.concatenate([xdt[:, col:col + LANES], state[:, col:col + LANES]], axis=0).astype(BF16)
            zero = jnp.zeros_like(rhs)
            y_parts.append(jnp.dot(lhs[2 * j], jnp.where(lo_half, rhs, zero), preferred_element_type=F32)
                           + jnp.dot(lhs[2 * j + 1], jnp.where(lo_half, zero, rhs), preferred_element_type=F32))
        gw = heads_per_group * SSM_HEAD_DIM
        new_states.append(jnp.dot(bg.T.astype(BF16), xdt_end[:, g * gw:(g + 1) * gw],
                                  preferred_element_type=F32))
    y = jnp.concatenate(y_parts, axis=1)
    state[...] = state[...] * tot_e + jnp.concatenate(new_states, axis=1)

    y = y + xs * dskip_ref[...]
    gy = y * (z * jax.nn.sigmoid(z))
    gw = D_MODEL // SSM_GROUPS
    outs = []
    for g in range(SSM_GROUPS):
        gg = gy[:, g * gw:(g + 1) * gw]
        ms = jnp.mean(gg * gg, axis=-1, keepdims=True)
        outs.append(gg * lax.rsqrt(ms + RMS_EPS))
    return jnp.concatenate(outs, axis=1) * nw_ref[...]


def _branch_b(proj, small, cw, cb, dtb, alog, dskip_e, nw, bsz, seq, chunks_per_step=4):
    L = chunks_per_step * SSM_CHUNK
    nc = seq // L
    full = lambda shape: pl.BlockSpec(shape, lambda b, c: (0,) * len(shape))
    return pl.pallas_call(
        _ssd_kernel,
        grid=(bsz, nc),
        in_specs=[pl.BlockSpec((L, D_MODEL), lambda b, c: (b * nc + c, COL_BZ // D_MODEL)),
                  pl.BlockSpec((L, SSM_CONV_CH), lambda b, c: (b * nc + c, COL_XBC // SSM_CONV_CH)),
                  pl.BlockSpec((L, N_SMALL), lambda b, c: (b * nc + c, 0)),
                  full((CONV_WIDTH, SSM_CONV_CH)), full((1, SSM_CONV_CH)),
                  full((1, N_SMALL)), full((1, N_SMALL)), full((1, D_MODEL)), full((1, D_MODEL))],
        out_specs=pl.BlockSpec((L, D_MODEL), lambda b, c: (b * nc + c, 0)),
        out_shape=jax.ShapeDtypeStruct((bsz * seq, D_MODEL), BF16),
        scratch_shapes=[pltpu.VMEM((BF16_ROWS, SSM_CONV_CH), BF16), pltpu.VMEM((SSM_STATE, D_MODEL), F32)],
        compiler_params=_cparams(("parallel", "arbitrary")),
        name="ssd",
    )(proj, proj, small, cw, cb, dtb, alog, dskip_e, nw)


CUM_BLOCK = 256
LOG2E = 1.4426950408889634
BIAS_PARTS = 3


def _fox_cum_kernel(dtf_ref, fb_ref, o_ref):
    seq = dtf_ref.shape[0]
    npair = ATTN_HEADS // 2
    hd = ATTN_HEAD_DIM
    lane = lax.broadcasted_iota(jnp.int32, (CUM_BLOCK, LANES), 1)
    live = (lane >= SSM_HEADS) & (lane < SSM_HEADS + ATTN_HEADS)
    sr = lax.broadcasted_iota(jnp.int32, (BIAS_PARTS * LANES, npair * LANES), 0)
    sc = lax.broadcasted_iota(jnp.int32, (BIAS_PARTS * LANES, npair * LANES), 1)
    piece, src = sr // LANES, sr % LANES
    head0 = SSM_HEADS + 2 * (sc // LANES)
    dst = sc % LANES
    place = jnp.where(((src == head0) & (dst == hd + piece)) | ((src == head0 + 1) & (dst == piece)),
                      1.0, 0.0).astype(BF16)
    half = lax.broadcasted_iota(jnp.int32, (CUM_BLOCK, npair * LANES), 1) % hd
    ones = (half >= BIAS_PARTS) & (half < 2 * BIAS_PARTS)
    carry = jnp.zeros((1, LANES), F32)
    for i in range(seq // CUM_BLOCK):
        rows = slice(i * CUM_BLOCK, (i + 1) * CUM_BLOCK)
        logf = jnp.where(live, jax.nn.log_sigmoid(dtf_ref[rows, :] + fb_ref[...]), 0.0)
        cb = _cumsum_rows(logf, 3) + carry
        carry = cb[CUM_BLOCK - 1:CUM_BLOCK, :]
        pieces = jnp.concatenate(_bf16_pieces(cb * (-LOG2E), BIAS_PARTS), axis=1)
        aug = jnp.dot(pieces, place, preferred_element_type=F32)
        o_ref[rows, :] = jnp.where(ones, 1.0, aug).astype(BF16)


def _fox_cum(small, fb, bsz, seq):
    width = (ATTN_HEADS // 2) * LANES
    return pl.pallas_call(
        _fox_cum_kernel,
        grid=(bsz,),
        in_specs=[pl.BlockSpec((seq, N_SMALL), lambda b: (b, 0)),
                  pl.BlockSpec((1, N_SMALL), lambda b: (0, 0))],
        out_specs=pl.BlockSpec((seq, width), lambda b: (b, 0)),
        out_shape=jax.ShapeDtypeStruct((bsz * seq, width), BF16),
        compiler_params=_cparams(("parallel",)),
        name="fox_cum",
    )(small, fb)


def _fox_attn_kernel(q_ref, k_ref, v_ref, aug_ref, o_ref, k0_s, k1_s, *, tq):
    seq = q_ref.shape[0]
    hd = ATTN_HEAD_DIM
    lane = lax.broadcasted_iota(jnp.int32, (seq, LANES), 1)
    lo_half = lane < hd
    aug_k = aug_ref[...]
    k = k_ref[...]
    k0_s[...] = jnp.where(lo_half, k, aug_k)
    k1_s[...] = jnp.where(lo_half, aug_k, k)

    lane_q = lax.broadcasted_iota(jnp.int32, (tq, LANES), 1)
    lo_half_q = lane_q < hd
    half_q = lane_q % hd
    q_ones = half_q < BIAS_PARTS
    q_const = (half_q >= BIAS_PARTS) & (half_q < 2 * BIAS_PARTS)
    tri = lax.broadcasted_iota(jnp.int32, (tq, tq), 1) <= lax.broadcasted_iota(jnp.int32, (tq, tq), 0)
    nt = (((1,), (1,)), ((), ()))

    for qi in reversed(range(seq // tq)):
        q0 = qi * tq
        q = q_ref[q0:q0 + tq, :]
        row0 = aug_ref[q0:q0 + BF16_ROWS, :][0:1, :].astype(F32)
        c_row = jnp.broadcast_to(pltpu.roll(-row0, BIAS_PARTS, axis=1), (tq, LANES))
        q_aug = jnp.where(q_ones, 1.0, jnp.where(q_const, c_row, 0.0)).astype(BF16)
        q_heads = (jnp.where(lo_half_q, q, q_aug), jnp.where(lo_half_q, q_aug, q))
        outs = []
        for hh, k_s in enumerate((k0_s, k1_s)):
            s_diag = lax.dot_general(q_heads[hh], k_s[q0:q0 + tq, :], nt, preferred_element_type=F32)
            s_diag = jnp.where(tri, s_diag, -jnp.inf)
            m = jnp.max(s_diag, axis=-1, keepdims=True)
            if qi > 0:
                s_off = lax.dot_general(q_heads[hh], k_s[0:q0, :], nt, preferred_element_type=F32)
                m = jnp.maximum(m, jnp.max(s_off, axis=-1, keepdims=True))
            p_diag = jnp.exp2(s_diag - m)
            l = jnp.sum(p_diag, axis=-1, keepdims=True)
            acc = jnp.dot(p_diag.astype(BF16), v_ref[q0:q0 + tq, :], preferred_element_type=F32)
            if qi > 0:
                p_off = jnp.exp2(s_off - m)
                l = l + jnp.sum(p_off, axis=-1, keepdims=True)
                acc = acc + jnp.dot(p_off.astype(BF16), v_ref[0:q0, :], preferred_element_type=F32)
            outs.append(acc * (1.0 / l))
        o_ref[q0:q0 + tq, :] = jnp.where(lo_half_q, outs[0], outs[1]).astype(o_ref.dtype)


def _branch_c(proj, cum, bsz, seq, tq=512):
    npair = ATTN_HEADS // 2
    kv_scratch = pltpu.VMEM((seq, LANES), BF16)
    return pl.pallas_call(
        functools.partial(_fox_attn_kernel, tq=tq),
        grid=(bsz, npair),
        in_specs=[pl.BlockSpec((seq, LANES), lambda b, p: (b, COL_QKV // LANES + p)),
                  pl.BlockSpec((seq, LANES), lambda b, p: (b, COL_QKV // LANES + npair + p)),
                  pl.BlockSpec((seq, LANES), lambda b, p: (b, COL_QKV // LANES + 2 * npair + p)),
                  pl.BlockSpec((seq, LANES), lambda b, p: (b, p))],
        out_specs=pl.BlockSpec((seq, LANES), lambda b, p: (b, p)),
        out_shape=jax.ShapeDtypeStruct((bsz * seq, D_MODEL), BF16),
        scratch_shapes=[kv_scratch, kv_scratch],
        compiler_params=_cparams(("parallel", "parallel")),
        name="fox_attn",
    )(proj, proj, proj, cum)


def _layer_norm(x, g, b):
    mu = jnp.mean(x, axis=-1, keepdims=True)
    xc = x - mu
    var = jnp.mean(xc * xc, axis=-1, keepdims=True)
    return xc * lax.rsqrt(var + LN_EPS) * g + b


def _top2_sum(a, b, c, d):
    hi1, lo1 = jnp.maximum(a, b), jnp.minimum(a, b)
    hi2, lo2 = jnp.maximum(c, d), jnp.minimum(c, d)
    return jnp.maximum(hi1, hi2) + jnp.maximum(jnp.minimum(hi1, hi2), jnp.maximum(lo1, lo2))


def _route_rows(logits_t):
    rows = [logits_t[e:e + 1, :] for e in range(N_EXPERTS)]
    mx = functools.reduce(jnp.maximum, rows)
    ex = [jnp.exp(r - mx) for r in rows]
    den = functools.reduce(jnp.add, ex)
    probs = [e / den for e in ex]
    ngroups = N_EXPERTS // EXPERTS_PER_GROUP
    scores = [_top2_sum(*probs[EXPERTS_PER_GROUP * g:EXPERTS_PER_GROUP * (g + 1)]) for g in range(ngroups)]
    best_g = jnp.zeros_like(mx, dtype=jnp.int32)
    best_s = scores[0]
    for g in range(1, ngroups):
        better = scores[g] > best_s
        best_g = jnp.where(better, g, best_g)
        best_s = jnp.where(better, scores[g], best_s)
    masked = [jnp.where(best_g == e // EXPERTS_PER_GROUP, probs[e], -1.0) for e in range(N_EXPERTS)]
    v1, i1 = masked[0], jnp.zeros_like(best_g)
    for e in range(1, N_EXPERTS):
        better = masked[e] > v1
        i1 = jnp.where(better, e, i1)
        v1 = jnp.where(better, masked[e], v1)
    v2, i2 = jnp.full_like(v1, -2.0), jnp.zeros_like(best_g)
    for e in range(N_EXPERTS):
        better = (masked[e] > v2) & (i1 != e)
        i2 = jnp.where(better, e, i2)
        v2 = jnp.where(better, masked[e], v2)
    tot = v1 + v2
    return i1, i2, v1 / tot, v2 / tot


def _merge_kernel(ha_ref, hb_ref, hc_ref, gate_ref, x_ref, wa_ref, wb_ref, wc_ref, wo_ref, gb_ref,
                  lg_ref, lb_ref, rwh_ref, rwl_ref, rb_ref,
                  x1t_ref, rcols_ref, ids_ref):
    tm = x_ref.shape[0]
    ya = jnp.dot(ha_ref[...], wa_ref[...], preferred_element_type=F32)
    yb = jnp.dot(hb_ref[...], wb_ref[...], preferred_element_type=F32)
    yc = jnp.dot(hc_ref[...], wc_ref[...], preferred_element_type=F32)
    g = jax.nn.sigmoid(gate_ref[...].astype(F32) + gb_ref[...])
    mixed_in = (g[:, :D_MODEL] * ya + g[:, D_MODEL:2 * D_MODEL] * yb + g[:, 2 * D_MODEL:] * yc).astype(BF16)
    mixed = jnp.dot(mixed_in, wo_ref[...], preferred_element_type=F32)
    x1 = _layer_norm(DEEPNORM_ALPHA * x_ref[...] + mixed, lg_ref[...], lb_ref[...])
    _store_token_tiles(x1t_ref, x1)
    x1h = x1.astype(BF16)
    x1l = (x1 - x1h.astype(F32)).astype(BF16)
    nt = (((1,), (1,)), ((), ()))
    logits_t = (lax.dot_general(rwh_ref[...], x1h, nt, preferred_element_type=F32)
                + lax.dot_general(rwl_ref[...], x1h, nt, preferred_element_type=F32)
                + lax.dot_general(rwh_ref[...], x1l, nt, preferred_element_type=F32)
                + rb_ref[...])
    i1, i2, w1, w2 = _route_rows(logits_t)
    sub = lax.broadcasted_iota(jnp.int32, (SUBLANES, tm), 0)
    ids_ref[...] = jnp.where(sub == 0, i1, jnp.where(sub == 1, i2, 0))
    wrows = jnp.where(sub == 0, w1, jnp.where(sub == 1, w2, 0.0))
    wrows = jnp.concatenate([wrows, jnp.zeros((LANES - SUBLANES, tm), F32)], axis=0)
    rcols_ref[...] = wrows.T


def _merge(ha, hb, hc, proj, x, wa, wb, wc, wo, gb, lg, lb, rwh, rwl, rb, tm=512):
    t = x.shape[0]
    full = lambda shape: pl.BlockSpec(shape, lambda i: (0,) * len(shape))
    row = lambda w: pl.BlockSpec((tm, w), lambda i: (i, 0))
    return pl.pallas_call(
        _merge_kernel,
        grid=(t // tm,),
        in_specs=[row(D_MODEL), row(D_MODEL), row(D_MODEL),
                  pl.BlockSpec((tm, 3 * D_MODEL), lambda i: (i, COL_GATE // (3 * D_MODEL))),
                  row(D_MODEL),
                  full((D_MODEL, D_MODEL)), full((D_MODEL, D_MODEL)), full((D_MODEL, D_MODEL)),
                  full((D_MODEL, D_MODEL)), full((1, 3 * D_MODEL)),
                  full((1, D_MODEL)), full((1, D_MODEL)),
                  full((N_EXPERTS, D_MODEL)), full((N_EXPERTS, D_MODEL)), full((N_EXPERTS, 1))],
        out_specs=[pl.BlockSpec((tm * ROW_TILES, LANES), lambda i: (i, 0)), row(LANES),
                   pl.BlockSpec((SUBLANES, tm), lambda i: (0, i))],
        out_shape=[jax.ShapeDtypeStruct((t * ROW_TILES, LANES), F32), jax.ShapeDtypeStruct((t, LANES), F32),
                   jax.ShapeDtypeStruct((SUBLANES, t), jnp.int32)],
        compiler_params=_cparams(("parallel",)),
        name="merge",
    )(ha, hb, hc, proj, x, wa, wb, wc, wo, gb, lg, lb, rwh, rwl, rb)


TOP_K = 2
MOE_TILE = 512
MOE_TILE_SHIFT = 9
PLAN_BLOCK = 256


def _moe_rows(t):
    return TOP_K * t + (N_EXPERTS + 1) * MOE_TILE


def _plan_kernel(ids_ref, pos_ref, meta_ref, cnt_s):
    t = ids_ref.shape[1]
    nblk = t // PLAN_BLOCK
    sub_e = lax.broadcasted_iota(jnp.int32, (N_EXPERTS, PLAN_BLOCK), 0)
    ur = lax.broadcasted_iota(jnp.int32, (PLAN_BLOCK, PLAN_BLOCK), 0)
    uc = lax.broadcasted_iota(jnp.int32, (PLAN_BLOCK, PLAN_BLOCK), 1)
    before = jnp.where(ur < uc, 1.0, 0.0).astype(BF16)

    def one_hots(c):
        cols = pl.ds(pl.multiple_of(c * PLAN_BLOCK, PLAN_BLOCK), PLAN_BLOCK)
        ids = ids_ref[:, cols]
        oh0 = jnp.where(ids[0:1, :] == sub_e, 1.0, 0.0)
        oh1 = jnp.where(ids[1:2, :] == sub_e, 1.0, 0.0)
        return cols, oh0, oh1

    def count_block(c, carry):
        cols, oh0, oh1 = one_hots(c)
        oh = oh0 + oh1
        cnt_s[:, cols] = jnp.dot(oh.astype(BF16), before, preferred_element_type=F32) + carry
        return carry + jnp.sum(oh, axis=1, keepdims=True)

    counts = lax.fori_loop(0, nblk, count_block, jnp.zeros((N_EXPERTS, 1), F32))
    padded = ((counts.astype(jnp.int32) + (MOE_TILE - 1)) >> MOE_TILE_SHIFT) << MOE_TILE_SHIFT
    padded = jnp.broadcast_to(padded, (N_EXPERTS, LANES))
    sub = lax.broadcasted_iota(jnp.int32, (N_EXPERTS, LANES), 0)
    lane = lax.broadcasted_iota(jnp.int32, (N_EXPERTS, LANES), 1)
    start = jnp.zeros((N_EXPERTS, LANES), jnp.int32)
    run = jnp.zeros((1, LANES), jnp.int32)
    for e in range(N_EXPERTS):
        start = jnp.where(sub == e, run, start)
        run = run + padded[e:e + 1, :]
    ended = jnp.where(start + padded <= lane * MOE_TILE, 1, 0)
    tile_expert = jnp.minimum(jnp.sum(ended, axis=0, keepdims=True), N_EXPERTS - 1)
    first_pad = start + jnp.broadcast_to(counts.astype(jnp.int32), (N_EXPERTS, LANES))
    first_pad = jnp.sum(jnp.where(sub == lane, first_pad, 0), axis=0, keepdims=True)
    sub8 = lax.broadcasted_iota(jnp.int32, (SUBLANES, LANES), 0)
    meta_ref[...] = jnp.where(sub8 == 0, tile_expert,
                              jnp.where(sub8 == 1, run >> MOE_TILE_SHIFT, jnp.where(sub8 == 2, first_pad, 0)))

    start_f = start[:, 0:1].astype(F32)
    sub8b = lax.broadcasted_iota(jnp.int32, (SUBLANES, PLAN_BLOCK), 0)

    def place_block(c, _):
        cols, oh0, oh1 = one_hots(c)
        base = cnt_s[:, cols] + start_f
        p0 = jnp.sum(oh0 * base, axis=0, keepdims=True).astype(jnp.int32)
        p1 = jnp.sum(oh1 * base, axis=0, keepdims=True).astype(jnp.int32)
        pos_ref[:, cols] = jnp.where(sub8b == 0, p0, jnp.where(sub8b == 1, p1, 0))
        return 0

    lax.fori_loop(0, nblk, place_block, 0)


def _plan(ids):
    t = ids.shape[1]
    return pl.pallas_call(
        _plan_kernel,
        out_shape=[jax.ShapeDtypeStruct((SUBLANES, t), jnp.int32),
                   jax.ShapeDtypeStruct((SUBLANES, LANES), jnp.int32)],
        scratch_shapes=[pltpu.VMEM((N_EXPERTS, t), F32)],
        compiler_params=pltpu.CompilerParams(vmem_limit_bytes=VMEM_LIMIT),
        name="moe_plan",
    )(ids)


def _dispatch_kernel(pos_ref, pad_ref, nt_ref, x_ref, xs_hbm, zeros, sems):
    tm = x_ref.shape[0] // ROW_TILES
    t = pos_ref.shape[0] // TOP_K
    base = pl.program_id(0) * tm
    first = pl.program_id(0) == 0
    tile_rows = MOE_TILE * ROW_TILES
    unused = [(j, pltpu.make_async_copy(zeros, xs_hbm.at[pl.ds(j * tile_rows, tile_rows), :], sems.at[1]))
              for j in range(xs_hbm.shape[0] // tile_rows)]

    @pl.when(first)
    def _():
        zeros[...] = jnp.zeros(zeros.shape, F32)
        fills = [pltpu.make_async_copy(zeros, xs_hbm.at[pl.ds(pad_ref[e] * ROW_TILES, tile_rows), :], sems.at[0])
                 for e in range(N_EXPERTS)]
        for f in fills:
            f.start()
        for f in fills:
            f.wait()
        for j, fill in unused:
            pl.when(j >= nt_ref[0])(fill.start)

    def body(j, _):
        for k in range(TOP_K):
            pltpu.make_async_copy(_token_tile(x_ref, j), _token_tile(xs_hbm, pos_ref[k * t + base + j]),
                                  sems.at[0]).start(priority=k)
        return 0

    lax.fori_loop(0, tm, body, 0, unroll=8)
    for k in range(TOP_K):
        pltpu.make_async_copy(x_ref, xs_hbm.at[pl.ds(0, tm * ROW_TILES), :], sems.at[0]).wait()

    @pl.when(first)
    def _():
        for j, fill in unused:
            pl.when(j >= nt_ref[0])(fill.wait)


def _dispatch(pos, first_pad, ntiles, x1t, tm=1024):
    t = x1t.shape[0] // ROW_TILES
    return pl.pallas_call(
        _dispatch_kernel,
        grid_spec=pltpu.PrefetchScalarGridSpec(
            num_scalar_prefetch=3, grid=(t // tm,),
            in_specs=[pl.BlockSpec((tm * ROW_TILES, LANES), lambda i, pos, pad, nt: (i, 0))],
            out_specs=pl.BlockSpec(memory_space=pl.ANY),
            scratch_shapes=[pltpu.VMEM((MOE_TILE * ROW_TILES, LANES), F32), pltpu.SemaphoreType.DMA((2,))]),
        out_shape=jax.ShapeDtypeStruct((_moe_rows(t) * ROW_TILES, LANES), F32),
        compiler_params=_cparams(("arbitrary",)),
        name="moe_dispatch",
    )(pos, first_pad, ntiles, x1t)


def _ffn_kernel(te_ref, nt_ref, xs_ref, w1_ref, w3_ref, w2_ref, ys_ref, w1b, w3b, w2b):
    j = pl.program_id(0)
    in_use = j < nt_ref[0]

    @pl.when(in_use & ((j == 0) | (te_ref[j] != te_ref[jnp.maximum(j - 1, 0)])))
    def _():
        w1b[...] = w1_ref[0].astype(BF16)
        w3b[...] = w3_ref[0].astype(BF16)
        w2b[...] = w2_ref[0].astype(BF16)

    @pl.when(in_use)
    def _():
        xb = _load_token_tiles(xs_ref, MOE_TILE).astype(BF16)
        h1 = jnp.dot(xb, w1b[...], preferred_element_type=F32)
        h3 = jnp.dot(xb, w3b[...], preferred_element_type=F32)
        h = (h1 * jax.nn.sigmoid(h1) * h3).astype(BF16)
        _store_token_tiles(ys_ref, jnp.dot(h, w2b[...], preferred_element_type=F32))

    @pl.when(jnp.logical_not(in_use))
    def _():
        ys_ref[...] = jnp.zeros(ys_ref.shape, F32)


def _ffn(tile_expert, ntiles, xs, layer, w1, w3, w2):
    ntile = xs.shape[0] // (MOE_TILE * ROW_TILES)
    tile = lambda j, te, nt: (jnp.minimum(j, nt[0] - 1), 0)
    expert = lambda j, te, nt: (layer, te[jnp.minimum(j, nt[0] - 1)], 0, 0)
    return pl.pallas_call(
        _ffn_kernel,
        grid_spec=pltpu.PrefetchScalarGridSpec(
            num_scalar_prefetch=2, grid=(ntile,),
            in_specs=[pl.BlockSpec((MOE_TILE * ROW_TILES, LANES), tile),
                      pl.BlockSpec((None, 1, D_MODEL, D_EXPERT), expert),
                      pl.BlockSpec((None, 1, D_MODEL, D_EXPERT), expert),
                      pl.BlockSpec((None, 1, D_EXPERT, D_MODEL), expert)],
            out_specs=pl.BlockSpec((MOE_TILE * ROW_TILES, LANES), lambda j, te, nt: (j, 0)),
            scratch_shapes=[pltpu.VMEM((D_MODEL, D_EXPERT), BF16), pltpu.VMEM((D_MODEL, D_EXPERT), BF16),
                            pltpu.VMEM((D_EXPERT, D_MODEL), BF16)]),
        out_shape=jax.ShapeDtypeStruct(xs.shape, F32),
        compiler_params=_cparams(("arbitrary",)),
        name="moe_ffn",
    )(tile_expert, ntiles, xs, w1, w3, w2)


def _combine_kernel(pos_ref, ys_hbm, x1t_ref, rc_ref, lg_ref, lb_ref, o_ref, ob_ref, gath, sems):
    tm = o_ref.shape[0]
    t = pos_ref.shape[0] // TOP_K
    i = pl.program_id(0)
    ntile = pl.num_programs(0)

    def issue(tile, slot):
        def body(j, _):
            for k in range(TOP_K):
                pltpu.make_async_copy(_token_tile(ys_hbm, pos_ref[k * t + tile * tm + j]),
                                      _token_tile(gath.at[slot, k], j), sems.at[slot]).start(priority=k)
            return 0
        lax.fori_loop(0, tm, body, 0, unroll=8)

    @pl.when(i == 0)
    def _():
        issue(0, 0)

    @pl.when(i + 1 < ntile)
    def _():
        issue(i + 1, (i + 1) % 2)

    slot = i % 2
    for k in range(TOP_K):
        pltpu.make_async_copy(ys_hbm.at[pl.ds(0, tm * ROW_TILES), :], gath.at[slot, k], sems.at[slot]).wait()
    rc = rc_ref[...]
    y = (rc[:, 0:1] * _load_token_tiles(gath.at[slot, 0], tm)
         + rc[:, 1:2] * _load_token_tiles(gath.at[slot, 1], tm))
    x2 = _layer_norm(DEEPNORM_ALPHA * _load_token_tiles(x1t_ref, tm) + y, lg_ref[...], lb_ref[...])
    o_ref[...] = x2
    ob_ref[...] = x2.astype(BF16)


def _combine(pos, ys, x1t, rcols, lg, lb, tm=256):
    t = x1t.shape[0] // ROW_TILES
    row = lambda w: pl.BlockSpec((tm, w), lambda i, pos: (i, 0))
    full = lambda shape: pl.BlockSpec(shape, lambda i, pos: (0,) * len(shape))
    return pl.pallas_call(
        _combine_kernel,
        grid_spec=pltpu.PrefetchScalarGridSpec(
            num_scalar_prefetch=1, grid=(t // tm,),
            in_specs=[pl.BlockSpec(memory_space=pl.ANY),
                      pl.BlockSpec((tm * ROW_TILES, LANES), lambda i, pos: (i, 0)), row(LANES),
                      full((1, D_MODEL)), full((1, D_MODEL))],
            out_specs=[row(D_MODEL), row(D_MODEL)],
            scratch_shapes=[pltpu.VMEM((2, TOP_K, tm * ROW_TILES, LANES), F32), pltpu.SemaphoreType.DMA((2,))]),
        out_shape=[jax.ShapeDtypeStruct((t, D_MODEL), F32), jax.ShapeDtypeStruct((t, D_MODEL), BF16)],
        compiler_params=_cparams(("arbitrary",)),
        name="moe_combine",
    )(pos, ys, x1t, rcols, lg, lb)


def _moe(x1t, rcols, ids, layer, w1, w3, w2, lg, lb):
    pos8, meta = _plan(ids)
    pos = pos8[:TOP_K].reshape(-1)
    nt_max = _moe_rows(x1t.shape[0] // ROW_TILES) // MOE_TILE
    ntiles = meta[1, :1]
    xs = _dispatch(pos, meta[2, :N_EXPERTS], ntiles, x1t)
    ys = _ffn(meta[0, :nt_max], ntiles, xs, layer, w1, w3, w2)
    return _combine(pos, ys, x1t, rcols, lg, lb)


def _split_hi_lo(w):
    hi = w.astype(BF16)
    return hi, (w - hi.astype(F32)).astype(BF16)


IN_AX, IN_AGATE, IN_BZ, IN_XBC = 0, D_MODEL, 2 * D_MODEL, 3 * D_MODEL
IN_DT = IN_XBC + SSM_CONV_CH
IN_QKV = IN_DT + SSM_HEADS
IN_F = IN_QKV + 3 * D_MODEL
IN_GATE = IN_F + ATTN_HEADS
P_IN = IN_GATE + 3 * D_MODEL
W_PREP_COLS = D_MODEL
W_PREP_SRC = (IN_QKV, IN_QKV + D_MODEL, IN_QKV + 2 * D_MODEL, IN_GATE, IN_GATE + D_MODEL, IN_GATE + 2 * D_MODEL,
              IN_XBC, IN_XBC + D_MODEL, IN_AX, IN_AGATE, IN_BZ)


def _w_prep_kernel(src_ref, wt_hbm, main_ref, small_ref, buf, sbuf, sems):
    layer = pl.program_id(0)
    j = pl.program_id(1)
    nblk = len(W_PREP_SRC)
    nstep = wt_hbm.shape[0] * nblk
    step = layer * nblk + j

    def block_copy(s):
        rows = pl.ds(pl.multiple_of(src_ref[s % nblk], SUBLANES), W_PREP_COLS)
        return pltpu.make_async_copy(wt_hbm.at[s // nblk, rows, :], buf.at[s % 2], sems.at[s % 2])

    @pl.when(step == 0)
    def _():
        block_copy(step).start()

    @pl.when(step + 1 < nstep)
    def _():
        block_copy(step + 1).start()

    @pl.when(j == 0)
    def _():
        sbuf[...] = jnp.zeros(sbuf.shape, F32)
        parts = [pltpu.make_async_copy(wt_hbm.at[layer, pl.ds(src, n), :], sbuf.at[pl.ds(dst, n), :], sems.at[2])
                 for src, dst, n in ((IN_DT, 0, SSM_HEADS), (IN_F, SSM_HEADS, ATTN_HEADS))]
        for c in parts:
            c.start()
        for c in parts:
            c.wait()
        small_ref[0] = sbuf[...].T.astype(BF16)

    block_copy(step).wait()
    scale = jnp.where(j == 0, ATTN_HEAD_DIM ** -0.5 * LOG2E, 1.0)
    main_ref[0] = (buf[step % 2].T * scale).astype(BF16)


def _w_prep(w_in):
    depth, k, n = w_in.shape
    assert n == P_IN and COL_QKV == 0 and len(W_PREP_SRC) * W_PREP_COLS == N_MAIN
    w_t = jnp.swapaxes(w_in, 1, 2)
    return pl.pallas_call(
        _w_prep_kernel,
        grid_spec=pltpu.PrefetchScalarGridSpec(
            num_scalar_prefetch=1, grid=(depth, len(W_PREP_SRC)),
            in_specs=[pl.BlockSpec(memory_space=pl.ANY)],
            out_specs=[pl.BlockSpec((1, k, W_PREP_COLS), lambda l, j, src: (l, 0, j)),
                       pl.BlockSpec((1, k, N_SMALL), lambda l, j, src: (l, 0, 0))],
            scratch_shapes=[pltpu.VMEM((2, W_PREP_COLS, k), F32), pltpu.VMEM((N_SMALL, k), F32),
                            pltpu.SemaphoreType.DMA((3,))]),
        out_shape=[jax.ShapeDtypeStruct((depth, k, N_MAIN), BF16), jax.ShapeDtypeStruct((depth, k, N_SMALL), BF16)],
        compiler_params=_cparams(("arbitrary", "arbitrary")),
        name="w_prep",
    )(jnp.asarray(W_PREP_SRC, jnp.int32), w_t)


def _prepare(w_in, gate_b, conv_a_w, conv_a_b, lru_wa, lru_ba, lru_wx, lru_bx, lru_lambda,
             conv_b_w, conv_b_b, dt_bias, a_log, d_skip, ssm_norm_w, forget_b,
             w_branch_a, w_branch_b, w_branch_c, w_out, ln1_g, ln1_b,
             router_w, router_b, w1, w3, w2, ln2_g, ln2_b):
    w_main, w_small = _w_prep(w_in)
    w_gates = jnp.concatenate([lru_wa, lru_wx], axis=-1).astype(BF16)
    pad_heads = lambda v, off: jnp.pad(v, ((0, 0), (off, N_SMALL - off - v.shape[1])))[:, None, :]
    dtb_p = pad_heads(dt_bias, 0)
    alog_p = pad_heads(a_log, 0)
    fb_p = pad_heads(forget_b, SSM_HEADS)
    dskip_e = jnp.repeat(d_skip, SSM_HEAD_DIM, axis=-1)[:, None, :]
    row = lambda v: v[:, None, :]
    wa_b, wb_b, wc_b, wo_b = (w.astype(BF16) for w in (w_branch_a, w_branch_b, w_branch_c, w_out))
    rwh, rwl = _split_hi_lo(router_w.T)
    return dict(
        w_main=w_main, w_small=w_small, conv_a_w=conv_a_w, conv_a_b=row(conv_a_b), w_gates=w_gates,
        lru_ba=row(lru_ba), lru_bx=row(lru_bx), lru_lambda=row(lru_lambda),
        conv_b_w=conv_b_w, conv_b_b=row(conv_b_b), dtb=dtb_p, alog=alog_p, dskip=dskip_e,
        ssm_norm_w=row(ssm_norm_w), fb=fb_p, wa=wa_b, wb=wb_b, wc=wc_b, wo=wo_b, gate_b=row(gate_b),
        ln1_g=row(ln1_g), ln1_b=row(ln1_b), rwh=rwh, rwl=rwl, rb=router_b[:, None],
        w1=w1, w3=w3, w2=w2, ln2_g=row(ln2_g), ln2_b=row(ln2_b))


def _layer(l, xf, xb, p, bsz, seq):
    proj, small = _in_proj(xb, p['w_main'], p['w_small'], l)
    ha = _branch_a(proj, p['conv_a_w'][l], p['conv_a_b'][l], p['w_gates'][l], p['lru_ba'][l], p['lru_bx'][l],
                   p['lru_lambda'][l], bsz, seq)
    hb = _branch_b(proj, small, p['conv_b_w'][l], p['conv_b_b'][l], p['dtb'][l], p['alog'][l], p['dskip'][l],
                   p['ssm_norm_w'][l], bsz, seq)
    cum = _fox_cum(small, p['fb'][l], bsz, seq)
    hc = _branch_c(proj, cum, bsz, seq)
    x1, rcols, ids = _merge(ha, hb, hc, proj, xf, p['wa'][l], p['wb'][l], p['wc'][l], p['wo'][l],
                            p['gate_b'][l], p['ln1_g'][l], p['ln1_b'][l], p['rwh'], p['rwl'], p['rb'])
    x2, x2b = _moe(x1, rcols, ids, l, p['w1'], p['w3'], p['w2'], p['ln2_g'][l], p['ln2_b'][l])
    return dict(proj=proj, small=small, ha=ha, hb=hb, cum=cum, hc=hc, x1=x1, rcols=rcols, ids=ids,
                x2=x2, x2b=x2b)


def kernel(x, w_in, gate_b, conv_a_w, conv_a_b, lru_wa, lru_ba, lru_wx, lru_bx, lru_lambda,
           conv_b_w, conv_b_b, dt_bias, a_log, d_skip, ssm_norm_w, forget_b,
           w_branch_a, w_branch_b, w_branch_c, w_out, ln1_g, ln1_b,
           router_w, router_b, w1, w3, w2, ln2_g, ln2_b):
    bsz, seq, d = x.shape
    p = _prepare(w_in, gate_b, conv_a_w, conv_a_b, lru_wa, lru_ba, lru_wx, lru_bx, lru_lambda,
                 conv_b_w, conv_b_b, dt_bias, a_log, d_skip, ssm_norm_w, forget_b,
                 w_branch_a, w_branch_b, w_branch_c, w_out, ln1_g, ln1_b,
                 router_w, router_b, w1, w3, w2, ln2_g, ln2_b)
    xf = x.reshape(bsz * seq, d)
    xb = xf.astype(BF16)
    for l in range(w_in.shape[0]):
        stages = _layer(l, xf, xb, p, bsz, seq)
        xf, xb = stages['x2'], stages['x2b']
    return xf.reshape(bsz, seq, d)
```

```python
import functools

import jax
import jax.numpy as jnp
from jax import lax
from jax.experimental import pallas as pl
from jax.experimental.pallas import tpu as pltpu

F32 = jnp.float32
BF16 = jnp.bfloat16

D_MODEL = 1024
DEPTH = 4
RNN_HEADS = 8
RNN_BLOCK = 128
CONV_WIDTH = 4
LRU_C = 8.0
SSM_HEADS = 16
SSM_HEAD_DIM = 64
SSM_GROUPS = 4
SSM_STATE = 128
SSM_CHUNK = 128
SSM_CONV_CH = 2048
ATTN_HEADS = 16
ATTN_HEAD_DIM = 64
N_EXPERTS = 16
EXPERTS_PER_GROUP = 4
D_EXPERT = 512
LN_EPS = 1e-5
RMS_EPS = 1e-6
DEEPNORM_ALPHA = (2 * DEPTH) ** 0.25

LANES = 128
SUBLANES = 8
VMEM_LIMIT = 48 * 1024 * 1024

COL_QKV = 0
COL_GATE = 3072
COL_XBC = 6144
COL_AX = 8192
COL_AGATE = 9216
COL_BZ = 10240
N_MAIN = 11264
N_SMALL = 128


def _cparams(sem):
    return pltpu.CompilerParams(dimension_semantics=sem, vmem_limit_bytes=VMEM_LIMIT)


def _in_proj_kernel(x_ref, w_ref, ws_ref, o_ref, os_ref):
    x = x_ref[...]
    o_ref[...] = jnp.dot(x, w_ref[...], preferred_element_type=F32).astype(o_ref.dtype)

    @pl.when(pl.program_id(1) == 0)
    def _():
        os_ref[...] = jnp.dot(x, ws_ref[...], preferred_element_type=F32)


def _in_proj(x, w_main, w_small, layer, tm=2048, tn=1024):
    m, k = x.shape
    return pl.pallas_call(
        _in_proj_kernel,
        grid=(m // tm, N_MAIN // tn),
        in_specs=[pl.BlockSpec((tm, k), lambda i, j: (i, 0)),
                  pl.BlockSpec((None, k, tn), lambda i, j: (layer, 0, j)),
                  pl.BlockSpec((None, k, N_SMALL), lambda i, j: (layer, 0, 0))],
        out_specs=[pl.BlockSpec((tm, tn), lambda i, j: (i, j)),
                   pl.BlockSpec((tm, N_SMALL), lambda i, j: (i, 0))],
        out_shape=[jax.ShapeDtypeStruct((m, N_MAIN), BF16), jax.ShapeDtypeStruct((m, N_SMALL), F32)],
        compiler_params=_cparams(("parallel", "arbitrary")),
        name="in_proj",
    )(x, w_main, w_small)


CONV_BAND = 128
BF16_ROWS = 16


def _causal_conv(x, hist, cw_ref, cb_ref, first):
    ts, ch = x.shape

    @pl.when(first)
    def _():
        hist[...] = jnp.zeros(hist.shape, BF16)

    ext = jnp.concatenate([hist[...], x], axis=0)
    hist[...] = x[ts - BF16_ROWS:ts, :]
    taps = CONV_WIDTH - 1
    out_row = lax.broadcasted_iota(jnp.int32, (taps * CONV_BAND, CONV_BAND + BF16_ROWS), 0)
    in_row = lax.broadcasted_iota(jnp.int32, (taps * CONV_BAND, CONV_BAND + BF16_ROWS), 1)
    tap = out_row // CONV_BAND
    shift = jnp.where(in_row == (out_row - tap * CONV_BAND) + BF16_ROWS - taps + tap, 1.0, 0.0).astype(BF16)
    bands = []
    for b0 in range(0, ts, CONV_BAND):
        shifted = jnp.dot(shift, ext[b0:b0 + CONV_BAND + BF16_ROWS, :], preferred_element_type=F32)
        y = cb_ref[...] + cw_ref[taps:taps + 1, :] * x[b0:b0 + CONV_BAND, :].astype(F32)
        for k in range(taps):
            y = y + cw_ref[k:k + 1, :] * shifted[k * CONV_BAND:(k + 1) * CONV_BAND, :]
        bands.append(y)
    return bands[0] if len(bands) == 1 else jnp.concatenate(bands, axis=0)


def _bf16_pieces(x, parts):
    pieces = []
    rest = x
    for _ in range(parts):
        piece = rest.astype(BF16)
        pieces.append(piece)
        rest = rest - piece.astype(F32)
    return pieces


def _select_dot(x, w, parts):
    pieces = _bf16_pieces(x, parts)
    return jnp.dot(jnp.concatenate(pieces, axis=1), jnp.concatenate([w] * parts, axis=0),
                   preferred_element_type=F32)


def _cumsum_rows(x, parts):
    n = x.shape[0]
    ri = lax.broadcasted_iota(jnp.int32, (n, n), 0)
    ci = lax.broadcasted_iota(jnp.int32, (n, n), 1)
    tril = jnp.where(ri >= ci, 1.0, 0.0).astype(BF16)
    pieces = _bf16_pieces(x, parts)
    return jnp.dot(jnp.concatenate([tril] * parts, axis=1), jnp.concatenate(pieces, axis=0),
                   preferred_element_type=F32)


ROW_TILES = D_MODEL // LANES


def _store_token_tiles(ref, x):
    rows = x.shape[0]
    for s in range(ROW_TILES):
        ref[pl.ds(s, rows, stride=ROW_TILES), :] = x[:, s * LANES:(s + 1) * LANES]


def _load_token_tiles(ref, rows):
    return jnp.concatenate([ref[pl.ds(s, rows, stride=ROW_TILES), :] for s in range(ROW_TILES)], axis=1)


def _token_tile(ref, row):
    return ref.at[pl.ds(pl.multiple_of(row * ROW_TILES, ROW_TILES), ROW_TILES), :]


def _rglru_kernel(x_ref, g_ref, cw_ref, cb_ref, wg_ref, ba_ref, bx_ref, lam_ref, o_ref, hist, hcar):
    s = pl.program_id(1)
    ts = x_ref.shape[0]
    first = s == 0

    @pl.when(first)
    def _():
        hcar[...] = jnp.zeros(hcar.shape, F32)

    xa = _causal_conv(x_ref[...], hist, cw_ref, cb_ref, first)
    xab = xa.astype(BF16)
    r_parts, i_parts = [], []
    for h in range(RNN_HEADS):
        pre = jnp.dot(xab[:, h * RNN_BLOCK:(h + 1) * RNN_BLOCK], wg_ref[h], preferred_element_type=F32)
        r_parts.append(pre[:, :RNN_BLOCK])
        i_parts.append(pre[:, RNN_BLOCK:])
    r_gate = jax.nn.sigmoid(jnp.concatenate(r_parts, axis=1) + ba_ref[...])
    i_gate = jax.nn.sigmoid(jnp.concatenate(i_parts, axis=1) + bx_ref[...])
    log_a = (-LRU_C) * r_gate * jax.nn.softplus(-lam_ref[...])
    a = jnp.exp(log_a)
    one_minus_a2 = 1.0 - jnp.exp(2.0 * log_a)
    mult = one_minus_a2 * lax.rsqrt(jnp.maximum(one_minus_a2, 1e-30))
    u = (xa * i_gate) * mult

    ng = ts // SUBLANES
    a3 = a.reshape(ng, SUBLANES, D_MODEL)
    b3 = u.reshape(ng, SUBLANES, D_MODEL)
    row = lax.broadcasted_iota(jnp.int32, a3.shape, 1)
    d = 1
    while d < SUBLANES:
        valid = row >= d
        a_s = jnp.where(valid, pltpu.roll(a3, d, axis=1), 1.0)
        b_s = jnp.where(valid, pltpu.roll(b3, d, axis=1), 0.0)
        b3 = a3 * b_s + b3
        a3 = a3 * a_s
        d *= 2
    h_in = hcar[SUBLANES - 1:SUBLANES, :]
    groups = []
    for gi in range(ng):
        hg = b3[gi] + a3[gi] * h_in
        groups.append(hg)
        h_in = hg[SUBLANES - 1:SUBLANES, :]
    h = jnp.concatenate(groups, axis=0)
    hcar[...] = groups[-1]
    o_ref[...] = (h * jax.nn.gelu(g_ref[...].astype(F32))).astype(o_ref.dtype)


def _branch_a(proj, cw, cb, wg, ba, bx, lam, bsz, seq, ts=512):
    nst = seq // ts
    full = lambda shape: pl.BlockSpec(shape, lambda b, s: (0,) * len(shape))
    return pl.pallas_call(
        _rglru_kernel,
        grid=(bsz, nst),
        in_specs=[pl.BlockSpec((ts, D_MODEL), lambda b, s: (b * nst + s, COL_AX // D_MODEL)),
                  pl.BlockSpec((ts, D_MODEL), lambda b, s: (b * nst + s, COL_AGATE // D_MODEL)),
                  full((CONV_WIDTH, D_MODEL)), full((1, D_MODEL)),
                  full((RNN_HEADS, RNN_BLOCK, 2 * RNN_BLOCK)),
                  full((1, D_MODEL)), full((1, D_MODEL)), full((1, D_MODEL))],
        out_specs=pl.BlockSpec((ts, D_MODEL), lambda b, s: (b * nst + s, 0)),
        out_shape=jax.ShapeDtypeStruct((bsz * seq, D_MODEL), BF16),
        scratch_shapes=[pltpu.VMEM((BF16_ROWS, D_MODEL), BF16), pltpu.VMEM((SUBLANES, D_MODEL), F32)],
        compiler_params=_cparams(("parallel", "arbitrary")),
        name="rglru",
    )(proj, proj, cw, cb, wg, ba, bx, lam)


def _ssd_kernel(z_ref, xbc_ref, dtf_ref, cw_ref, cb_ref, dtb_ref, alog_ref, dskip_ref, nw_ref,
                o_ref, hist, state):
    first = pl.program_id(1) == 0

    @pl.when(first)
    def _():
        state[...] = jnp.zeros(state.shape, F32)

    conv = _causal_conv(xbc_ref[...], hist, cw_ref, cb_ref, first)
    for r0 in range(0, conv.shape[0], SSM_CHUNK):
        rows = slice(r0, r0 + SSM_CHUNK)
        o_ref[rows, :] = _ssd_chunk(conv[rows], dtf_ref[rows, :], z_ref[rows, :].astype(F32),
                                    dtb_ref, alog_ref, dskip_ref, nw_ref, state).astype(o_ref.dtype)


def _ssd_chunk(conv, dtf, z, dtb_ref, alog_ref, dskip_ref, nw_ref, state):
    L = SSM_CHUNK
    act = conv * jax.nn.sigmoid(conv)
    xs = act[:, :D_MODEL]
    bm = act[:, D_MODEL:D_MODEL + SSM_GROUPS * SSM_STATE]
    cm = act[:, D_MODEL + SSM_GROUPS * SSM_STATE:]

    lane = lax.broadcasted_iota(jnp.int32, (L, LANES), 1)
    head_lane = lane < SSM_HEADS
    dt = jnp.where(head_lane, jax.nn.softplus(dtf + dtb_ref[...]), 0.0)
    a_dt = dt * (-jnp.exp(alog_ref[...]))
    ri = lax.broadcasted_iota(jnp.int32, (L, L), 0)
    ci = lax.broadcasted_iota(jnp.int32, (L, L), 1)
    causal = ri >= ci
    cs = _cumsum_rows(a_dt, 3)
    cs_t = cs.T
    tot = cs[L - 1:L, :]
    dstate = jnp.exp(tot - cs)
    exp_cs = jnp.exp(cs)

    er = lax.broadcasted_iota(jnp.int32, (LANES, D_MODEL), 0)
    ec = lax.broadcasted_iota(jnp.int32, (LANES, D_MODEL), 1)
    expand = jnp.where(ec // SSM_HEAD_DIM == er, 1.0, 0.0).astype(BF16)
    dt_e = _select_dot(dt, expand, 2)
    dtds_e = _select_dot(dt * dstate, expand, 2)
    tot_e = _select_dot(jnp.broadcast_to(jnp.exp(tot), (SUBLANES, LANES)), expand, 3)[0:1, :]
    xdt = xs * dt_e
    xdt_end = (xs * dtds_e).astype(BF16)

    lo_half = lax.broadcasted_iota(jnp.int32, (2 * L, LANES), 1) < SSM_HEAD_DIM
    heads_per_group = SSM_HEADS // SSM_GROUPS
    y_parts = []
    new_states = []
    for g in range(SSM_GROUPS):
        bg = bm[:, g * SSM_STATE:(g + 1) * SSM_STATE]
        cg = cm[:, g * SSM_STATE:(g + 1) * SSM_STATE]
        cb = lax.dot_general(cg.astype(BF16), bg.astype(BF16), (((1,), (1,)), ((), ())),
                             preferred_element_type=F32)
        lhs = []
        for e in range(heads_per_group):
            hd = g * heads_per_group + e
            colb = jnp.broadcast_to(cs[:, hd:hd + 1], (L, L))
            rowb = jnp.broadcast_to(cs_t[hd:hd + 1, :], (L, L))
            decay = jnp.exp(jnp.where(causal, colb - rowb, -jnp.inf))
            m = (cb * decay).astype(BF16)
            c_off = (cg * jnp.broadcast_to(exp_cs[:, hd:hd + 1], (L, L))).astype(BF16)
            lhs.append(jnp.concatenate([m, c_off], axis=1))
        for j in range(heads_per_group // 2):
            col = (g * heads_per_group + 2 * j) * SSM_HEAD_DIM
            rhs = jnp.concatenate([xdt[:, col:col + LANES], state[:, col:col + LANES]], axis=0).astype(BF16)
            zero = jnp.zeros_like(rhs)
            y_parts.append(jnp.dot(lhs[2 * j], jnp.where(lo_half, rhs, zero), preferred_element_type=F32)
                           + jnp.dot(lhs[2 * j + 1], jnp.where(lo_half, zero, rhs), preferred_element_type=F32))
        gw = heads_per_group * SSM_HEAD_DIM
        new_states.append(jnp.dot(bg.T.astype(BF16), xdt_end[:, g * gw:(g + 1) * gw],
                                  preferred_element_type=F32))
    y = jnp.concatenate(y_parts, axis=1)
    state[...] = state[...] * tot_e + jnp.concatenate(new_states, axis=1)

    y = y + xs * dskip_ref[...]
    gy = y * (z * jax.nn.sigmoid(z))
    gw = D_MODEL // SSM_GROUPS
    outs = []
    for g in range(SSM_GROUPS):
        gg = gy[:, g * gw:(g + 1) * gw]
        ms = jnp.mean(gg * gg, axis=-1, keepdims=True)
        outs.append(gg * lax.rsqrt(ms + RMS_EPS))
    return jnp.concatenate(outs, axis=1) * nw_ref[...]


def _branch_b(proj, small, cw, cb, dtb, alog, dskip_e, nw, bsz, seq, chunks_per_step=8):
    L = chunks_per_step * SSM_CHUNK
    nc = seq // L
    full = lambda shape: pl.BlockSpec(shape, lambda b, c: (0,) * len(shape))
    return pl.pallas_call(
        _ssd_kernel,
        grid=(bsz, nc),
        in_specs=[pl.BlockSpec((L, D_MODEL), lambda b, c: (b * nc + c, COL_BZ // D_MODEL)),
                  pl.BlockSpec((L, SSM_CONV_CH), lambda b, c: (b * nc + c, COL_XBC // SSM_CONV_CH)),
                  pl.BlockSpec((L, N_SMALL), lambda b, c: (b * nc + c, 0)),
                  full((CONV_WIDTH, SSM_CONV_CH)), full((1, SSM_CONV_CH)),
                  full((1, N_SMALL)), full((1, N_SMALL)), full((1, D_MODEL)), full((1, D_MODEL))],
        out_specs=pl.BlockSpec((L, D_MODEL), lambda b, c: (b * nc + c, 0)),
        out_shape=jax.ShapeDtypeStruct((bsz * seq, D_MODEL), BF16),
        scratch_shapes=[pltpu.VMEM((BF16_ROWS, SSM_CONV_CH), BF16), pltpu.VMEM((SSM_STATE, D_MODEL), F32)],
        compiler_params=_cparams(("parallel", "arbitrary")),
        name="ssd",
    )(proj, proj, small, cw, cb, dtb, alog, dskip_e, nw)


CUM_BLOCK = 256
LOG2E = 1.4426950408889634
BIAS_PARTS = 3


def _fox_cum_kernel(dtf_ref, fb_ref, o_ref):
    seq = dtf_ref.shape[0]
    npair = ATTN_HEADS // 2
    hd = ATTN_HEAD_DIM
    lane = lax.broadcasted_iota(jnp.int32, (CUM_BLOCK, LANES), 1)
    live = (lane >= SSM_HEADS) & (lane < SSM_HEADS + ATTN_HEADS)
    sr = lax.broadcasted_iota(jnp.int32, (BIAS_PARTS * LANES, npair * LANES), 0)
    sc = lax.broadcasted_iota(jnp.int32, (BIAS_PARTS * LANES, npair * LANES), 1)
    piece, src = sr // LANES, sr % LANES
    head0 = SSM_HEADS + 2 * (sc // LANES)
    dst = sc % LANES
    place = jnp.where(((src == head0) & (dst == hd + piece)) | ((src == head0 + 1) & (dst == piece)),
                      1.0, 0.0).astype(BF16)
    half = lax.broadcasted_iota(jnp.int32, (CUM_BLOCK, npair * LANES), 1) % hd
    ones = (half >= BIAS_PARTS) & (half < 2 * BIAS_PARTS)
    carry = jnp.zeros((1, LANES), F32)
    for i in range(seq // CUM_BLOCK):
        rows = slice(i * CUM_BLOCK, (i + 1) * CUM_BLOCK)
        logf = jnp.where(live, jax.nn.log_sigmoid(dtf_ref[rows, :] + fb_ref[...]), 0.0)
        cb = _cumsum_rows(logf, 3) + carry
        carry = cb[CUM_BLOCK - 1:CUM_BLOCK, :]
        pieces = jnp.concatenate(_bf16_pieces(cb * (-LOG2E), BIAS_PARTS), axis=1)
        aug = jnp.dot(pieces, place, preferred_element_type=F32)
        o_ref[rows, :] = jnp.where(ones, 1.0, aug).astype(BF16)


def _fox_cum(small, fb, bsz, seq):
    width = (ATTN_HEADS // 2) * LANES
    return pl.pallas_call(
        _fox_cum_kernel,
        grid=(bsz,),
        in_specs=[pl.BlockSpec((seq, N_SMALL), lambda b: (b, 0)),
                  pl.BlockSpec((1, N_SMALL), lambda b: (0, 0))],
        out_specs=pl.BlockSpec((seq, width), lambda b: (b, 0)),
        out_shape=jax.ShapeDtypeStruct((bsz * seq, width), BF16),
        compiler_params=_cparams(("parallel",)),
        name="fox_cum",
    )(small, fb)


def _fox_attn_kernel(q_ref, k_ref, v_ref, aug_ref, o_ref, k0_s, k1_s, *, tq):
    seq = q_ref.shape[0]
    hd = ATTN_HEAD_DIM
    lane = lax.broadcasted_iota(jnp.int32, (seq, LANES), 1)
    lo_half = lane < hd
    aug_k = aug_ref[...]
    k = k_ref[...]
    k0_s[...] = jnp.where(lo_half, k, aug_k)
    k1_s[...] = jnp.where(lo_half, aug_k, k)

    lane_q = lax.broadcasted_iota(jnp.int32, (tq, LANES), 1)
    lo_half_q = lane_q < hd
    half_q = lane_q % hd
    q_ones = half_q < BIAS_PARTS
    q_const = (half_q >= BIAS_PARTS) & (half_q < 2 * BIAS_PARTS)
    tri = lax.broadcasted_iota(jnp.int32, (tq, tq), 1) <= lax.broadcasted_iota(jnp.int32, (tq, tq), 0)
    nt = (((1,), (1,)), ((), ()))

    for qi in reversed(range(seq // tq)):
        q0 = qi * tq
        q = q_ref[q0:q0 + tq, :]
        row0 = aug_ref[q0:q0 + BF16_ROWS, :][0:1, :].astype(F32)
        c_row = jnp.broadcast_to(pltpu.roll(-row0, BIAS_PARTS, axis=1), (tq, LANES))
        q_aug = jnp.where(q_ones, 1.0, jnp.where(q_const, c_row, 0.0)).astype(BF16)
        q_heads = (jnp.where(lo_half_q, q, q_aug), jnp.where(lo_half_q, q_aug, q))
        outs = []
        for hh, k_s in enumerate((k0_s, k1_s)):
            s_diag = lax.dot_general(q_heads[hh], k_s[q0:q0 + tq, :], nt, preferred_element_type=F32)
            s_diag = jnp.where(tri, s_diag, -jnp.inf)
            m = jnp.max(s_diag, axis=-1, keepdims=True)
            if qi > 0:
                s_off = lax.dot_general(q_heads[hh], k_s[0:q0, :], nt, preferred_element_type=F32)
                m = jnp.maximum(m, jnp.max(s_off, axis=-1, keepdims=True))
            p_diag = jnp.exp2(s_diag - m)
            l = jnp.sum(p_diag, axis=-1, keepdims=True)
            acc = jnp.dot(p_diag.astype(BF16), v_ref[q0:q0 + tq, :], preferred_element_type=F32)
            if qi > 0:
                p_off = jnp.exp2(s_off - m)
                l = l + jnp.sum(p_off, axis=-1, keepdims=True)
                acc = acc + jnp.dot(p_off.astype(BF16), v_ref[0:q0, :], preferred_element_type=F32)
            outs.append(acc * (1.0 / l))
        o_ref[q0:q0 + tq, :] = jnp.where(lo_half_q, outs[0], outs[1]).astype(o_ref.dtype)


def _branch_c(proj, cum, bsz, seq, tq=512):
    npair = ATTN_HEADS // 2
    kv_scratch = pltpu.VMEM((seq, LANES), BF16)
    return pl.pallas_call(
        functools.partial(_fox_attn_kernel, tq=tq),
        grid=(bsz, npair),
        in_specs=[pl.BlockSpec((seq, LANES), lambda b, p: (b, COL_QKV // LANES + p)),
                  pl.BlockSpec((seq, LANES), lambda b, p: (b, COL_QKV // LANES + npair + p)),
                  pl.BlockSpec((seq, LANES), lambda b, p: (b, COL_QKV // LANES + 2 * npair + p)),
                  pl.BlockSpec((seq, LANES), lambda b, p: (b, p))],
        out_specs=pl.BlockSpec((seq, LANES), lambda b, p: (b, p)),
        out_shape=jax.ShapeDtypeStruct((bsz * seq, D_MODEL), BF16),
        scratch_shapes=[kv_scratch, kv_scratch],
        compiler_params=_cparams(("parallel", "parallel")),
        name="fox_attn",
    )(proj, proj, proj, cum)


def _layer_norm(x, g, b):
    mu = jnp.mean(x, axis=-1, keepdims=True)
    xc = x - mu
    var = jnp.mean(xc * xc, axis=-1, keepdims=True)
    return xc * lax.rsqrt(var + LN_EPS) * g + b


def _top2_sum(a, b, c, d):
    hi1, lo1 = jnp.maximum(a, b), jnp.minimum(a, b)
    hi2, lo2 = jnp.maximum(c, d), jnp.minimum(c, d)
    return jnp.maximum(hi1, hi2) + jnp.maximum(jnp.minimum(hi1, hi2), jnp.maximum(lo1, lo2))


def _route_rows(logits_t):
    rows = [logits_t[e:e + 1, :] for e in range(N_EXPERTS)]
    mx = functools.reduce(jnp.maximum, rows)
    ex = [jnp.exp(r - mx) for r in rows]
    den = functools.reduce(jnp.add, ex)
    probs = [e / den for e in ex]
    ngroups = N_EXPERTS // EXPERTS_PER_GROUP
    scores = [_top2_sum(*probs[EXPERTS_PER_GROUP * g:EXPERTS_PER_GROUP * (g + 1)]) for g in range(ngroups)]
    best_g = jnp.zeros_like(mx, dtype=jnp.int32)
    best_s = scores[0]
    for g in range(1, ngroups):
        better = scores[g] > best_s
        best_g = jnp.where(better, g, best_g)
        best_s = jnp.where(better, scores[g], best_s)
    masked = [jnp.where(best_g == e // EXPERTS_PER_GROUP, probs[e], -1.0) for e in range(N_EXPERTS)]
    v1, i1 = masked[0], jnp.zeros_like(best_g)
    for e in range(1, N_EXPERTS):
        better = masked[e] > v1
        i1 = jnp.where(better, e, i1)
        v1 = jnp.where(better, masked[e], v1)
    v2, i2 = jnp.full_like(v1, -2.0), jnp.zeros_like(best_g)
    for e in range(N_EXPERTS):
        better = (masked[e] > v2) & (i1 != e)
        i2 = jnp.where(better, e, i2)
        v2 = jnp.where(better, masked[e], v2)
    tot = v1 + v2
    return i1, i2, v1 / tot, v2 / tot


def _merge_kernel(ha_ref, hb_ref, hc_ref, gate_ref, x_ref, wa_ref, wb_ref, wc_ref, wo_ref, gb_ref,
                  lg_ref, lb_ref, rwh_ref, rwl_ref, rb_ref,
                  x1t_ref, rcols_ref, ids_ref):
    tm = x_ref.shape[0]
    ya = jnp.dot(ha_ref[...], wa_ref[...], preferred_element_type=F32)
    yb = jnp.dot(hb_ref[...], wb_ref[...], preferred_element_type=F32)
    yc = jnp.dot(hc_ref[...], wc_ref[...], preferred_element_type=F32)
    g = jax.nn.sigmoid(gate_ref[...].astype(F32) + gb_ref[...])
    mixed_in = (g[:, :D_MODEL] * ya + g[:, D_MODEL:2 * D_MODEL] * yb + g[:, 2 * D_MODEL:] * yc).astype(BF16)
    mixed = jnp.dot(mixed_in, wo_ref[...], preferred_element_type=F32)
    x1 = _layer_norm(DEEPNORM_ALPHA * x_ref[...] + mixed, lg_ref[...], lb_ref[...])
    _store_token_tiles(x1t_ref, x1)
    x1h = x1.astype(BF16)
    x1l = (x1 - x1h.astype(F32)).astype(BF16)
    nt = (((1,), (1,)), ((), ()))
    logits_t = (lax.dot_general(rwh_ref[...], x1h, nt, preferred_element_type=F32)
                + lax.dot_general(rwl_ref[...], x1h, nt, preferred_element_type=F32)
                + lax.dot_general(rwh_ref[...], x1l, nt, preferred_element_type=F32)
                + rb_ref[...])
    i1, i2, w1, w2 = _route_rows(logits_t)
    sub = lax.broadcasted_iota(jnp.int32, (SUBLANES, tm), 0)
    ids_ref[...] = jnp.where(sub == 0, i1, jnp.where(sub == 1, i2, 0))
    wrows = jnp.where(sub == 0, w1, jnp.where(sub == 1, w2, 0.0))
    wrows = jnp.concatenate([wrows, jnp.zeros((LANES - SUBLANES, tm), F32)], axis=0)
    rcols_ref[...] = wrows.T


def _merge(ha, hb, hc, proj, x, wa, wb, wc, wo, gb, lg, lb, rwh, rwl, rb, tm=512):
    t = x.shape[0]
    full = lambda shape: pl.BlockSpec(shape, lambda i: (0,) * len(shape))
    row = lambda w: pl.BlockSpec((tm, w), lambda i: (i, 0))
    return pl.pallas_call(
        _merge_kernel,
        grid=(t // tm,),
        in_specs=[row(D_MODEL), row(D_MODEL), row(D_MODEL),
                  pl.BlockSpec((tm, 3 * D_MODEL), lambda i: (i, COL_GATE // (3 * D_MODEL))),
                  row(D_MODEL),
                  full((D_MODEL, D_MODEL)), full((D_MODEL, D_MODEL)), full((D_MODEL, D_MODEL)),
                  full((D_MODEL, D_MODEL)), full((1, 3 * D_MODEL)),
                  full((1, D_MODEL)), full((1, D_MODEL)),
                  full((N_EXPERTS, D_MODEL)), full((N_EXPERTS, D_MODEL)), full((N_EXPERTS, 1))],
        out_specs=[pl.BlockSpec((tm * ROW_TILES, LANES), lambda i: (i, 0)), row(LANES),
                   pl.BlockSpec((SUBLANES, tm), lambda i: (0, i))],
        out_shape=[jax.ShapeDtypeStruct((t * ROW_TILES, LANES), F32), jax.ShapeDtypeStruct((t, LANES), F32),
                   jax.ShapeDtypeStruct((SUBLANES, t), jnp.int32)],
        compiler_params=_cparams(("parallel",)),
        name="merge",
    )(ha, hb, hc, proj, x, wa, wb, wc, wo, gb, lg, lb, rwh, rwl, rb)


TOP_K = 2
MOE_TILE = 512
MOE_TILE_SHIFT = 9
PLAN_BLOCK = 256


def _moe_rows(t):
    return TOP_K * t + (N_EXPERTS + 1) * MOE_TILE


def _plan_kernel(ids_ref, pos_ref, meta_ref, cnt_s):
    t = ids_ref.shape[1]
    nblk = t // PLAN_BLOCK
    sub_e = lax.broadcasted_iota(jnp.int32, (N_EXPERTS, PLAN_BLOCK), 0)
    ur = lax.broadcasted_iota(jnp.int32, (PLAN_BLOCK, PLAN_BLOCK), 0)
    uc = lax.broadcasted_iota(jnp.int32, (PLAN_BLOCK, PLAN_BLOCK), 1)
    before = jnp.where(ur < uc, 1.0, 0.0).astype(BF16)

    def one_hots(c):
        cols = pl.ds(pl.multiple_of(c * PLAN_BLOCK, PLAN_BLOCK), PLAN_BLOCK)
        ids = ids_ref[:, cols]
        oh0 = jnp.where(ids[0:1, :] == sub_e, 1.0, 0.0)
        oh1 = jnp.where(ids[1:2, :] == sub_e, 1.0, 0.0)
        return cols, oh0, oh1

    def count_block(c, carry):
        cols, oh0, oh1 = one_hots(c)
        oh = oh0 + oh1
        cnt_s[:, cols] = jnp.dot(oh.astype(BF16), before, preferred_element_type=F32) + carry
        return carry + jnp.sum(oh, axis=1, keepdims=True)

    counts = lax.fori_loop(0, nblk, count_block, jnp.zeros((N_EXPERTS, 1), F32))
    padded = ((counts.astype(jnp.int32) + (MOE_TILE - 1)) >> MOE_TILE_SHIFT) << MOE_TILE_SHIFT
    padded = jnp.broadcast_to(padded, (N_EXPERTS, LANES))
    sub = lax.broadcasted_iota(jnp.int32, (N_EXPERTS, LANES), 0)
    lane = lax.broadcasted_iota(jnp.int32, (N_EXPERTS, LANES), 1)
    start = jnp.zeros((N_EXPERTS, LANES), jnp.int32)
    run = jnp.zeros((1, LANES), jnp.int32)
    for e in range(N_EXPERTS):
        start = jnp.where(sub == e, run, start)
        run = run + padded[e:e + 1, :]
    ended = jnp.where(start + padded <= lane * MOE_TILE, 1, 0)
    tile_expert = jnp.minimum(jnp.sum(ended, axis=0, keepdims=True), N_EXPERTS - 1)
    first_pad = start + jnp.broadcast_to(counts.astype(jnp.int32), (N_EXPERTS, LANES))
    first_pad = jnp.sum(jnp.where(sub == lane, first_pad, 0), axis=0, keepdims=True)
    sub8 = lax.broadcasted_iota(jnp.int32, (SUBLANES, LANES), 0)
    meta_ref[...] = jnp.where(sub8 == 0, tile_expert,
                              jnp.where(sub8 == 1, run >> MOE_TILE_SHIFT, jnp.where(sub8 == 2, first_pad, 0)))

    start_f = start[:, 0:1].astype(F32)
    sub8b = lax.broadcasted_iota(jnp.int32, (SUBLANES, PLAN_BLOCK), 0)

    def place_block(c, _):
        cols, oh0, oh1 = one_hots(c)
        base = cnt_s[:, cols] + start_f
        p0 = jnp.sum(oh0 * base, axis=0, keepdims=True).astype(jnp.int32)
        p1 = jnp.sum(oh1 * base, axis=0, keepdims=True).astype(jnp.int32)
        pos_ref[:, cols] = jnp.where(sub8b == 0, p0, jnp.where(sub8b == 1, p1, 0))
        return 0

    lax.fori_loop(0, nblk, place_block, 0)


def _plan(ids):
    t = ids.shape[1]
    return pl.pallas_call(
        _plan_kernel,
        out_shape=[jax.ShapeDtypeStruct((SUBLANES, t), jnp.int32),
                   jax.ShapeDtypeStruct((SUBLANES, LANES), jnp.int32)],
        scratch_shapes=[pltpu.VMEM((N_EXPERTS, t), F32)],
        compiler_params=pltpu.CompilerParams(vmem_limit_bytes=VMEM_LIMIT),
        name="moe_plan",
    )(ids)


def _dispatch_kernel(pos_ref, pad_ref, nt_ref, x_ref, xs_hbm, zeros, sems):
    tm = x_ref.shape[0] // ROW_TILES
    t = pos_ref.shape[0] // TOP_K
    base = pl.program_id(0) * tm
    first = pl.program_id(0) == 0
    tile_rows = MOE_TILE * ROW_TILES
    unused = [(j, pltpu.make_async_copy(zeros, xs_hbm.at[pl.ds(j * tile_rows, tile_rows), :], sems.at[1]))
              for j in range(xs_hbm.shape[0] // tile_rows)]

    @pl.when(first)
    def _():
        zeros[...] = jnp.zeros(zeros.shape, F32)
        fills = [pltpu.make_async_copy(zeros, xs_hbm.at[pl.ds(pad_ref[e] * ROW_TILES, tile_rows), :], sems.at[0])
                 for e in range(N_EXPERTS)]
        for f in fills:
            f.start()
        for f in fills:
            f.wait()
        for j, fill in unused:
            pl.when(j >= nt_ref[0])(fill.start)

    def body(j, _):
        for k in range(TOP_K):
            pltpu.make_async_copy(_token_tile(x_ref, j), _token_tile(xs_hbm, pos_ref[k * t + base + j]),
                                  sems.at[0]).start(priority=k)
        return 0

    lax.fori_loop(0, tm, body, 0, unroll=8)
    for k in range(TOP_K):
        pltpu.make_async_copy(x_ref, xs_hbm.at[pl.ds(0, tm * ROW_TILES), :], sems.at[0]).wait()

    @pl.when(first)
    def _():
        for j, fill in unused:
            pl.when(j >= nt_ref[0])(fill.wait)


def _dispatch(pos, first_pad, ntiles, x1t, tm=1024):
    t = x1t.shape[0] // ROW_TILES
    return pl.pallas_call(
        _dispatch_kernel,
        grid_spec=pltpu.PrefetchScalarGridSpec(
            num_scalar_prefetch=3, grid=(t // tm,),
            in_specs=[pl.BlockSpec((tm * ROW_TILES, LANES), lambda i, pos, pad, nt: (i, 0))],
            out_specs=pl.BlockSpec(memory_space=pl.ANY),
            scratch_shapes=[pltpu.VMEM((MOE_TILE * ROW_TILES, LANES), F32), pltpu.SemaphoreType.DMA((2,))]),
        out_shape=jax.ShapeDtypeStruct((_moe_rows(t) * ROW_TILES, LANES), F32),
        compiler_params=_cparams(("arbitrary",)),
        name="moe_dispatch",
    )(pos, first_pad, ntiles, x1t)


def _ffn_kernel(te_ref, nt_ref, xs_ref, w1_ref, w3_ref, w2_ref, ys_ref, w1b, w3b, w2b):
    j = pl.program_id(0)
    in_use = j < nt_ref[0]

    @pl.when(in_use & ((j == 0) | (te_ref[j] != te_ref[jnp.maximum(j - 1, 0)])))
    def _():
        w1b[...] = w1_ref[0].astype(BF16)
        w3b[...] = w3_ref[0].astype(BF16)
        w2b[...] = w2_ref[0].astype(BF16)

    @pl.when(in_use)
    def _():
        xb = _load_token_tiles(xs_ref, MOE_TILE).astype(BF16)
        h1 = jnp.dot(xb, w1b[...], preferred_element_type=F32)
        h3 = jnp.dot(xb, w3b[...], preferred_element_type=F32)
        h = (h1 * jax.nn.sigmoid(h1) * h3).astype(BF16)
        _store_token_tiles(ys_ref, jnp.dot(h, w2b[...], preferred_element_type=F32))

    @pl.when(jnp.logical_not(in_use))
    def _():
        ys_ref[...] = jnp.zeros(ys_ref.shape, F32)


def _ffn(tile_expert, ntiles, xs, layer, w1, w3, w2):
    ntile = xs.shape[0] // (MOE_TILE * ROW_TILES)
    tile = lambda j, te, nt: (jnp.minimum(j, nt[0] - 1), 0)
    expert = lambda j, te, nt: (layer, te[jnp.minimum(j, nt[0] - 1)], 0, 0)
    return pl.pallas_call(
        _ffn_kernel,
        grid_spec=pltpu.PrefetchScalarGridSpec(
            num_scalar_prefetch=2, grid=(ntile,),
            in_specs=[pl.BlockSpec((MOE_TILE * ROW_TILES, LANES), tile),
                      pl.BlockSpec((None, 1, D_MODEL, D_EXPERT), expert),
                      pl.BlockSpec((None, 1, D_MODEL, D_EXPERT), expert),
                      pl.BlockSpec((None, 1, D_EXPERT, D_MODEL), expert)],
            out_specs=pl.BlockSpec((MOE_TILE * ROW_TILES, LANES), lambda j, te, nt: (j, 0)),
            scratch_shapes=[pltpu.VMEM((D_MODEL, D_EXPERT), BF16), pltpu.VMEM((D_MODEL, D_EXPERT), BF16),
                            pltpu.VMEM((D_EXPERT, D_MODEL), BF16)]),
        out_shape=jax.ShapeDtypeStruct(xs.shape, F32),
        compiler_params=_cparams(("arbitrary",)),
        name="moe_ffn",
    )(tile_expert, ntiles, xs, w1, w3, w2)


def _combine_kernel(pos_ref, ys_hbm, x1t_ref, rc_ref, lg_ref, lb_ref, o_ref, ob_ref, gath, sems):
    tm = o_ref.shape[0]
    t = pos_ref.shape[0] // TOP_K
    i = pl.program_id(0)
    ntile = pl.num_programs(0)

    def issue(tile, slot):
        def body(j, _):
            for k in range(TOP_K):
                pltpu.make_async_copy(_token_tile(ys_hbm, pos_ref[k * t + tile * tm + j]),
                                      _token_tile(gath.at[slot, k], j), sems.at[slot]).start(priority=k)
            return 0
        lax.fori_loop(0, tm, body, 0, unroll=8)

    @pl.when(i == 0)
    def _():
        issue(0, 0)

    @pl.when(i + 1 < ntile)
    def _():
        issue(i + 1, (i + 1) % 2)

    slot = i % 2
    for k in range(TOP_K):
        pltpu.make_async_copy(ys_hbm.at[pl.ds(0, tm * ROW_TILES), :], gath.at[slot, k], sems.at[slot]).wait()
    rc = rc_ref[...]
    y = (rc[:, 0:1] * _load_token_tiles(gath.at[slot, 0], tm)
         + rc[:, 1:2] * _load_token_tiles(gath.at[slot, 1], tm))
    x2 = _layer_norm(DEEPNORM_ALPHA * _load_token_tiles(x1t_ref, tm) + y, lg_ref[...], lb_ref[...])
    o_ref[...] = x2
    ob_ref[...] = x2.astype(BF16)


def _combine(pos, ys, x1t, rcols, lg, lb, tm=256):
    t = x1t.shape[0] // ROW_TILES
    row = lambda w: pl.BlockSpec((tm, w), lambda i, pos: (i, 0))
    full = lambda shape: pl.BlockSpec(shape, lambda i, pos: (0,) * len(shape))
    return pl.pallas_call(
        _combine_kernel,
        grid_spec=pltpu.PrefetchScalarGridSpec(
            num_scalar_prefetch=1, grid=(t // tm,),
            in_specs=[pl.BlockSpec(memory_space=pl.ANY),
                      pl.BlockSpec((tm * ROW_TILES, LANES), lambda i, pos: (i, 0)), row(LANES),
                      full((1, D_MODEL)), full((1, D_MODEL))],
            out_specs=[row(D_MODEL), row(D_MODEL)],
            scratch_shapes=[pltpu.VMEM((2, TOP_K, tm * ROW_TILES, LANES), F32), pltpu.SemaphoreType.DMA((2,))]),
        out_shape=[jax.ShapeDtypeStruct((t, D_MODEL), F32), jax.ShapeDtypeStruct((t, D_MODEL), BF16)],
        compiler_params=_cparams(("arbitrary",)),
        name="moe_combine",
    )(pos, ys, x1t, rcols, lg, lb)


def _moe(x1t, rcols, ids, layer, w1, w3, w2, lg, lb):
    pos8, meta = _plan(ids)
    pos = pos8[:TOP_K].reshape(-1)
    nt_max = _moe_rows(x1t.shape[0] // ROW_TILES) // MOE_TILE
    ntiles = meta[1, :1]
    xs = _dispatch(pos, meta[2, :N_EXPERTS], ntiles, x1t)
    ys = _ffn(meta[0, :nt_max], ntiles, xs, layer, w1, w3, w2)
    return _combine(pos, ys, x1t, rcols, lg, lb)


def _split_hi_lo(w):
    hi = w.astype(BF16)
    return hi, (w - hi.astype(F32)).astype(BF16)


IN_AX, IN_AGATE, IN_BZ, IN_XBC = 0, D_MODEL, 2 * D_MODEL, 3 * D_MODEL
IN_DT = IN_XBC + SSM_CONV_CH
IN_QKV = IN_DT + SSM_HEADS
IN_F = IN_QKV + 3 * D_MODEL
IN_GATE = IN_F + ATTN_HEADS
P_IN = IN_GATE + 3 * D_MODEL
W_PREP_COLS = D_MODEL
W_PREP_SRC = (IN_QKV, IN_QKV + D_MODEL, IN_QKV + 2 * D_MODEL, IN_GATE, IN_GATE + D_MODEL, IN_GATE + 2 * D_MODEL,
              IN_XBC, IN_XBC + D_MODEL, IN_AX, IN_AGATE, IN_BZ)


def _w_prep_kernel(src_ref, wt_hbm, main_ref, small_ref, buf, sbuf, sems):
    layer = pl.program_id(0)
    j = pl.program_id(1)
    nblk = len(W_PREP_SRC)
    nstep = wt_hbm.shape[0] * nblk
    step = layer * nblk + j

    def block_copy(s):
        rows = pl.ds(pl.multiple_of(src_ref[s % nblk], SUBLANES), W_PREP_COLS)
        return pltpu.make_async_copy(wt_hbm.at[s // nblk, rows, :], buf.at[s % 2], sems.at[s % 2])

    @pl.when(step == 0)
    def _():
        block_copy(step).start()

    @pl.when(step + 1 < nstep)
    def _():
        block_copy(step + 1).start()

    @pl.when(j == 0)
    def _():
        sbuf[...] = jnp.zeros(sbuf.shape, F32)
        parts = [pltpu.make_async_copy(wt_hbm.at[layer, pl.ds(src, n), :], sbuf.at[pl.ds(dst, n), :], sems.at[2])
                 for src, dst, n in ((IN_DT, 0, SSM_HEADS), (IN_F, SSM_HEADS, ATTN_HEADS))]
        for c in parts:
            c.start()
        for c in parts:
            c.wait()
        small_ref[0] = sbuf[...].T.astype(BF16)

    block_copy(step).wait()
    scale = jnp.where(j == 0, ATTN_HEAD_DIM ** -0.5 * LOG2E, 1.0)
    main_ref[0] = (buf[step % 2].T * scale).astype(BF16)


def _w_prep(w_in):
    depth, k, n = w_in.shape
    assert n == P_IN and COL_QKV == 0 and len(W_PREP_SRC) * W_PREP_COLS == N_MAIN
    w_t = jnp.swapaxes(w_in, 1, 2)
    return pl.pallas_call(
        _w_prep_kernel,
        grid_spec=pltpu.PrefetchScalarGridSpec(
            num_scalar_prefetch=1, grid=(depth, len(W_PREP_SRC)),
            in_specs=[pl.BlockSpec(memory_space=pl.ANY)],
            out_specs=[pl.BlockSpec((1, k, W_PREP_COLS), lambda l, j, src: (l, 0, j)),
                       pl.BlockSpec((1, k, N_SMALL), lambda l, j, src: (l, 0, 0))],
            scratch_shapes=[pltpu.VMEM((2, W_PREP_COLS, k), F32), pltpu.VMEM((N_SMALL, k), F32),
                            pltpu.SemaphoreType.DMA((3,))]),
        out_shape=[jax.ShapeDtypeStruct((depth, k, N_MAIN), BF16), jax.ShapeDtypeStruct((depth, k, N_SMALL), BF16)],
        compiler_params=_cparams(("arbitrary", "arbitrary")),
        name="w_prep",
    )(jnp.asarray(W_PREP_SRC, jnp.int32), w_t)


def _prepare(w_in, gate_b, conv_a_w, conv_a_b, lru_wa, lru_ba, lru_wx, lru_bx, lru_lambda,
             conv_b_w, conv_b_b, dt_bias, a_log, d_skip, ssm_norm_w, forget_b,
             w_branch_a, w_branch_b, w_branch_c, w_out, ln1_g, ln1_b,
             router_w, router_b, w1, w3, w2, ln2_g, ln2_b):
    w_main, w_small = _w_prep(w_in)
    w_gates = jnp.concatenate([lru_wa, lru_wx], axis=-1).astype(BF16)
    pad_heads = lambda v, off: jnp.pad(v, ((0, 0), (off, N_SMALL - off - v.shape[1])))[:, None, :]
    dtb_p = pad_heads(dt_bias, 0)
    alog_p = pad_heads(a_log, 0)
    fb_p = pad_heads(forget_b, SSM_HEADS)
    dskip_e = jnp.repeat(d_skip, SSM_HEAD_DIM, axis=-1)[:, None, :]
    row = lambda v: v[:, None, :]
    wa_b, wb_b, wc_b, wo_b = (w.astype(BF16) for w in (w_branch_a, w_branch_b, w_branch_c, w_out))
    rwh, rwl = _split_hi_lo(router_w.T)
    return dict(
        w_main=w_main, w_small=w_small, conv_a_w=conv_a_w, conv_a_b=row(conv_a_b), w_gates=w_gates,
        lru_ba=row(lru_ba), lru_bx=row(lru_bx), lru_lambda=row(lru_lambda),
        conv_b_w=conv_b_w, conv_b_b=row(conv_b_b), dtb=dtb_p, alog=alog_p, dskip=dskip_e,
        ssm_norm_w=row(ssm_norm_w), fb=fb_p, wa=wa_b, wb=wb_b, wc=wc_b, wo=wo_b, gate_b=row(gate_b),
        ln1_g=row(ln1_g), ln1_b=row(ln1_b), rwh=rwh, rwl=rwl, rb=router_b[:, None],
        w1=w1, w3=w3, w2=w2, ln2_g=row(ln2_g), ln2_b=row(ln2_b))


def _layer(l, xf, xb, p, bsz, seq):
    proj, small = _in_proj(xb, p['w_main'], p['w_small'], l)
    ha = _branch_a(proj, p['conv_a_w'][l], p['conv_a_b'][l], p['w_gates'][l], p['lru_ba'][l], p['lru_bx'][l],
                   p['lru_lambda'][l], bsz, seq)
    hb = _branch_b(proj, small, p['conv_b_w'][l], p['conv_b_b'][l], p['dtb'][l], p['alog'][l], p['dskip'][l],
                   p['ssm_norm_w'][l], bsz, seq)
    cum = _fox_cum(small, p['fb'][l], bsz, seq)
    hc = _branch_c(proj, cum, bsz, seq)
    x1, rcols, ids = _merge(ha, hb, hc, proj, xf, p['wa'][l], p['wb'][l], p['wc'][l], p['wo'][l],
                            p['gate_b'][l], p['ln1_g'][l], p['ln1_b'][l], p['rwh'], p['rwl'], p['rb'])
    x2, x2b = _moe(x1, rcols, ids, l, p['w1'], p['w3'], p['w2'], p['ln2_g'][l], p['ln2_b'][l])
    return dict(proj=proj, small=small, ha=ha, hb=hb, cum=cum, hc=hc, x1=x1, rcols=rcols, ids=ids,
                x2=x2, x2b=x2b)


def kernel(x, w_in, gate_b, conv_a_w, conv_a_b, lru_wa, lru_ba, lru_wx, lru_bx, lru_lambda,
           conv_b_w, conv_b_b, dt_bias, a_log, d_skip, ssm_norm_w, forget_b,
           w_branch_a, w_branch_b, w_branch_c, w_out, ln1_g, ln1_b,
           router_w, router_b, w1, w3, w2, ln2_g, ln2_b):
    bsz, seq, d = x.shape
    p = _prepare(w_in, gate_b, conv_a_w, conv_a_b, lru_wa, lru_ba, lru_wx, lru_bx, lru_lambda,
                 conv_b_w, conv_b_b, dt_bias, a_log, d_skip, ssm_norm_w, forget_b,
                 w_branch_a, w_branch_b, w_branch_c, w_out, ln1_g, ln1_b,
                 router_w, router_b, w1, w3, w2, ln2_g, ln2_b)
    xf = x.reshape(bsz * seq, d)
    xb = xf.astype(BF16)
    for l in range(w_in.shape[0]):
        stages = _layer(l, xf, xb, p, bsz, seq)
        xf, xb = stages['x2'], stages['x2b']
    return xf.reshape(bsz, seq, d)
```
